```python
import math
import jax, jax.numpy as jnp
from jax import lax
import numpy as np

D_MODEL = 2048
BATCH = 1
SEQ = 16384
DEPTH = 2

MIX_WIDTH = D_MODEL
DIFF_HEADS = 8
DIFF_DIM = 64
DIFF_WIDTH = DIFF_HEADS * 2 * DIFF_DIM
DIL_HEADS = 8
DIL_DIM = 128
DIL_WIDTH = DIL_HEADS * DIL_DIM
DIL_BRANCHES = ((128, 1), (512, 4), (2048, 16))
QKV_WIDTH = 3 * DIFF_WIDTH + 3 * DIL_WIDTH
ROPE_THETA = 10000.0
Q_BLOCK = 128
D_FF = 5632
N_EXPERTS = 8
TOP_K = 2
D_EXPERT = 7168
MOE_BLOCK = 256
EPS = 1e-6
NEG = -1e30

kernel_name = "hymba_diffattn_dilated_moe_encoder"


def rms_norm(x, g):
    xf = x.astype(jnp.float32)
    y = xf * lax.rsqrt(jnp.mean(xf * xf, axis=-1, keepdims=True) + EPS)
    return (y * g.astype(jnp.float32)).astype(x.dtype)


def rope_tables(seq, dim):
    pos = jnp.arange(seq, dtype=jnp.float32)
    inv = ROPE_THETA ** (-jnp.arange(0, dim, 2, dtype=jnp.float32) / dim)
    ang = pos[:, None] * inv[None, :]
    ang = jnp.concatenate([ang, ang], axis=-1)
    return jnp.cos(ang), jnp.sin(ang)


def apply_rope(x, cos, sin):
    xf = x.astype(jnp.float32)
    x1, x2 = jnp.split(xf, 2, axis=-1)
    rot = jnp.concatenate([-x2, x1], axis=-1)
    return (xf * cos + rot * sin).astype(x.dtype)


def diff_attention(q, k, v, lam, lam_init, q_g, k_g, o_g, cos, sin):
    c, s_ = cos[:, None, None, :], sin[:, None, None, :]
    q = apply_rope(rms_norm(q, q_g), c, s_) * (DIFF_DIM ** -0.5)
    k = apply_rope(rms_norm(k, k_g), c, s_)
    q = q.transpose(0, 2, 3, 1, 4)
    k = k.transpose(0, 2, 3, 1, 4)
    v = v.transpose(0, 2, 1, 3)
    B, H, _, S, Dd = q.shape
    nq = S // Q_BLOCK
    qb = jnp.moveaxis(q.reshape(B, H, 2, nq, Q_BLOCK, Dd), 3, 0)

    def block(qi):
        sc = jnp.einsum('bhmqd,bhmkd->bhmqk', qi, k).astype(jnp.float32)
        p = jax.nn.softmax(sc, axis=-1)
        a = p[:, :, 0] - lam * p[:, :, 1]
        return jnp.einsum('bhqk,bhkd->bhqd', a.astype(v.dtype), v)

    o = lax.map(block, qb)
    o = jnp.moveaxis(o, 0, 2).reshape(B, H, S, 2 * Dd)
    o = rms_norm(o, o_g) * (1.0 - lam_init)
    return o.transpose(0, 2, 1, 3).reshape(B, S, H * 2 * Dd)


def dilated_branch(q, k, v, window, dil):
    n_side = window // (2 * dil)
    blk = n_side
    B, H, S, D = q.shape
    L = S // dil
    nb = -(-L // blk)
    Lp = nb * blk

    def strided(t):
        return t.reshape(B, H, L, dil, D).transpose(0, 1, 3, 2, 4)

    qs = jnp.pad(strided(q), ((0, 0), (0, 0), (0, 0), (0, Lp - L), (0, 0)))
    ks = jnp.pad(strided(k), ((0, 0), (0, 0), (0, 0), (blk, Lp - L + blk), (0, 0)))
    vs = jnp.pad(strided(v), ((0, 0), (0, 0), (0, 0), (blk, Lp - L + blk), (0, 0)))
    qb = qs.reshape(B, H, dil, nb, blk, D)
    kidx = jnp.arange(nb)[:, None] * blk + jnp.arange(3 * blk)[None, :]
    kb = ks[:, :, :, kidx, :]
    vb = vs[:, :, :, kidx, :]
    m_pos = jnp.arange(nb)[:, None] * blk + jnp.arange(blk)[None, :]
    n_pos = kidx - blk
    valid = ((n_pos[:, None, :] >= 0) & (n_pos[:, None, :] < L)
             & (jnp.abs(m_pos[:, :, None] - n_pos[:, None, :]) <= n_side))
    sc = jnp.einsum('bhrnqd,bhrnkd->bhrnqk', qb, kb).astype(jnp.float32)
    sc = jnp.where(valid, sc, NEG)
    mx = jnp.max(sc, axis=-1, keepdims=True)
    p = jnp.exp(sc - mx)
    den = jnp.sum(p, axis=-1, keepdims=True)
    o = jnp.einsum('bhrnqk,bhrnkd->bhrnqd', (p / den).astype(v.dtype), vb)
    lse = (mx + jnp.log(den))[..., 0]
    o = o.reshape(B, H, dil, Lp, D)[:, :, :, :L].transpose(0, 1, 3, 2, 4).reshape(B, H, S, D)
    lse = lse.reshape(B, H, dil, Lp)[..., :L].transpose(0, 1, 3, 2).reshape(B, H, S)
    return o, lse


def dilated_attention(q, k, v, q_g, k_g, o_g, cos, sin):
    c, s_ = cos[:, None, :], sin[:, None, :]
    q = apply_rope(rms_norm(q, q_g), c, s_) * (DIL_DIM ** -0.5)
    k = apply_rope(rms_norm(k, k_g), c, s_)
    q, k, v = (t.transpose(0, 2, 1, 3) for t in (q, k, v))
    outs, lses = [], []
    for window, dil in DIL_BRANCHES:
        o, l = dilated_branch(q, k, v, window, dil)
        outs.append(o)
        lses.append(l)
    w = jax.nn.softmax(jnp.stack(lses), axis=0)
    o = jnp.einsum('gbhs,gbhsd->bhsd', w.astype(v.dtype), jnp.stack(outs))
    o = rms_norm(o, o_g)
    B, H, S, D = o.shape
    return o.transpose(0, 2, 1, 3).reshape(B, S, H * D)


def dense_swiglu(h, wg, wu, wd):
    return (jax.nn.silu(h @ wg) * (h @ wu)) @ wd


def moe_swiglu(h, router, wg, wu, wd):
    B, S, D = h.shape
    T = B * S
    xt = h.reshape(T, D)
    logits = xt.astype(jnp.float32) @ router.astype(jnp.float32)
    top_val, top_idx = lax.top_k(logits, TOP_K)
    gates = jax.nn.softmax(top_val, axis=-1)
    n_assign = T * TOP_K
    e_flat = top_idx.reshape(-1).astype(jnp.int32)
    tok_flat = jnp.repeat(jnp.arange(T, dtype=jnp.int32), TOP_K)
    g_flat = gates.reshape(-1)
    order = jnp.argsort(e_flat)
    e_s, tok_s, g_s = e_flat[order], tok_flat[order], g_flat[order]
    counts = jnp.bincount(e_flat, length=N_EXPERTS)
    starts = jnp.cumsum(counts) - counts
    nblk_e = (counts + MOE_BLOCK - 1) // MOE_BLOCK
    cum_blk = jnp.cumsum(nblk_e)
    pad_starts = (cum_blk - nblk_e) * MOE_BLOCK
    pos = pad_starts[e_s] + jnp.arange(n_assign, dtype=jnp.int32) - starts[e_s]
    n_blocks = -(-n_assign // MOE_BLOCK) + N_EXPERTS
    tok_buf = jnp.full((n_blocks * MOE_BLOCK,), T, jnp.int32).at[pos].set(tok_s)
    g_buf = jnp.zeros((n_blocks * MOE_BLOCK,), jnp.float32).at[pos].set(g_s)
    blk_expert = jnp.minimum(jnp.searchsorted(cum_blk, jnp.arange(n_blocks), side='right'),
                             N_EXPERTS - 1)
    x_pad = jnp.concatenate([xt, jnp.zeros((1, D), xt.dtype)], axis=0)

    def run(args):
        tok, e = args
        xb = x_pad[tok]
        return (jax.nn.silu(xb @ wg[e]) * (xb @ wu[e])) @ wd[e]

    ys = lax.map(run, (tok_buf.reshape(n_blocks, MOE_BLOCK), blk_expert))
    y = jnp.zeros((T + 1, D), h.dtype).at[tok_buf].add(
        ys.reshape(-1, D) * g_buf[:, None].astype(h.dtype))
    return y[:T].reshape(B, S, D)


def setup_inputs(seed: int = 0) -> dict:
    key = jax.random.key(seed)
    keys = iter(jax.random.split(key, 64))

    def nrm(shape, scale):
        return jax.random.normal(next(keys), shape, jnp.float32) * scale

    def gain(n):
        return 1.0 + nrm((n,), 0.02)

    d = {"x": nrm((BATCH, SEQ, D_MODEL), 1.0)}
    for i in range(DEPTH):
        d[f"attn_norm_{i}"] = gain(D_MODEL)
        d[f"w_in_{i}"] = nrm((D_MODEL, QKV_WIDTH), D_MODEL ** -0.5)
        d[f"diff_q_norm_{i}"] = gain(DIFF_DIM)
        d[f"diff_k_norm_{i}"] = gain(DIFF_DIM)
        d[f"diff_lam_q1_{i}"] = nrm((DIFF_DIM,), 0.1)
        d[f"diff_lam_k1_{i}"] = nrm((DIFF_DIM,), 0.1)
        d[f"diff_lam_q2_{i}"] = nrm((DIFF_DIM,), 0.1)
        d[f"diff_lam_k2_{i}"] = nrm((DIFF_DIM,), 0.1)
        d[f"diff_out_norm_{i}"] = gain(2 * DIFF_DIM)
        d[f"dil_q_norm_{i}"] = gain(DIL_DIM)
        d[f"dil_k_norm_{i}"] = gain(DIL_DIM)
        d[f"dil_out_norm_{i}"] = gain(DIL_DIM)
        d[f"w_out_{i}"] = nrm((MIX_WIDTH, D_MODEL), MIX_WIDTH ** -0.5)
        d[f"ffn_norm_{i}"] = gain(D_MODEL)
        if i % 2 == 0:
            d[f"ffn_w_gate_{i}"] = nrm((D_MODEL, D_FF), D_MODEL ** -0.5)
            d[f"ffn_w_up_{i}"] = nrm((D_MODEL, D_FF), D_MODEL ** -0.5)
            d[f"ffn_w_down_{i}"] = nrm((D_FF, D_MODEL), D_FF ** -0.5)
        else:
            d[f"router_{i}"] = nrm((D_MODEL, N_EXPERTS), D_MODEL ** -0.5)
            d[f"moe_w_gate_{i}"] = nrm((N_EXPERTS, D_MODEL, D_EXPERT), D_MODEL ** -0.5)
            d[f"moe_w_up_{i}"] = nrm((N_EXPERTS, D_MODEL, D_EXPERT), D_MODEL ** -0.5)
            d[f"moe_w_down_{i}"] = nrm((N_EXPERTS, D_EXPERT, D_MODEL), D_EXPERT ** -0.5)
    return d


def reference(x,
              attn_norm_0, w_in_0, diff_q_norm_0, diff_k_norm_0, diff_lam_q1_0, diff_lam_k1_0,
              diff_lam_q2_0, diff_lam_k2_0, diff_out_norm_0, dil_q_norm_0, dil_k_norm_0,
              dil_out_norm_0, w_out_0, ffn_norm_0, ffn_w_gate_0, ffn_w_up_0, ffn_w_down_0,
              attn_norm_1, w_in_1, diff_q_norm_1, diff_k_norm_1, diff_lam_q1_1, diff_lam_k1_1,
              diff_lam_q2_1, diff_lam_k2_1, diff_out_norm_1, dil_q_norm_1, dil_k_norm_1,
              dil_out_norm_1, w_out_1, ffn_norm_1, router_1, moe_w_gate_1, moe_w_up_1,
              moe_w_down_1):
    mix_params = [
        (attn_norm_0, w_in_0, diff_q_norm_0, diff_k_norm_0, diff_lam_q1_0, diff_lam_k1_0,
         diff_lam_q2_0, diff_lam_k2_0, diff_out_norm_0, dil_q_norm_0, dil_k_norm_0,
         dil_out_norm_0, w_out_0, ffn_norm_0),
        (attn_norm_1, w_in_1, diff_q_norm_1, diff_k_norm_1, diff_lam_q1_1, diff_lam_k1_1,
         diff_lam_q2_1, diff_lam_k2_1, diff_out_norm_1, dil_q_norm_1, dil_k_norm_1,
         dil_out_norm_1, w_out_1, ffn_norm_1),
    ]
    ffn_params = [(ffn_w_gate_0, ffn_w_up_0, ffn_w_down_0),
                  (router_1, moe_w_gate_1, moe_w_up_1, moe_w_down_1)]
    B, S, _ = x.shape
    cos_a, sin_a = rope_tables(S, DIFF_DIM)
    cos_b, sin_b = rope_tables(S, DIL_DIM)
    split_at = [int(v) for v in np.cumsum([DIFF_WIDTH] * 3 + [DIL_WIDTH] * 3)[:-1]]
    for i in range(DEPTH):
        (a_norm, w_in, dqn, dkn, lq1, lk1, lq2, lk2, don, bqn, bkn, bon, w_out,
         f_norm) = mix_params[i]
        lam_init = 0.8 - 0.6 * math.exp(-0.3 * i)
        lam = (jnp.exp(jnp.sum(lq1.astype(jnp.float32) * lk1.astype(jnp.float32)))
               - jnp.exp(jnp.sum(lq2.astype(jnp.float32) * lk2.astype(jnp.float32)))
               + lam_init)
        h = rms_norm(x, a_norm)
        proj = h @ w_in
        qa, ka, va, qb, kb, vb = jnp.split(proj, split_at, axis=-1)
        out_a = diff_attention(qa.reshape(B, S, DIFF_HEADS, 2, DIFF_DIM),
                               ka.reshape(B, S, DIFF_HEADS, 2, DIFF_DIM),
                               va.reshape(B, S, DIFF_HEADS, 2 * DIFF_DIM),
                               lam, lam_init, dqn, dkn, don, cos_a, sin_a)
        out_b = dilated_attention(qb.reshape(B, S, DIL_HEADS, DIL_DIM),
                                  kb.reshape(B, S, DIL_HEADS, DIL_DIM),
                                  vb.reshape(B, S, DIL_HEADS, DIL_DIM),
                                  bqn, bkn, bon, cos_b, sin_b)
        x = x + jnp.concatenate([out_a, out_b], axis=-1) @ w_out
        h = rms_norm(x, f_norm)
        if i % 2 == 0:
            x = x + dense_swiglu(h, *ffn_params[i])
        else:
            x = x + moe_swiglu(h, *ffn_params[i])
    return x
```

```python
import functools
import math

import numpy as np
import jax
import jax.numpy as jnp
from jax import lax
from jax.experimental import pallas as pl
from jax.experimental.pallas import tpu as pltpu

F32 = jnp.float32
BF16 = jnp.bfloat16

D_MODEL = 2048
N_HEADS = 8
HEAD_W = 128
DIFF_DIM = 64
SEG_W = N_HEADS * HEAD_W
N_SEG = 6
DIL_BRANCHES = ((128, 1), (512, 4), (2048, 16))
N_SIDE = 64
D_FF = 5632
N_EXPERTS = 8
D_EXPERT = 7168
EPS = 1e-6
NEG = -1e30
LOG2E = 1.4426950408889634

V_ROWS = 144
DIL_PAD = 1024
VMEM_LIMIT = 52 * 1024 * 1024

ROW_TILE = 512
ATT_TQ = 512
ATT_TK = 512
DIL_TQ = 256
DIL_CHUNK = 768
FF_TILE = 512
MOE_TILE = 512
GATHER_ROWS = 512


def _cparams(sem):
    return pltpu.CompilerParams(dimension_semantics=sem, vmem_limit_bytes=VMEM_LIMIT)


def _diff_lane_perm():
    perm = np.zeros(HEAD_W, np.int32)
    for m in range(2):
        for t in range(DIFF_DIM):
            p = (t // 32) * 64 + m * 32 + (t % 32)
            perm[p] = m * DIFF_DIM + t
    return perm


def _segment_tables(seq, dqn, dkn, bqn, bkn):
    perm = _diff_lane_perm()
    t_of_lane = perm % DIFF_DIM
    ones = jnp.ones((SEG_W,), F32)
    g_qa = jnp.tile(dqn.astype(F32)[t_of_lane], N_HEADS)
    g_ka = jnp.tile(dkn.astype(F32)[t_of_lane], N_HEADS)
    g_qb = jnp.tile(bqn.astype(F32), N_HEADS)
    g_kb = jnp.tile(bkn.astype(F32), N_HEADS)
    gains = jnp.stack([g_qa, g_ka, ones, g_qb, g_kb, ones]).reshape(N_SEG, 1, SEG_W)

    lane = np.arange(HEAD_W)
    map_of_lane = (lane // 32) % 2
    g_diff = (map_of_lane[:, None] == map_of_lane[None, :]).astype(np.float32)
    g_dil = np.ones((HEAD_W, HEAD_W), np.float32)
    gmat = jnp.asarray(np.stack([g_diff, g_dil]), BF16)

    pos = jnp.arange(seq, dtype=F32)

    def ang(dim):
        inv = 10000.0 ** (-jnp.arange(0, dim, 2, dtype=F32) / dim)
        return pos[:, None] * inv[None, :]

    a32 = ang(DIFF_DIM)
    a64 = ang(HEAD_W)
    cos_a = jnp.tile(jnp.cos(a32), (1, 4))
    sin_a = jnp.tile(jnp.sin(a32), (1, 4))
    cos_b = jnp.tile(jnp.cos(a64), (1, 2))
    sin_b = jnp.tile(jnp.sin(a64), (1, 2))
    sign = jnp.where(jnp.arange(HEAD_W) < 64, -1.0, 1.0).astype(F32)[None, :]
    cos_t = jnp.stack([cos_a, cos_b])
    sin_t = jnp.stack([sin_a * sign, sin_b * sign])
    return gains, gmat, cos_t, sin_t


def _permute_w_in(w_in):
    perm = _diff_lane_perm()
    cols = np.arange(N_SEG * SEG_W)
    for seg in (0, 1):
        for h in range(N_HEADS):
            base = seg * SEG_W + h * HEAD_W
            cols[base:base + HEAD_W] = base + perm
    return w_in[:, cols].astype(BF16)


def _inproj_kernel(x_ref, g_ref, w_ref, gain_ref, cos_ref, sin_ref, gmat_ref,
                   qaT_ref, ka_ref, vaT_ref, qbT_ref, kb_ref, vbT_ref, h_scr,
                   *, n_row_blocks, pad_blocks):
    ip = pl.program_id(0)
    j = pl.program_id(1)
    real = jnp.logical_and(ip >= pad_blocks, ip < pad_blocks + n_row_blocks)
    tm = x_ref.shape[0]

    @pl.when(jnp.logical_and(real, j == 0))
    def _():
        x = x_ref[...]
        ms = jnp.mean(x * x, axis=-1, keepdims=True)
        h_scr[...] = (x * lax.rsqrt(ms + EPS) * g_ref[...]).astype(BF16)

    def norm_rope(y, c, n_group, scale):
        sq = y * y
        sq_hi = sq.astype(BF16)
        sq_lo = (sq - sq_hi.astype(F32)).astype(BF16)
        gm = gmat_ref[...]
        ss = (jnp.dot(sq_hi, gm, preferred_element_type=F32)
              + jnp.dot(sq_lo, gm, preferred_element_type=F32))
        yn = y * lax.rsqrt(ss * (1.0 / n_group) + EPS) * gain_ref[:, c * HEAD_W:(c + 1) * HEAD_W]
        out = yn * cos_ref[...] + pltpu.roll(yn, 64, 1) * sin_ref[...]
        if scale != 1.0:
            out = out * scale
        return out

    def aug_rows():
        row = lax.broadcasted_iota(jnp.int32, (V_ROWS - HEAD_W, tm), 0)
        return jnp.where(row == 0, 1.0, 0.0).astype(BF16)

    def segment(seg):
        acc = jnp.dot(h_scr[...], w_ref[...], preferred_element_type=F32)
        for c in range(N_HEADS):
            y = acc[:, c * HEAD_W:(c + 1) * HEAD_W]
            if seg == 0:
                qaT_ref[c] = norm_rope(y, c, DIFF_DIM, DIFF_DIM ** -0.5 * LOG2E).T.astype(BF16)
            elif seg == 1:
                ka_ref[:, c * HEAD_W:(c + 1) * HEAD_W] = norm_rope(y, c, DIFF_DIM, 1.0).astype(BF16)
            elif seg == 3:
                qbT_ref[c] = norm_rope(y, c, HEAD_W, HEAD_W ** -0.5 * LOG2E).T.astype(BF16)
            elif seg == 4:
                kb_ref[:, c * HEAD_W:(c + 1) * HEAD_W] = norm_rope(y, c, HEAD_W, 1.0).astype(BF16)
            else:
                ref = vaT_ref if seg == 2 else vbT_ref
                ref[c, 0:HEAD_W, :] = y.T.astype(BF16)
                ref[c, HEAD_W:V_ROWS, :] = aug_rows()

    for seg in range(N_SEG):
        pl.when(jnp.logical_and(real, j == seg))(functools.partial(segment, seg))

    @pl.when(jnp.logical_and(jnp.logical_not(real), j == 4))
    def _():
        kb_ref[...] = jnp.zeros(kb_ref.shape, BF16)

    @pl.when(jnp.logical_and(jnp.logical_not(real), j == 5))
    def _():
        vbT_ref[...] = jnp.zeros(vbT_ref.shape, BF16)


def _inproj(x, a_norm, w_in_p, gains, gmat, cos_t, sin_t):
    seq = x.shape[0]
    tm = ROW_TILE
    nrb = seq // tm
    pb = DIL_PAD // tm
    seq_p = seq + 2 * DIL_PAD

    def row(ip):
        return jnp.clip(ip - pb, 0, nrb - 1)

    kern = functools.partial(_inproj_kernel, n_row_blocks=nrb, pad_blocks=pb)
    out_shape = (
        jax.ShapeDtypeStruct((N_HEADS, HEAD_W, seq), BF16),
        jax.ShapeDtypeStruct((seq, SEG_W), BF16),
        jax.ShapeDtypeStruct((N_HEADS, V_ROWS, seq), BF16),
        jax.ShapeDtypeStruct((N_HEADS, HEAD_W, seq), BF16),
        jax.ShapeDtypeStruct((seq_p, SEG_W), BF16),
        jax.ShapeDtypeStruct((N_HEADS, V_ROWS, seq_p), BF16),
    )
    in_specs = [
        pl.BlockSpec((tm, D_MODEL), lambda ip, j: (row(ip), 0)),
        pl.BlockSpec((1, D_MODEL), lambda ip, j: (0, 0)),
        pl.BlockSpec((D_MODEL, SEG_W), lambda ip, j: (0, j)),
        pl.BlockSpec((None, 1, SEG_W), lambda ip, j: (j, 0, 0)),
        pl.BlockSpec((None, tm, HEAD_W), lambda ip, j: (j // 3, row(ip), 0)),
        pl.BlockSpec((None, tm, HEAD_W), lambda ip, j: (j // 3, row(ip), 0)),
        pl.BlockSpec((None, HEAD_W, HEAD_W), lambda ip, j: (j // 3, 0, 0)),
    ]
    out_specs = (
        pl.BlockSpec((N_HEADS, HEAD_W, tm), lambda ip, j: (0, 0, row(ip))),
        pl.BlockSpec((tm, SEG_W), lambda ip, j: (row(ip), 0)),
        pl.BlockSpec((N_HEADS, V_ROWS, tm), lambda ip, j: (0, 0, row(ip))),
        pl.BlockSpec((N_HEADS, HEAD_W, tm), lambda ip, j: (0, 0, row(ip))),
        pl.BlockSpec((tm, SEG_W), lambda ip, j: (ip, 0)),
        pl.BlockSpec((N_HEADS, V_ROWS, tm), lambda ip, j: (0, 0, ip)),
    )
    return pl.pallas_call(
        kern,
        out_shape=out_shape,
        grid=(nrb + 2 * pb, N_SEG),
        in_specs=in_specs,
        out_specs=out_specs,
        scratch_shapes=[pltpu.VMEM((tm, D_MODEL), BF16)],
        compiler_params=_cparams(("arbitrary", "arbitrary")),
        name="inproj",
    )(x, a_norm.reshape(1, D_MODEL).astype(F32), w_in_p, gains, cos_t, sin_t, gmat)


def _flash_step(s, m_old, acc_ref, vT_t):
    m_new = jnp.maximum(m_old, jnp.max(s, axis=0, keepdims=True))
    alpha = jnp.exp2(m_old - m_new)
    p = jnp.exp2((s - m_new).astype(BF16))
    acc_ref[...] = acc_ref[...] * alpha + jnp.dot(vT_t, p, preferred_element_type=F32)
    return m_new


def _head_out(aT, gain_row, out_scale):
    a = aT.T
    ms = jnp.mean(a * a, axis=-1, keepdims=True)
    y = a * lax.rsqrt(ms + EPS) * gain_row
    if out_scale != 1.0:
        y = y * out_scale
    return y


def _diff_attn_kernel(lq1_ref, lk1_ref, lq2_ref, lk2_ref, qT_ref, k_ref, vT_ref, og_ref, o_ref,
                      acc0, acc1, *, lam_init, tk):
    tq = qT_ref.shape[1]
    seq = k_ref.shape[0]
    qT = qT_ref[...]
    row = lax.broadcasted_iota(jnp.int32, qT.shape, 0)
    in_map1 = ((row // 32) % 2) == 1
    zero = jnp.zeros_like(qT)
    q0 = jnp.where(in_map1, zero, qT)
    q1 = jnp.where(in_map1, qT, zero)
    acc0[...] = jnp.zeros(acc0.shape, F32)
    acc1[...] = jnp.zeros(acc1.shape, F32)

    def body(kt, carry):
        m0, m1 = carry
        start = pl.multiple_of(kt * tk, tk)
        k_t = k_ref[pl.ds(start, tk), :]
        vT_t = vT_ref[:, pl.ds(start, tk)]
        s0 = jnp.dot(k_t, q0, preferred_element_type=F32)
        m0 = _flash_step(s0, m0, acc0, vT_t)
        s1 = jnp.dot(k_t, q1, preferred_element_type=F32)
        m1 = _flash_step(s1, m1, acc1, vT_t)
        return m0, m1

    m_init = jnp.full((1, tq), NEG, F32)
    lax.fori_loop(0, seq // tk, body, (m_init, m_init))

    lam = (jnp.exp(jnp.sum(lq1_ref[...] * lk1_ref[...], axis=-1, keepdims=True))
           - jnp.exp(jnp.sum(lq2_ref[...] * lk2_ref[...], axis=-1, keepdims=True)) + lam_init)
    o0 = acc0[0:HEAD_W, :] * (1.0 / acc0[HEAD_W:HEAD_W + 1, :])
    o1 = acc1[0:HEAD_W, :] * (1.0 / acc1[HEAD_W:HEAD_W + 1, :])
    aT = o0 - lam * o1
    o_ref[...] = _head_out(aT, og_ref[...], 1.0 - lam_init).astype(o_ref.dtype)


def _diff_attn(qaT, ka, vaT, lq1, lk1, lq2, lk2, og, lam_init):
    seq = ka.shape[0]
    tq = min(ATT_TQ, seq)
    tk = min(ATT_TK, seq)
    kern = functools.partial(_diff_attn_kernel, lam_init=lam_init, tk=tk)
    vec = lambda v: v.reshape(1, -1).astype(F32)
    small = pl.BlockSpec((1, DIFF_DIM), lambda h, qi: (0, 0))
    return pl.pallas_call(
        kern,
        out_shape=jax.ShapeDtypeStruct((seq, SEG_W), BF16),
        grid=(N_HEADS, seq // tq),
        in_specs=[small, small, small, small,
                  pl.BlockSpec((None, HEAD_W, tq), lambda h, qi: (h, 0, qi)),
                  pl.BlockSpec((seq, HEAD_W), lambda h, qi: (0, h)),
                  pl.BlockSpec((None, V_ROWS, seq), lambda h, qi: (h, 0, 0)),
                  pl.BlockSpec((1, HEAD_W), lambda h, qi: (0, 0))],
        out_specs=pl.BlockSpec((tq, HEAD_W), lambda h, qi: (qi, h)),
        scratch_shapes=[pltpu.VMEM((V_ROWS, tq), F32), pltpu.VMEM((V_ROWS, tq), F32)],
        compiler_params=_cparams(("arbitrary", "arbitrary")),
        name="diff_attn",
    )(vec(lq1), vec(lk1), vec(lq2), vec(lk2), qaT, ka, vaT, vec(og))


def _dil_chunks(tq):
    chunks, off = [], 0
    for _, dil in DIL_BRANCHES:
        pad = -(-(N_SIDE * dil) // 128) * 128
        total = tq + 2 * pad
        c0 = 0
        while c0 < total:
            nk = min(DIL_CHUNK, total - c0)
            chunks.append((dil, c0 - pad, nk, off))
            off += nk
            c0 += nk
    return chunks, off


def _dil_bias(tq):
    chunks, total = _dil_chunks(tq)
    bias = np.full((total, tq), NEG, np.float32)
    col = np.arange(tq)[None, :]
    for dil, rel, nk, off in chunks:
        delta = rel + np.arange(nk)[:, None] - col
        ok = (np.abs(delta) <= N_SIDE * dil) & (delta % dil == 0)
        bias[off:off + nk][ok] = 0.0
    return jnp.asarray(bias)


def _dil_attn_kernel(qT_ref, k_ref, vT_ref, bias_ref, og_ref, o_ref, acc, *, seq, chunks):
    tq = qT_ref.shape[1]
    i0 = pl.program_id(1) * tq
    qT = qT_ref[...]
    acc[...] = jnp.zeros(acc.shape, F32)
    m = jnp.full((1, tq), NEG, F32)
    for dil, rel, nk, off in chunks:
        start = pl.multiple_of(i0 + (DIL_PAD + rel), 128)
        k_t = k_ref[pl.ds(start, nk), :]
        vT_t = vT_ref[:, pl.ds(start, nk)]
        s = jnp.dot(k_t, qT, preferred_element_type=F32)
        kpos = lax.broadcasted_iota(jnp.int32, (nk, tq), 0) + (i0 + rel)
        valid = jnp.logical_and(kpos >= 0, kpos < seq)
        s = jnp.where(valid, s + bias_ref[off:off + nk, :], NEG)
        m = _flash_step(s, m, acc, vT_t)
    oT = acc[0:HEAD_W, :] * (1.0 / acc[HEAD_W:HEAD_W + 1, :])
    o_ref[...] = _head_out(oT, og_ref[...], 1.0).astype(o_ref.dtype)


def _dil_attn(qbT, kb, vbT, og):
    seq = qbT.shape[2]
    seq_p = kb.shape[0]
    tq = min(DIL_TQ, seq)
    chunks, _ = _dil_chunks(tq)
    bias = _dil_bias(tq)
    kern = functools.partial(_dil_attn_kernel, seq=seq, chunks=chunks)
    return pl.pallas_call(
        kern,
        out_shape=jax.ShapeDtypeStruct((seq, SEG_W), BF16),
        grid=(N_HEADS, seq // tq),
        in_specs=[pl.BlockSpec((None, HEAD_W, tq), lambda h, qi: (h, 0, qi)),
                  pl.BlockSpec((seq_p, HEAD_W), lambda h, qi: (0, h)),
                  pl.BlockSpec((None, V_ROWS, seq_p), lambda h, qi: (h, 0, 0)),
                  pl.BlockSpec(bias.shape, lambda h, qi: (0, 0)),
                  pl.BlockSpec((1, HEAD_W), lambda h, qi: (0, 0))],
        out_specs=pl.BlockSpec((tq, HEAD_W), lambda h, qi: (qi, h)),
        scratch_shapes=[pltpu.VMEM((V_ROWS, tq), F32)],
        compiler_params=_cparams(("arbitrary", "arbitrary")),
        name="dil_attn",
    )(qbT, kb, vbT, bias, og.reshape(1, HEAD_W).astype(F32))


def _outproj_kernel(*refs, with_router):
    if with_router:
        a_ref, b_ref, w_ref, x_ref, g_ref, r_hi_ref, r_lo_ref, x1_ref, h_ref, route_ref = refs
    else:
        a_ref, b_ref, w_ref, x_ref, g_ref, x1_ref, h_ref = refs
    acc = (jnp.dot(a_ref[...], w_ref[0:SEG_W, :], preferred_element_type=F32)
           + jnp.dot(b_ref[...], w_ref[SEG_W:2 * SEG_W, :], preferred_element_type=F32))
    x1 = x_ref[...] + acc
    x1_ref[...] = x1
    ms = jnp.mean(x1 * x1, axis=-1, keepdims=True)
    hn = x1 * lax.rsqrt(ms + EPS) * g_ref[...]
    h_ref[...] = hn.astype(BF16)
    if with_router:
        h_hi = hn.astype(BF16)
        h_lo = (hn - h_hi.astype(F32)).astype(BF16)
        logits = (jnp.dot(h_hi, r_hi_ref[...], preferred_element_type=F32)
                  + jnp.dot(h_hi, r_lo_ref[...], preferred_element_type=F32)
                  + jnp.dot(h_lo, r_hi_ref[...], preferred_element_type=F32))
        lane = lax.broadcasted_iota(jnp.int32, logits.shape, 1)
        ninf = jnp.float32(-jnp.inf)
        lg = jnp.where(lane < N_EXPERTS, logits, ninf)
        v1 = jnp.max(lg, axis=-1, keepdims=True)
        i1 = jnp.min(jnp.where(lg == v1, lane, HEAD_W), axis=-1, keepdims=True)
        lg2 = jnp.where(lane == i1, ninf, lg)
        v2 = jnp.max(lg2, axis=-1, keepdims=True)
        i2 = jnp.min(jnp.where(lg2 == v2, lane, HEAD_W), axis=-1, keepdims=True)
        g1 = 1.0 / (1.0 + jnp.exp(v2 - v1))
        g2 = 1.0 - g1
        route_ref[...] = jnp.where(lane == 0, i1.astype(F32),
                                   jnp.where(lane == 1, i2.astype(F32),
                                             jnp.where(lane == 2, g1, jnp.where(lane == 3, g2, 0.0))))


def _outproj(out_a, out_b, w_out, x, f_norm, router=None):
    seq = x.shape[0]
    tm = ROW_TILE
    with_router = router is not None
    in_specs = [pl.BlockSpec((tm, SEG_W), lambda i: (i, 0)),
                pl.BlockSpec((tm, SEG_W), lambda i: (i, 0)),
                pl.BlockSpec((D_MODEL, D_MODEL), lambda i: (0, 0)),
                pl.BlockSpec((tm, D_MODEL), lambda i: (i, 0)),
                pl.BlockSpec((1, D_MODEL), lambda i: (0, 0))]
    args = [out_a, out_b, w_out.astype(BF16), x, f_norm.reshape(1, D_MODEL).astype(F32)]
    out_shape = [jax.ShapeDtypeStruct((seq, D_MODEL), F32), jax.ShapeDtypeStruct((seq, D_MODEL), BF16)]
    out_specs = [pl.BlockSpec((tm, D_MODEL), lambda i: (i, 0)), pl.BlockSpec((tm, D_MODEL), lambda i: (i, 0))]
    if with_router:
        r = jnp.zeros((D_MODEL, HEAD_W), F32).at[:, :N_EXPERTS].set(router.astype(F32))
        r_hi = r.astype(BF16)
        r_lo = (r - r_hi.astype(F32)).astype(BF16)
        in_specs += [pl.BlockSpec((D_MODEL, HEAD_W), lambda i: (0, 0))] * 2
        args += [r_hi, r_lo]
        out_shape.append(jax.ShapeDtypeStruct((seq, HEAD_W), F32))
        out_specs.append(pl.BlockSpec((tm, HEAD_W), lambda i: (i, 0)))
    return pl.pallas_call(
        functools.partial(_outproj_kernel, with_router=with_router),
        out_shape=tuple(out_shape),
        grid=(seq // tm,),
        in_specs=in_specs,
        out_specs=tuple(out_specs),
        compiler_params=_cparams(("arbitrary",)),
        name="outproj",
    )(*args)


def _swiglu_part(h, wg, wu, wd):
    g = jnp.dot(h, wg, preferred_element_type=F32)
    u = jnp.dot(h, wu, preferred_element_type=F32)
    act = (g * (1.0 / (1.0 + jnp.exp(-g))) * u).astype(BF16)
    return jnp.dot(act, wd, preferred_element_type=F32)


def _dense_ffn_kernel(h_ref, x_ref, wg_ref, wu_ref, wd_ref, o_ref):
    j = pl.program_id(1)
    part = _swiglu_part(h_ref[...], wg_ref[...], wu_ref[...], wd_ref[...])

    @pl.when(j == 0)
    def _():
        o_ref[...] = x_ref[...] + part

    @pl.when(j != 0)
    def _():
        o_ref[...] += part


def _dense_ffn(h, x1, wg, wu, wd):
    seq = x1.shape[0]
    tm, tf = ROW_TILE, FF_TILE
    return pl.pallas_call(
        _dense_ffn_kernel,
        out_shape=jax.ShapeDtypeStruct((seq, D_MODEL), F32),
        grid=(seq // tm, D_FF // tf),
        in_specs=[pl.BlockSpec((tm, D_MODEL), lambda i, j: (i, 0)),
                  pl.BlockSpec((tm, D_MODEL), lambda i, j: (i, 0)),
                  pl.BlockSpec((D_MODEL, tf), lambda i, j: (0, j)),
                  pl.BlockSpec((D_MODEL, tf), lambda i, j: (0, j)),
                  pl.BlockSpec((tf, D_MODEL), lambda i, j: (j, 0))],
        out_specs=pl.BlockSpec((tm, D_MODEL), lambda i, j: (i, 0)),
        compiler_params=_cparams(("arbitrary", "arbitrary")),
        name="dense_ffn",
    )(h, x1, wg.astype(BF16), wu.astype(BF16), wd.astype(BF16))


def _gather_kernel(idx_ref, src_ref, dst_ref, sem, *, rows):
    base = pl.program_id(0) * rows

    def start(r, c):
        pltpu.make_async_copy(src_ref.at[idx_ref[base + r]], dst_ref.at[base + r], sem).start()
        return c

    lax.fori_loop(0, rows, start, 0)
    pltpu.make_async_copy(src_ref.at[pl.ds(0, rows)], dst_ref.at[pl.ds(base, rows)], sem).wait()


def _gather_rows(src, idx):
    n = idx.shape[0]
    width = src.shape[1]
    rows = min(GATHER_ROWS, n)
    src3 = src.reshape(src.shape[0], width // HEAD_W, HEAD_W)
    out = pl.pallas_call(
        functools.partial(_gather_kernel, rows=rows),
        out_shape=jax.ShapeDtypeStruct((n, width // HEAD_W, HEAD_W), src.dtype),
        grid_spec=pltpu.PrefetchScalarGridSpec(
            num_scalar_prefetch=1,
            grid=(n // rows,),
            in_specs=[pl.BlockSpec(memory_space=pl.ANY)],
            out_specs=pl.BlockSpec(memory_space=pl.ANY),
            scratch_shapes=[pltpu.SemaphoreType.DMA(())]),
        compiler_params=_cparams(("arbitrary",)),
        name="gather_rows",
    )(idx, src3)
    return out.reshape(n, width)


def _moe_ffn_kernel(be_ref, nused_ref, xs_ref, wg_ref, wu_ref, wd_ref, o_ref):
    b = pl.program_id(0)
    j = pl.program_id(1)
    used = b < nused_ref[0]

    @pl.when(used)
    def _():
        part = _swiglu_part(xs_ref[...], wg_ref[...], wu_ref[...], wd_ref[...])

        @pl.when(j == 0)
        def _():
            o_ref[...] = part

        @pl.when(j != 0)
        def _():
            o_ref[...] += part

    @pl.when(jnp.logical_and(jnp.logical_not(used), j == 0))
    def _():
        o_ref[...] = jnp.zeros(o_ref.shape, F32)


def _moe_ffn(xs, blk_expert, n_used, wg, wu, wd):
    n_rows = xs.shape[0]
    tm, tf = MOE_TILE, FF_TILE
    nj = D_EXPERT // tf

    def jj(b, j, be, nu):
        return jnp.where(b < nu[0], j, nj - 1)

    return pl.pallas_call(
        _moe_ffn_kernel,
        out_shape=jax.ShapeDtypeStruct((n_rows, D_MODEL), F32),
        grid_spec=pltpu.PrefetchScalarGridSpec(
            num_scalar_prefetch=2,
            grid=(n_rows // tm, nj),
            in_specs=[pl.BlockSpec((tm, D_MODEL), lambda b, j, be, nu: (b, 0)),
                      pl.BlockSpec((None, D_MODEL, tf), lambda b, j, be, nu: (be[b], 0, jj(b, j, be, nu))),
                      pl.BlockSpec((None, D_MODEL, tf), lambda b, j, be, nu: (be[b], 0, jj(b, j, be, nu))),
                      pl.BlockSpec((None, tf, D_MODEL), lambda b, j, be, nu: (be[b], jj(b, j, be, nu), 0))],
            out_specs=pl.BlockSpec((tm, D_MODEL), lambda b, j, be, nu: (b, 0))),
        compiler_params=_cparams(("arbitrary", "arbitrary")),
        name="moe_ffn",
    )(blk_expert, n_used, xs, wg, wu, wd)


def _combine_kernel(x_ref, ya_ref, yb_ref, route_ref, o_ref):
    r = route_ref[...]
    o_ref[...] = x_ref[...] + r[:, 2:3] * ya_ref[...] + r[:, 3:4] * yb_ref[...]


def _moe_combine(x1, ya, yb, route):
    seq = x1.shape[0]
    tm = ROW_TILE
    big = pl.BlockSpec((tm, D_MODEL), lambda i: (i, 0))
    return pl.pallas_call(
        _combine_kernel,
        out_shape=jax.ShapeDtypeStruct((seq, D_MODEL), F32),
        grid=(seq // tm,),
        in_specs=[big, big, big, pl.BlockSpec((tm, HEAD_W), lambda i: (i, 0))],
        out_specs=big,
        compiler_params=_cparams(("arbitrary",)),
        name="moe_combine",
    )(x1, ya, yb, route)


def _moe_plan(route, seq):
    tm = MOE_TILE
    n_assign = 2 * seq
    n_blocks = n_assign // tm + N_EXPERTS
    e = route[:, 0:2].astype(jnp.int32).reshape(-1)
    onehot = (e[:, None] == jnp.arange(N_EXPERTS, dtype=jnp.int32)[None, :]).astype(jnp.int32)
    csum = jnp.cumsum(onehot, axis=0)
    counts = csum[-1]
    rank = jnp.sum((csum - onehot) * onehot, axis=1)
    nblk = (counts + tm - 1) // tm
    cum_blk = jnp.cumsum(nblk)
    pad_start = (cum_blk - nblk) * tm
    pos = pad_start[e] + rank
    tok = jnp.arange(n_assign, dtype=jnp.int32) // 2
    tok_buf = jnp.zeros((n_blocks * tm,), jnp.int32).at[pos].set(tok)
    n_used = cum_blk[-1].astype(jnp.int32)
    blk = jnp.arange(n_blocks, dtype=jnp.int32)
    blk_expert = jnp.searchsorted(cum_blk, jnp.minimum(blk, n_used - 1), side='right').astype(jnp.int32)
    blk_expert = jnp.minimum(blk_expert, N_EXPERTS - 1)
    pos2 = pos.reshape(seq, 2)
    return tok_buf, blk_expert, n_used.reshape(1), pos2[:, 0], pos2[:, 1]


def _moe(h, x1, route, wg, wu, wd):
    seq = x1.shape[0]
    tok_buf, blk_expert, n_used, pos_a, pos_b = _moe_plan(route, seq)
    xs = _gather_rows(h, tok_buf)
    ys = _moe_ffn(xs, blk_expert, n_used, wg.astype(BF16), wu.astype(BF16), wd.astype(BF16))
    ya = _gather_rows(ys, pos_a)
    yb = _gather_rows(ys, pos_b)
    return _moe_combine(x1, ya, yb, route)


def _layer(x, i, a_norm, w_in, dqn, dkn, lq1, lk1, lq2, lk2, don, bqn, bkn, bon, w_out, f_norm, ffn):
    seq = x.shape[0]
    lam_init = 0.8 - 0.6 * math.exp(-0.3 * i)
    gains, gmat, cos_t, sin_t = _segment_tables(seq, dqn, dkn, bqn, bkn)
    qaT, ka, vaT, qbT, kb, vbT = _inproj(x, a_norm, _permute_w_in(w_in), gains, gmat, cos_t, sin_t)
    out_a = _diff_attn(qaT, ka, vaT, lq1, lk1, lq2, lk2, don, lam_init)
    out_b = _dil_attn(qbT, kb, vbT, bon)
    if len(ffn) == 3:
        x1, h = _outproj(out_a, out_b, w_out, x, f_norm)
        return _dense_ffn(h, x1, *ffn)
    router, wg, wu, wd = ffn
    x1, h, route = _outproj(out_a, out_b, w_out, x, f_norm, router)
    return _moe(h, x1, route, wg, wu, wd)


def kernel(x, attn_norm_0, w_in_0, diff_q_norm_0, diff_k_norm_0, diff_lam_q1_0, diff_lam_k1_0, diff_lam_q2_0, diff_lam_k2_0, diff_out_norm_0, dil_q_norm_0, dil_k_norm_0, dil_out_norm_0, w_out_0, ffn_norm_0, ffn_w_gate_0, ffn_w_up_0, ffn_w_down_0, attn_norm_1, w_in_1, diff_q_norm_1, diff_k_norm_1, diff_lam_q1_1, diff_lam_k1_1, diff_lam_q2_1, diff_lam_k2_1, diff_out_norm_1, dil_q_norm_1, dil_k_norm_1, dil_out_norm_1, w_out_1, ffn_norm_1, router_1, moe_w_gate_1, moe_w_up_1, moe_w_down_1):
    b, seq, d = x.shape
    assert b == 1 and d == D_MODEL and seq % ROW_TILE == 0
    xs = x.reshape(seq, d)
    xs = _layer(xs, 0, attn_norm_0, w_in_0, diff_q_norm_0, diff_k_norm_0, diff_lam_q1_0, diff_lam_k1_0,
                diff_lam_q2_0, diff_lam_k2_0, diff_out_norm_0, dil_q_norm_0, dil_k_norm_0, dil_out_norm_0,
                w_out_0, ffn_norm_0, (ffn_w_gate_0, ffn_w_up_0, ffn_w_down_0))
    xs = _layer(xs, 1, attn_norm_1, w_in_1, diff_q_norm_1, diff_k_norm_1, diff_lam_q1_1, diff_lam_k1_1,
                diff_lam_q2_1, diff_lam_k2_1, diff_out_norm_1, dil_q_norm_1, dil_k_norm_1, dil_out_norm_1,
                w_out_1, ffn_norm_1, (router_1, moe_w_gate_1, moe_w_up_1, moe_w_down_1))
    return xs.reshape(b, seq, d)
```

```python
import functools
import math

import numpy as np
import jax
import jax.numpy as jnp
from jax import lax
from jax.experimental import pallas as pl
from jax.experimental.pallas import tpu as pltpu

F32 = jnp.float32
BF16 = jnp.bfloat16

D_MODEL = 2048
N_HEADS = 8
HEAD_W = 128
DIFF_DIM = 64
SEG_W = N_HEADS * HEAD_W
N_SEG = 6
ROW_CHUNKS = D_MODEL // HEAD_W
DIL_BRANCHES = ((128, 1), (512, 4), (2048, 16))
N_SIDE = 64
D_FF = 5632
N_EXPERTS = 8
D_EXPERT = 7168
EPS = 1e-6
NEG = -1e30
LOG2E = 1.4426950408889634

MAP0_LANE = 0
MAP1_LANE = 32
ROW_SUM_FLOOR = 2.0 ** -80

V_ROWS = 144
DIL_PAD = 1024
VMEM_LIMIT = 52 * 1024 * 1024

ROW_TILE = 512
ATT_TQ = 512
ATT_TK = 2048
DIL_TQ = 256
DIL_CHUNK = 768
FF_TILE = 512
MOE_TILE = 512


def _cparams(sem):
    return pltpu.CompilerParams(dimension_semantics=sem, vmem_limit_bytes=VMEM_LIMIT)


def _diff_lane_perm():
    perm = np.zeros(HEAD_W, np.int32)
    for m in range(2):
        for t in range(DIFF_DIM):
            p = (t // 32) * 64 + m * 32 + (t % 32)
            perm[p] = m * DIFF_DIM + t
    return perm


def _segment_tables(seq, dqn, dkn, bqn, bkn):
    perm = _diff_lane_perm()
    t_of_lane = perm % DIFF_DIM
    ones = jnp.ones((SEG_W,), F32)
    g_qa = jnp.tile(dqn.astype(F32)[t_of_lane], N_HEADS)
    g_ka = jnp.tile(dkn.astype(F32)[t_of_lane], N_HEADS)
    g_qb = jnp.tile(bqn.astype(F32), N_HEADS)
    g_kb = jnp.tile(bkn.astype(F32), N_HEADS)
    gains = jnp.stack([g_qa, g_ka, ones, g_qb, g_kb, ones]).reshape(N_SEG, 1, SEG_W)

    lane = np.arange(HEAD_W)
    map_of_lane = (lane // 32) % 2
    g_diff = (map_of_lane[:, None] == map_of_lane[None, :]).astype(np.float32)
    g_dil = np.ones((HEAD_W, HEAD_W), np.float32)
    gmat = jnp.asarray(np.stack([g_diff, g_dil]), BF16)

    pos = jnp.arange(seq, dtype=F32)

    def ang(dim):
        inv = 10000.0 ** (-jnp.arange(0, dim, 2, dtype=F32) / dim)
        return pos[:, None] * inv[None, :]

    a32 = ang(DIFF_DIM)
    a64 = ang(HEAD_W)
    cos_a = jnp.tile(jnp.cos(a32), (1, 4))
    sin_a = jnp.tile(jnp.sin(a32), (1, 4))
    cos_b = jnp.tile(jnp.cos(a64), (1, 2))
    sin_b = jnp.tile(jnp.sin(a64), (1, 2))
    sign = jnp.where(jnp.arange(HEAD_W) < 64, -1.0, 1.0).astype(F32)[None, :]
    cos_t = jnp.stack([cos_a, cos_b])
    sin_t = jnp.stack([sin_a * sign, sin_b * sign])
    return gains, gmat, cos_t, sin_t


def _permute_w_in(w_in):
    perm = _diff_lane_perm()
    cols = np.arange(N_SEG * SEG_W)
    for seg in (0, 1):
        for h in range(N_HEADS):
            base = seg * SEG_W + h * HEAD_W
            cols[base:base + HEAD_W] = base + perm
    return w_in[:, cols].astype(BF16)


def _inproj_kernel(x_ref, g_ref, w_ref, gain_ref, cos_ref, sin_ref, gmat_ref,
                   qaT_ref, ka_ref, vaT_ref, qbT_ref, kb_ref, vbT_ref, h_scr,
                   *, n_row_blocks, pad_blocks):
    ip = pl.program_id(0)
    j = pl.program_id(1)
    real = jnp.logical_and(ip >= pad_blocks, ip < pad_blocks + n_row_blocks)
    tm = x_ref.shape[0]

    @pl.when(jnp.logical_and(real, j == 0))
    def _():
        x = x_ref[...]
        ms = jnp.mean(x * x, axis=-1, keepdims=True)
        h_scr[...] = (x * lax.rsqrt(ms + EPS) * g_ref[...]).astype(BF16)

    def norm_rope(y, c, n_group, scale):
        sq = y * y
        sq_hi = sq.astype(BF16)
        sq_lo = (sq - sq_hi.astype(F32)).astype(BF16)
        gm = gmat_ref[...]
        ss = (jnp.dot(sq_hi, gm, preferred_element_type=F32)
              + jnp.dot(sq_lo, gm, preferred_element_type=F32))
        yn = y * lax.rsqrt(ss * (1.0 / n_group) + EPS) * gain_ref[:, c * HEAD_W:(c + 1) * HEAD_W]
        out = yn * cos_ref[...] + pltpu.roll(yn, 64, 1) * sin_ref[...]
        if scale != 1.0:
            out = out * scale
        return out

    def aug_rows():
        row = lax.broadcasted_iota(jnp.int32, (V_ROWS - HEAD_W, tm), 0)
        return jnp.where(row == 0, 1.0, 0.0).astype(BF16)

    def segment(seg):
        acc = jnp.dot(h_scr[...], w_ref[...], preferred_element_type=F32)
        for c in range(N_HEADS):
            y = acc[:, c * HEAD_W:(c + 1) * HEAD_W]
            if seg == 0:
                qaT_ref[c] = norm_rope(y, c, DIFF_DIM, DIFF_DIM ** -0.5 * LOG2E).T.astype(BF16)
            elif seg == 1:
                ka_ref[:, c * HEAD_W:(c + 1) * HEAD_W] = norm_rope(y, c, DIFF_DIM, 1.0).astype(BF16)
            elif seg == 3:
                qbT_ref[c] = norm_rope(y, c, HEAD_W, HEAD_W ** -0.5 * LOG2E).T.astype(BF16)
            elif seg == 4:
                kb_ref[:, c * HEAD_W:(c + 1) * HEAD_W] = norm_rope(y, c, HEAD_W, 1.0).astype(BF16)
            else:
                ref = vaT_ref if seg == 2 else vbT_ref
                ref[c, 0:HEAD_W, :] = y.T.astype(BF16)
                ref[c, HEAD_W:V_ROWS, :] = aug_rows()

    for seg in range(N_SEG):
        pl.when(jnp.logical_and(real, j == seg))(functools.partial(segment, seg))

    @pl.when(jnp.logical_and(jnp.logical_not(real), j == 4))
    def _():
        kb_ref[...] = jnp.zeros(kb_ref.shape, BF16)

    @pl.when(jnp.logical_and(jnp.logical_not(real), j == 5))
    def _():
        vbT_ref[...] = jnp.zeros(vbT_ref.shape, BF16)


def _inproj(x, a_norm, w_in_p, gains, gmat, cos_t, sin_t):
    seq = x.shape[0]
    tm = ROW_TILE
    nrb = seq // tm
    pb = DIL_PAD // tm
    seq_p = seq + 2 * DIL_PAD

    def row(ip):
        return jnp.clip(ip - pb, 0, nrb - 1)

    kern = functools.partial(_inproj_kernel, n_row_blocks=nrb, pad_blocks=pb)
    out_shape = (
        jax.ShapeDtypeStruct((N_HEADS, HEAD_W, seq), BF16),
        jax.ShapeDtypeStruct((seq, SEG_W), BF16),
        jax.ShapeDtypeStruct((N_HEADS, V_ROWS, seq), BF16),
        jax.ShapeDtypeStruct((N_HEADS, HEAD_W, seq), BF16),
        jax.ShapeDtypeStruct((seq_p, SEG_W), BF16),
        jax.ShapeDtypeStruct((N_HEADS, V_ROWS, seq_p), BF16),
    )
    in_specs = [
        pl.BlockSpec((tm, D_MODEL), lambda ip, j: (row(ip), 0)),
        pl.BlockSpec((1, D_MODEL), lambda ip, j: (0, 0)),
        pl.BlockSpec((D_MODEL, SEG_W), lambda ip, j: (0, j)),
        pl.BlockSpec((None, 1, SEG_W), lambda ip, j: (j, 0, 0)),
        pl.BlockSpec((None, tm, HEAD_W), lambda ip, j: (j // 3, row(ip), 0)),
        pl.BlockSpec((None, tm, HEAD_W), lambda ip, j: (j // 3, row(ip), 0)),
        pl.BlockSpec((None, HEAD_W, HEAD_W), lambda ip, j: (j // 3, 0, 0)),
    ]
    out_specs = (
        pl.BlockSpec((N_HEADS, HEAD_W, tm), lambda ip, j: (0, 0, row(ip))),
        pl.BlockSpec((tm, SEG_W), lambda ip, j: (row(ip), 0)),
        pl.BlockSpec((N_HEADS, V_ROWS, tm), lambda ip, j: (0, 0, row(ip))),
        pl.BlockSpec((N_HEADS, HEAD_W, tm), lambda ip, j: (0, 0, row(ip))),
        pl.BlockSpec((tm, SEG_W), lambda ip, j: (ip, 0)),
        pl.BlockSpec((N_HEADS, V_ROWS, tm), lambda ip, j: (0, 0, ip)),
    )
    return pl.pallas_call(
        kern,
        out_shape=out_shape,
        grid=(nrb + 2 * pb, N_SEG),
        in_specs=in_specs,
        out_specs=out_specs,
        scratch_shapes=[pltpu.VMEM((tm, D_MODEL), BF16)],
        compiler_params=_cparams(("arbitrary", "arbitrary")),
        name="inproj",
    )(x, a_norm.reshape(1, D_MODEL).astype(F32), w_in_p, gains, cos_t, sin_t, gmat)


def _flash_step(s, m_old, acc_ref, vT_t):
    m_new = jnp.maximum(m_old, jnp.max(s, axis=0, keepdims=True))
    alpha = jnp.exp2(m_old - m_new)
    p = jnp.exp2((s - m_new).astype(BF16))
    acc_ref[...] = acc_ref[...] * alpha + jnp.dot(vT_t, p, preferred_element_type=F32)
    return m_new


def _head_out(aT, gain_row, out_scale):
    a = aT.T
    ms = jnp.mean(a * a, axis=-1, keepdims=True)
    y = a * lax.rsqrt(ms + EPS) * gain_row
    if out_scale != 1.0:
        y = y * out_scale
    return y


def _diff_attn_kernel(lq1_ref, lk1_ref, lq2_ref, lk2_ref, qT_ref, k_ref, vT_ref, og_ref, gmat_ref, o_ref,
                      acc0, acc1, k0_scr, k1_scr, kmax_scr, *, lam_init, tk):
    tq = qT_ref.shape[1]
    seq = k_ref.shape[0]
    lane0, lane1 = MAP1_LANE, MAP0_LANE

    @pl.when(pl.program_id(1) == 0)
    def _():
        ck = min(1024, seq)

        def kchunk(c, mx):
            rows = pl.ds(pl.multiple_of(c * ck, ck), ck)
            kc = k_ref[rows, :]
            kf = kc.astype(F32)
            n2 = jnp.dot((kf * kf).astype(BF16), gmat_ref[...], preferred_element_type=F32)
            lane = lax.broadcasted_iota(jnp.int32, kc.shape, 1)
            one = jnp.ones_like(kc)
            k0_scr[rows, :] = jnp.where(lane == lane0, one, kc)
            k1_scr[rows, :] = jnp.where(lane == lane1, one, kc)
            return jnp.maximum(mx, jnp.max(n2, axis=0, keepdims=True))

        kn2 = lax.fori_loop(0, seq // ck, kchunk, jnp.zeros((1, HEAD_W), F32))
        kmax_scr[...] = jnp.sqrt(kn2)

    qT = qT_ref[...]
    row = lax.broadcasted_iota(jnp.int32, qT.shape, 0)
    in_map1 = ((row // 32) % 2) == 1
    zero = jnp.zeros_like(qT)
    qf = qT.astype(F32)
    q2 = qf * qf
    nq0 = jnp.sum(jnp.where(in_map1, 0.0, q2), axis=0, keepdims=True)
    nq1 = jnp.sum(jnp.where(in_map1, q2, 0.0), axis=0, keepdims=True)
    kmax = kmax_scr[...]
    b0 = jnp.sqrt(nq0) * kmax[:, MAP0_LANE:MAP0_LANE + 1]
    b1 = jnp.sqrt(nq1) * kmax[:, MAP1_LANE:MAP1_LANE + 1]
    q0 = jnp.where(in_map1, zero, qT)
    q1 = jnp.where(in_map1, qT, zero)
    q0s = jnp.where(row == lane0, jnp.broadcast_to(-b0, qf.shape).astype(BF16), q0)
    q1s = jnp.where(row == lane1, jnp.broadcast_to(-b1, qf.shape).astype(BF16), q1)
    acc0[...] = jnp.zeros(acc0.shape, F32)
    acc1[...] = jnp.zeros(acc1.shape, F32)

    def body(kt, carry):
        rows = pl.ds(pl.multiple_of(kt * tk, tk), tk)
        vT_t = vT_ref[:, rows]
        p0 = jnp.exp2(jnp.dot(k0_scr[rows, :], q0s, preferred_element_type=F32)).astype(BF16)
        acc0[...] += jnp.dot(vT_t, p0, preferred_element_type=F32)
        p1 = jnp.exp2(jnp.dot(k1_scr[rows, :], q1s, preferred_element_type=F32)).astype(BF16)
        acc1[...] += jnp.dot(vT_t, p1, preferred_element_type=F32)
        return carry

    lax.fori_loop(0, seq // tk, body, 0)

    lmin = jnp.min(jnp.minimum(acc0[HEAD_W:HEAD_W + 1, :], acc1[HEAD_W:HEAD_W + 1, :]))

    @pl.when(jnp.logical_not(lmin >= ROW_SUM_FLOOR))
    def _():
        acc0[...] = jnp.zeros(acc0.shape, F32)
        acc1[...] = jnp.zeros(acc1.shape, F32)

        def robust(kt, carry):
            m0, m1 = carry
            rows = pl.ds(pl.multiple_of(kt * tk, tk), tk)
            k_t = k_ref[rows, :]
            vT_t = vT_ref[:, rows]
            m0 = _flash_step(jnp.dot(k_t, q0, preferred_element_type=F32), m0, acc0, vT_t)
            m1 = _flash_step(jnp.dot(k_t, q1, preferred_element_type=F32), m1, acc1, vT_t)
            return m0, m1

        m_init = jnp.full((1, tq), NEG, F32)
        lax.fori_loop(0, seq // tk, robust, (m_init, m_init))

    lam = (jnp.exp(jnp.sum(lq1_ref[...] * lk1_ref[...], axis=-1, keepdims=True))
           - jnp.exp(jnp.sum(lq2_ref[...] * lk2_ref[...], axis=-1, keepdims=True)) + lam_init)
    o0 = acc0[0:HEAD_W, :] * (1.0 / acc0[HEAD_W:HEAD_W + 1, :])
    o1 = acc1[0:HEAD_W, :] * (1.0 / acc1[HEAD_W:HEAD_W + 1, :])
    aT = o0 - lam * o1
    o_ref[...] = _head_out(aT, og_ref[...], 1.0 - lam_init).astype(o_ref.dtype)


def _diff_attn(qaT, ka, vaT, lq1, lk1, lq2, lk2, og, gmat_diff, lam_init):
    seq = ka.shape[0]
    tq = min(ATT_TQ, seq)
    tk = min(ATT_TK, seq)
    kern = functools.partial(_diff_attn_kernel, lam_init=lam_init, tk=tk)
    vec = lambda v: v.reshape(1, -1).astype(F32)
    small = pl.BlockSpec((1, DIFF_DIM), lambda h, qi: (0, 0))
    return pl.pallas_call(
        kern,
        out_shape=jax.ShapeDtypeStruct((seq, SEG_W), BF16),
        grid=(N_HEADS, seq // tq),
        in_specs=[small, small, small, small,
                  pl.BlockSpec((None, HEAD_W, tq), lambda h, qi: (h, 0, qi)),
                  pl.BlockSpec((seq, HEAD_W), lambda h, qi: (0, h)),
                  pl.BlockSpec((None, V_ROWS, seq), lambda h, qi: (h, 0, 0)),
                  pl.BlockSpec((1, HEAD_W), lambda h, qi: (0, 0)),
                  pl.BlockSpec((HEAD_W, HEAD_W), lambda h, qi: (0, 0))],
        out_specs=pl.BlockSpec((tq, HEAD_W), lambda h, qi: (qi, h)),
        scratch_shapes=[pltpu.VMEM((V_ROWS, tq), F32), pltpu.VMEM((V_ROWS, tq), F32),
                        pltpu.VMEM((seq, HEAD_W), BF16), pltpu.VMEM((seq, HEAD_W), BF16),
                        pltpu.VMEM((1, HEAD_W), F32)],
        compiler_params=_cparams(("arbitrary", "arbitrary")),
        name="diff_attn",
    )(vec(lq1), vec(lk1), vec(lq2), vec(lk2), qaT, ka, vaT, vec(og), gmat_diff)


def _dil_chunks(tq):
    chunks, off = [], 0
    for _, dil in DIL_BRANCHES:
        pad = -(-(N_SIDE * dil) // 128) * 128
        total = tq + 2 * pad
        c0 = 0
        while c0 < total:
            nk = min(DIL_CHUNK, total - c0)
            chunks.append((dil, c0 - pad, nk, off))
            off += nk
            c0 += nk
    return chunks, off


def _dil_bias(tq):
    chunks, total = _dil_chunks(tq)
    bias = np.full((total, tq), NEG, np.float32)
    col = np.arange(tq)[None, :]
    for dil, rel, nk, off in chunks:
        delta = rel + np.arange(nk)[:, None] - col
        ok = (np.abs(delta) <= N_SIDE * dil) & (delta % dil == 0)
        bias[off:off + nk][ok] = 0.0
    return jnp.asarray(bias)


def _dil_attn_kernel(qT_ref, k_ref, vT_ref, bias_ref, og_ref, o_ref, acc, *, seq, chunks):
    tq = qT_ref.shape[1]
    i0 = pl.program_id(1) * tq
    qT = qT_ref[...]
    acc[...] = jnp.zeros(acc.shape, F32)
    m = jnp.full((1, tq), NEG, F32)
    for dil, rel, nk, off in chunks:
        start = pl.multiple_of(i0 + (DIL_PAD + rel), 128)
        k_t = k_ref[pl.ds(start, nk), :]
        vT_t = vT_ref[:, pl.ds(start, nk)]
        s = jnp.dot(k_t, qT, preferred_element_type=F32)
        kpos = lax.broadcasted_iota(jnp.int32, (nk, tq), 0) + (i0 + rel)
        valid = jnp.logical_and(kpos >= 0, kpos < seq)
        s = jnp.where(valid, s + bias_ref[off:off + nk, :], NEG)
        m = _flash_step(s, m, acc, vT_t)
    oT = acc[0:HEAD_W, :] * (1.0 / acc[HEAD_W:HEAD_W + 1, :])
    o_ref[...] = _head_out(oT, og_ref[...], 1.0).astype(o_ref.dtype)


def _dil_attn(qbT, kb, vbT, og):
    seq = qbT.shape[2]
    seq_p = kb.shape[0]
    tq = min(DIL_TQ, seq)
    chunks, _ = _dil_chunks(tq)
    bias = _dil_bias(tq)
    kern = functools.partial(_dil_attn_kernel, seq=seq, chunks=chunks)
    return pl.pallas_call(
        kern,
        out_shape=jax.ShapeDtypeStruct((seq, SEG_W), BF16),
        grid=(N_HEADS, seq // tq),
        in_specs=[pl.BlockSpec((None, HEAD_W, tq), lambda h, qi: (h, 0, qi)),
                  pl.BlockSpec((seq_p, HEAD_W), lambda h, qi: (0, h)),
                  pl.BlockSpec((None, V_ROWS, seq_p), lambda h, qi: (h, 0, 0)),
                  pl.BlockSpec(bias.shape, lambda h, qi: (0, 0)),
                  pl.BlockSpec((1, HEAD_W), lambda h, qi: (0, 0))],
        out_specs=pl.BlockSpec((tq, HEAD_W), lambda h, qi: (qi, h)),
        scratch_shapes=[pltpu.VMEM((V_ROWS, tq), F32)],
        compiler_params=_cparams(("arbitrary", "arbitrary")),
        name="dil_attn",
    )(qbT, kb, vbT, bias, og.reshape(1, HEAD_W).astype(F32))


def _outproj_kernel(*refs, with_router):
    if with_router:
        a_ref, b_ref, w_ref, x_ref, g_ref, r_hi_ref, r_lo_ref, x1_ref, h_ref, route_ref = refs
    else:
        a_ref, b_ref, w_ref, x_ref, g_ref, x1_ref, h_ref = refs
    acc = (jnp.dot(a_ref[...], w_ref[0:SEG_W, :], preferred_element_type=F32)
           + jnp.dot(b_ref[...], w_ref[SEG_W:2 * SEG_W, :], preferred_element_type=F32))
    x1 = x_ref[...] + acc
    x1_ref[...] = x1
    ms = jnp.mean(x1 * x1, axis=-1, keepdims=True)
    hn = x1 * lax.rsqrt(ms + EPS) * g_ref[...]
    if not with_router:
        h_ref[...] = hn.astype(BF16)
    else:
        for s in range(ROW_CHUNKS):
            h_ref[:, s, :] = hn[:, s * HEAD_W:(s + 1) * HEAD_W]
        h_hi = hn.astype(BF16)
        h_lo = (hn - h_hi.astype(F32)).astype(BF16)
        logits = (jnp.dot(h_hi, r_hi_ref[...], preferred_element_type=F32)
                  + jnp.dot(h_hi, r_lo_ref[...], preferred_element_type=F32)
                  + jnp.dot(h_lo, r_hi_ref[...], preferred_element_type=F32))
        lane = lax.broadcasted_iota(jnp.int32, logits.shape, 1)
        ninf = jnp.float32(-jnp.inf)
        lg = jnp.where(lane < N_EXPERTS, logits, ninf)
        v1 = jnp.max(lg, axis=-1, keepdims=True)
        i1 = jnp.min(jnp.where(lg == v1, lane, HEAD_W), axis=-1, keepdims=True)
        lg2 = jnp.where(lane == i1, ninf, lg)
        v2 = jnp.max(lg2, axis=-1, keepdims=True)
        i2 = jnp.min(jnp.where(lg2 == v2, lane, HEAD_W), axis=-1, keepdims=True)
        g1 = 1.0 / (1.0 + jnp.exp(v2 - v1))
        g2 = 1.0 - g1
        route_ref[...] = jnp.where(lane == 0, i1.astype(F32),
                                   jnp.where(lane == 1, i2.astype(F32),
                                             jnp.where(lane == 2, g1, jnp.where(lane == 3, g2, 0.0))))


def _outproj(out_a, out_b, w_out, x, f_norm, router=None):
    seq = x.shape[0]
    tm = ROW_TILE
    with_router = router is not None
    in_specs = [pl.BlockSpec((tm, SEG_W), lambda i: (i, 0)),
                pl.BlockSpec((tm, SEG_W), lambda i: (i, 0)),
                pl.BlockSpec((D_MODEL, D_MODEL), lambda i: (0, 0)),
                pl.BlockSpec((tm, D_MODEL), lambda i: (i, 0)),
                pl.BlockSpec((1, D_MODEL), lambda i: (0, 0))]
    args = [out_a, out_b, w_out.astype(BF16), x, f_norm.reshape(1, D_MODEL).astype(F32)]
    out_shape = [jax.ShapeDtypeStruct((seq, D_MODEL), F32)]
    out_specs = [pl.BlockSpec((tm, D_MODEL), lambda i: (i, 0))]
    if not with_router:
        out_shape.append(jax.ShapeDtypeStruct((seq, D_MODEL), BF16))
        out_specs.append(pl.BlockSpec((tm, D_MODEL), lambda i: (i, 0)))
    else:
        out_shape.append(jax.ShapeDtypeStruct((seq, ROW_CHUNKS, HEAD_W), F32))
        out_specs.append(pl.BlockSpec((tm, ROW_CHUNKS, HEAD_W), lambda i: (i, 0, 0)))
        r = jnp.zeros((D_MODEL, HEAD_W), F32).at[:, :N_EXPERTS].set(router.astype(F32))
        r_hi = r.astype(BF16)
        r_lo = (r - r_hi.astype(F32)).astype(BF16)
        in_specs += [pl.BlockSpec((D_MODEL, HEAD_W), lambda i: (0, 0))] * 2
        args += [r_hi, r_lo]
        out_shape.append(jax.ShapeDtypeStruct((seq, HEAD_W), F32))
        out_specs.append(pl.BlockSpec((tm, HEAD_W), lambda i: (i, 0)))
    return pl.pallas_call(
        functools.partial(_outproj_kernel, with_router=with_router),
        out_shape=tuple(out_shape),
        grid=(seq // tm,),
        in_specs=in_specs,
        out_specs=tuple(out_specs),
        compiler_params=_cparams(("arbitrary",)),
        name="outproj",
    )(*args)


def _swiglu_part(h, wg, wu, wd):
    g = jnp.dot(h, wg, preferred_element_type=F32)
    u = jnp.dot(h, wu, preferred_element_type=F32)
    act = (g * (1.0 / (1.0 + jnp.exp(-g))) * u).astype(BF16)
    return jnp.dot(act, wd, preferred_element_type=F32)


def _dense_ffn_kernel(h_ref, x_ref, wg_ref, wu_ref, wd_ref, o_ref):
    j = pl.program_id(1)
    part = _swiglu_part(h_ref[...], wg_ref[...], wu_ref[...], wd_ref[...])

    @pl.when(j == 0)
    def _():
        o_ref[...] = x_ref[...] + part

    @pl.when(j != 0)
    def _():
        o_ref[...] += part


def _dense_ffn(h, x1, wg, wu, wd):
    seq = x1.shape[0]
    tm, tf = ROW_TILE, FF_TILE
    return pl.pallas_call(
        _dense_ffn_kernel,
        out_shape=jax.ShapeDtypeStruct((seq, D_MODEL), F32),
        grid=(seq // tm, D_FF // tf),
        in_specs=[pl.BlockSpec((tm, D_MODEL), lambda i, j: (i, 0)),
                  pl.BlockSpec((tm, D_MODEL), lambda i, j: (i, 0)),
                  pl.BlockSpec((D_MODEL, tf), lambda i, j: (0, j)),
                  pl.BlockSpec((D_MODEL, tf), lambda i, j: (0, j)),
                  pl.BlockSpec((tf, D_MODEL), lambda i, j: (j, 0))],
        out_specs=pl.BlockSpec((tm, D_MODEL), lambda i, j: (i, 0)),
        compiler_params=_cparams(("arbitrary", "arbitrary")),
        name="dense_ffn",
    )(h, x1, wg.astype(BF16), wu.astype(BF16), wd.astype(BF16))


def _moe_ffn_kernel(be_ref, nused_ref, nvalid_ref, src_ref, dst_ref, h3_ref, wg_ref, wu_ref, wd_ref, y_ref,
                    x3, xb, acc, o3, gsem, ssem):
    b = pl.program_id(0)
    j = pl.program_id(1)
    nj = pl.num_programs(1)
    tm = xb.shape[0]
    used = b < nused_ref[0]

    @pl.when(jnp.logical_and(used, j == 0))
    def _():
        def start(r, c):
            pltpu.make_async_copy(h3_ref.at[src_ref[0, r]], x3.at[r], gsem).start()
            return c

        lax.fori_loop(0, tm, start, 0)
        pltpu.make_async_copy(h3_ref.at[pl.ds(0, tm)], x3, gsem).wait()
        for s in range(ROW_CHUNKS):
            xb[:, s * HEAD_W:(s + 1) * HEAD_W] = x3[:, s, :].astype(BF16)

    @pl.when(used)
    def _():
        part = _swiglu_part(xb[...], wg_ref[...], wu_ref[...], wd_ref[...])

        @pl.when(j == 0)
        def _():
            acc[...] = part

        @pl.when(j != 0)
        def _():
            acc[...] += part

    @pl.when(jnp.logical_and(used, j == nj - 1))
    def _():
        for s in range(ROW_CHUNKS):
            o3[:, s, :] = acc[:, s * HEAD_W:(s + 1) * HEAD_W]
        n_valid = nvalid_ref[b]

        def copy(r):
            return pltpu.make_async_copy(o3.at[r], y_ref.at[dst_ref[0, r]], ssem)

        def start(r, c):
            copy(r).start()
            return c

        def wait(r, c):
            copy(r).wait()
            return c

        lax.fori_loop(0, n_valid, start, 0)
        lax.fori_loop(0, n_valid, wait, 0)


def _moe_ffn(h3, src_idx, dst_idx, blk_expert, n_used, n_valid, wg, wu, wd):
    seq = h3.shape[0]
    n_blocks = src_idx.shape[0]
    tm, tf = MOE_TILE, FF_TILE
    nj = D_EXPERT // tf

    def jj(b, j, be, nu, nv):
        return jnp.where(b < nu[0], j, nj - 1)

    idx_spec = pl.BlockSpec((None, 1, tm), lambda b, j, be, nu, nv: (b, 0, 0), memory_space=pltpu.SMEM)
    return pl.pallas_call(
        _moe_ffn_kernel,
        out_shape=jax.ShapeDtypeStruct((2 * seq, ROW_CHUNKS, HEAD_W), F32),
        grid_spec=pltpu.PrefetchScalarGridSpec(
            num_scalar_prefetch=3,
            grid=(n_blocks, nj),
            in_specs=[idx_spec, idx_spec,
                      pl.BlockSpec(memory_space=pl.ANY),
                      pl.BlockSpec((None, D_MODEL, tf), lambda b, j, be, nu, nv: (be[b], 0, jj(b, j, be, nu, nv))),
                      pl.BlockSpec((None, D_MODEL, tf), lambda b, j, be, nu, nv: (be[b], 0, jj(b, j, be, nu, nv))),
                      pl.BlockSpec((None, tf, D_MODEL), lambda b, j, be, nu, nv: (be[b], jj(b, j, be, nu, nv), 0))],
            out_specs=pl.BlockSpec(memory_space=pl.ANY),
            scratch_shapes=[pltpu.VMEM((tm, ROW_CHUNKS, HEAD_W), F32),
                            pltpu.VMEM((tm, D_MODEL), BF16),
                            pltpu.VMEM((tm, D_MODEL), F32),
                            pltpu.VMEM((tm, ROW_CHUNKS, HEAD_W), F32),
                            pltpu.SemaphoreType.DMA(()),
                            pltpu.SemaphoreType.DMA(())]),
        compiler_params=_cparams(("arbitrary", "arbitrary")),
        name="moe_ffn",
    )(blk_expert, n_used, n_valid, src_idx, dst_idx, h3, wg, wu, wd)


def _combine_kernel(x_ref, ya_ref, yb_ref, route_ref, o_ref):
    r = route_ref[...]
    g1 = jnp.broadcast_to(r[:, 2:3], (r.shape[0], HEAD_W))
    g2 = jnp.broadcast_to(r[:, 3:4], (r.shape[0], HEAD_W))
    for s in range(ROW_CHUNKS):
        cols = slice(s * HEAD_W, (s + 1) * HEAD_W)
        o_ref[:, cols] = x_ref[:, cols] + g1 * ya_ref[:, s, :] + g2 * yb_ref[:, s, :]


def _moe_combine(x1, y, route):
    seq = x1.shape[0]
    tm = ROW_TILE
    nb = seq // tm
    big = pl.BlockSpec((tm, D_MODEL), lambda i: (i, 0))
    return pl.pallas_call(
        _combine_kernel,
        out_shape=jax.ShapeDtypeStruct((seq, D_MODEL), F32),
        grid=(nb,),
        in_specs=[big,
                  pl.BlockSpec((tm, ROW_CHUNKS, HEAD_W), lambda i: (i, 0, 0)),
                  pl.BlockSpec((tm, ROW_CHUNKS, HEAD_W), lambda i: (i + nb, 0, 0)),
                  pl.BlockSpec((tm, HEAD_W), lambda i: (i, 0))],
        out_specs=big,
        compiler_params=_cparams(("arbitrary",)),
        name="moe_combine",
    )(x1, y, y, route)


def _moe_plan(route, seq):
    tm = MOE_TILE
    n_assign = 2 * seq
    n_blocks = n_assign // tm + N_EXPERTS
    e = route[:, 0:2].astype(jnp.int32).reshape(-1)
    onehot = (e[:, None] == jnp.arange(N_EXPERTS, dtype=jnp.int32)[None, :]).astype(jnp.int32)
    csum = jnp.cumsum(onehot, axis=0)
    counts = csum[-1]
    rank = jnp.sum((csum - onehot) * onehot, axis=1)
    nblk = (counts + tm - 1) // tm
    cum_blk = jnp.cumsum(nblk)
    pad_start = (cum_blk - nblk) * tm
    pos = pad_start[e] + rank
    assign = jnp.arange(n_assign, dtype=jnp.int32)
    tok = assign // 2
    src_idx = jnp.zeros((n_blocks * tm,), jnp.int32).at[pos].set(tok)
    dst_idx = jnp.zeros((n_blocks * tm,), jnp.int32).at[pos].set((assign % 2) * seq + tok)
    n_used = cum_blk[-1].astype(jnp.int32)
    blk = jnp.arange(n_blocks, dtype=jnp.int32)
    blk_expert = jnp.searchsorted(cum_blk, jnp.minimum(blk, n_used - 1), side='right').astype(jnp.int32)
    blk_expert = jnp.minimum(blk_expert, N_EXPERTS - 1)
    first_blk = cum_blk - nblk
    n_valid = jnp.clip(counts[blk_expert] - (blk - first_blk[blk_expert]) * tm, 0, tm).astype(jnp.int32)
    return (src_idx.reshape(n_blocks, 1, tm), dst_idx.reshape(n_blocks, 1, tm), blk_expert,
            n_used.reshape(1), n_valid)


def _moe(h3, x1, route, wg, wu, wd):
    seq = x1.shape[0]
    src_idx, dst_idx, blk_expert, n_used, n_valid = _moe_plan(route, seq)
    y = _moe_ffn(h3, src_idx, dst_idx, blk_expert, n_used, n_valid,
                 wg.astype(BF16), wu.astype(BF16), wd.astype(BF16))
    return _moe_combine(x1, y, route)


def _layer(x, i, a_norm, w_in, dqn, dkn, lq1, lk1, lq2, lk2, don, bqn, bkn, bon, w_out, f_norm, ffn):
    seq = x.shape[0]
    lam_init = 0.8 - 0.6 * math.exp(-0.3 * i)
    gains, gmat, cos_t, sin_t = _segment_tables(seq, dqn, dkn, bqn, bkn)
    qaT, ka, vaT, qbT, kb, vbT = _inproj(x, a_norm, _permute_w_in(w_in), gains, gmat, cos_t, sin_t)
    out_a = _diff_attn(qaT, ka, vaT, lq1, lk1, lq2, lk2, don, gmat[0], lam_init)
    out_b = _dil_attn(qbT, kb, vbT, bon)
    if len(ffn) == 3:
        x1, h = _outproj(out_a, out_b, w_out, x, f_norm)
        return _dense_ffn(h, x1, *ffn)
    router, wg, wu, wd = ffn
    x1, h3, route = _outproj(out_a, out_b, w_out, x, f_norm, router)
    return _moe(h3, x1, route, wg, wu, wd)


def kernel(x, attn_norm_0, w_in_0, diff_q_norm_0, diff_k_norm_0, diff_lam_q1_0, diff_lam_k1_0, diff_lam_q2_0, diff_lam_k2_0, diff_out_norm_0, dil_q_norm_0, dil_k_norm_0, dil_out_norm_0, w_out_0, ffn_norm_0, ffn_w_gate_0, ffn_w_up_0, ffn_w_down_0, attn_norm_1, w_in_1, diff_q_norm_1, diff_k_norm_1, diff_lam_q1_1, diff_lam_k1_1, diff_lam_q2_1, diff_lam_k2_1, diff_out_norm_1, dil_q_norm_1, dil_k_norm_1, dil_out_norm_1, w_out_1, ffn_norm_1, router_1, moe_w_gate_1, moe_w_up_1, moe_w_down_1):
    b, seq, d = x.shape
    assert b == 1 and d == D_MODEL and seq % ROW_TILE == 0
    xs = x.reshape(seq, d)
    xs = _layer(xs, 0, attn_norm_0, w_in_0, diff_q_norm_0, diff_k_norm_0, diff_lam_q1_0, diff_lam_k1_0,
                diff_lam_q2_0, diff_lam_k2_0, diff_out_norm_0, dil_q_norm_0, dil_k_norm_0, dil_out_norm_0,
                w_out_0, ffn_norm_0, (ffn_w_gate_0, ffn_w_up_0, ffn_w_down_0))
    xs = _layer(xs, 1, attn_norm_1, w_in_1, diff_q_norm_1, diff_k_norm_1, diff_lam_q1_1, diff_lam_k1_1,
                diff_lam_q2_1, diff_lam_k2_1, diff_out_norm_1, dil_q_norm_1, dil_k_norm_1, dil_out_norm_1,
                w_out_1, ffn_norm_1, (router_1, moe_w_gate_1, moe_w_up_1, moe_w_down_1))
    return xs.reshape(b, seq, d)
```

```python
import functools
import math

import numpy as np
import jax
import jax.numpy as jnp
from jax import lax
from jax.experimental import pallas as pl
from jax.experimental.pallas import tpu as pltpu

F32 = jnp.float32
BF16 = jnp.bfloat16

D_MODEL = 2048
N_HEADS = 8
HEAD_W = 128
DIFF_DIM = 64
SEG_W = N_HEADS * HEAD_W
N_SEG = 6
ROW_CHUNKS = D_MODEL // HEAD_W
SLAB_PITCH = 24
DIL_BRANCHES = ((128, 1), (512, 4), (2048, 16))
N_SIDE = 64
D_FF = 5632
N_EXPERTS = 8
D_EXPERT = 7168
EPS = 1e-6
NEG = -1e30
LOG2E = 1.4426950408889634

MAP0_LANE = 0
MAP1_LANE = 32
ROW_SUM_FLOOR = 2.0 ** -80

V_ROWS = 144
DIL_PAD = 1024
VMEM_LIMIT = 56 * 1024 * 1024

ROW_TILE = 512
ATT_TQ = 512
ATT_TK = 2048
DIL_TQ = 256
DIL_CHUNK = 768
FF_TILE = 512
MOE_FF_TILE = 1024
MOE_TILE = 512


def _cparams(sem):
    return pltpu.CompilerParams(dimension_semantics=sem, vmem_limit_bytes=VMEM_LIMIT)


def _diff_lane_perm():
    perm = np.zeros(HEAD_W, np.int32)
    for m in range(2):
        for t in range(DIFF_DIM):
            p = (t // 32) * 64 + m * 32 + (t % 32)
            perm[p] = m * DIFF_DIM + t
    return perm


def _segment_tables(seq, dqn, dkn, bqn, bkn):
    perm = _diff_lane_perm()
    t_of_lane = perm % DIFF_DIM
    ones = jnp.ones((SEG_W,), F32)
    g_qa = jnp.tile(dqn.astype(F32)[t_of_lane], N_HEADS)
    g_ka = jnp.tile(dkn.astype(F32)[t_of_lane], N_HEADS)
    g_qb = jnp.tile(bqn.astype(F32), N_HEADS)
    g_kb = jnp.tile(bkn.astype(F32), N_HEADS)
    gains = jnp.stack([g_qa, g_ka, ones, g_qb, g_kb, ones]).reshape(N_SEG, 1, SEG_W)

    lane = np.arange(HEAD_W)
    map_of_lane = (lane // 32) % 2
    g_diff = (map_of_lane[:, None] == map_of_lane[None, :]).astype(np.float32)
    g_dil = np.ones((HEAD_W, HEAD_W), np.float32)
    gmat = jnp.asarray(np.stack([g_diff, g_dil]), BF16)

    pos = jnp.arange(seq, dtype=F32)

    def ang(dim):
        inv = 10000.0 ** (-jnp.arange(0, dim, 2, dtype=F32) / dim)
        return pos[:, None] * inv[None, :]

    a32 = ang(DIFF_DIM)
    a64 = ang(HEAD_W)
    cos_a = jnp.tile(jnp.cos(a32), (1, 4))
    sin_a = jnp.tile(jnp.sin(a32), (1, 4))
    cos_b = jnp.tile(jnp.cos(a64), (1, 2))
    sin_b = jnp.tile(jnp.sin(a64), (1, 2))
    sign = jnp.where(jnp.arange(HEAD_W) < 64, -1.0, 1.0).astype(F32)[None, :]
    cos_t = jnp.stack([cos_a, cos_b])
    sin_t = jnp.stack([sin_a * sign, sin_b * sign])
    return gains, gmat, cos_t, sin_t


def _permute_w_in(w_in):
    perm = _diff_lane_perm()
    cols = np.arange(N_SEG * SEG_W)
    for seg in (0, 1):
        for h in range(N_HEADS):
            base = seg * SEG_W + h * HEAD_W
            cols[base:base + HEAD_W] = base + perm
    return w_in[:, cols].astype(BF16)


def _inproj_kernel(x_ref, g_ref, w_ref, gain_ref, cos_ref, sin_ref, gmat_ref,
                   qaT_ref, ka_ref, vaT_ref, qbT_ref, kb_ref, vbT_ref, h_scr,
                   *, n_row_blocks, pad_blocks):
    ip = pl.program_id(0)
    j = pl.program_id(1)
    real = jnp.logical_and(ip >= pad_blocks, ip < pad_blocks + n_row_blocks)
    tm = x_ref.shape[0]

    @pl.when(jnp.logical_and(real, j == 0))
    def _():
        x = x_ref[...]
        ms = jnp.mean(x * x, axis=-1, keepdims=True)
        h_scr[...] = (x * lax.rsqrt(ms + EPS) * g_ref[...]).astype(BF16)

    def norm_rope(y, c, n_group, scale):
        sq = y * y
        sq_hi = sq.astype(BF16)
        sq_lo = (sq - sq_hi.astype(F32)).astype(BF16)
        gm = gmat_ref[...]
        ss = (jnp.dot(sq_hi, gm, preferred_element_type=F32)
              + jnp.dot(sq_lo, gm, preferred_element_type=F32))
        yn = y * lax.rsqrt(ss * (1.0 / n_group) + EPS) * gain_ref[:, c * HEAD_W:(c + 1) * HEAD_W]
        out = yn * cos_ref[...] + pltpu.roll(yn, 64, 1) * sin_ref[...]
        if scale != 1.0:
            out = out * scale
        return out

    def aug_rows():
        row = lax.broadcasted_iota(jnp.int32, (V_ROWS - HEAD_W, tm), 0)
        return jnp.where(row == 0, 1.0, 0.0).astype(BF16)

    def segment(seg):
        acc = jnp.dot(h_scr[...], w_ref[...], preferred_element_type=F32)
        for c in range(N_HEADS):
            y = acc[:, c * HEAD_W:(c + 1) * HEAD_W]
            if seg == 0:
                qaT_ref[c] = norm_rope(y, c, DIFF_DIM, DIFF_DIM ** -0.5 * LOG2E).T.astype(BF16)
            elif seg == 1:
                ka_ref[:, c * HEAD_W:(c + 1) * HEAD_W] = norm_rope(y, c, DIFF_DIM, 1.0).astype(BF16)
            elif seg == 3:
                qbT_ref[c] = norm_rope(y, c, HEAD_W, HEAD_W ** -0.5 * LOG2E).T.astype(BF16)
            elif seg == 4:
                kb_ref[:, c * HEAD_W:(c + 1) * HEAD_W] = norm_rope(y, c, HEAD_W, 1.0).astype(BF16)
            else:
                ref = vaT_ref if seg == 2 else vbT_ref
                ref[c, 0:HEAD_W, :] = y.T.astype(BF16)
                ref[c, HEAD_W:V_ROWS, :] = aug_rows()

    for seg in range(N_SEG):
        pl.when(jnp.logical_and(real, j == seg))(functools.partial(segment, seg))

    @pl.when(jnp.logical_and(jnp.logical_not(real), j == 4))
    def _():
        kb_ref[...] = jnp.zeros(kb_ref.shape, BF16)

    @pl.when(jnp.logical_and(jnp.logical_not(real), j == 5))
    def _():
        vbT_ref[...] = jnp.zeros(vbT_ref.shape, BF16)


def _inproj(x, a_norm, w_in_p, gains, gmat, cos_t, sin_t):
    seq = x.shape[0]
    tm = ROW_TILE
    nrb = seq // tm
    pb = DIL_PAD // tm
    seq_p = seq + 2 * DIL_PAD

    def row(ip):
        return jnp.clip(ip - pb, 0, nrb - 1)

    kern = functools.partial(_inproj_kernel, n_row_blocks=nrb, pad_blocks=pb)
    out_shape = (
        jax.ShapeDtypeStruct((N_HEADS, HEAD_W, seq), BF16),
        jax.ShapeDtypeStruct((seq, SEG_W), BF16),
        jax.ShapeDtypeStruct((N_HEADS, V_ROWS, seq), BF16),
        jax.ShapeDtypeStruct((N_HEADS, HEAD_W, seq), BF16),
        jax.ShapeDtypeStruct((seq_p, SEG_W), BF16),
        jax.ShapeDtypeStruct((N_HEADS, V_ROWS, seq_p), BF16),
    )
    in_specs = [
        pl.BlockSpec((tm, D_MODEL), lambda ip, j: (row(ip), 0)),
        pl.BlockSpec((1, D_MODEL), lambda ip, j: (0, 0)),
        pl.BlockSpec((D_MODEL, SEG_W), lambda ip, j: (0, j)),
        pl.BlockSpec((None, 1, SEG_W), lambda ip, j: (j, 0, 0)),
        pl.BlockSpec((None, tm, HEAD_W), lambda ip, j: (j // 3, row(ip), 0)),
        pl.BlockSpec((None, tm, HEAD_W), lambda ip, j: (j // 3, row(ip), 0)),
        pl.BlockSpec((None, HEAD_W, HEAD_W), lambda ip, j: (j // 3, 0, 0)),
    ]
    out_specs = (
        pl.BlockSpec((N_HEADS, HEAD_W, tm), lambda ip, j: (0, 0, row(ip))),
        pl.BlockSpec((tm, SEG_W), lambda ip, j: (row(ip), 0)),
        pl.BlockSpec((N_HEADS, V_ROWS, tm), lambda ip, j: (0, 0, row(ip))),
        pl.BlockSpec((N_HEADS, HEAD_W, tm), lambda ip, j: (0, 0, row(ip))),
        pl.BlockSpec((tm, SEG_W), lambda ip, j: (ip, 0)),
        pl.BlockSpec((N_HEADS, V_ROWS, tm), lambda ip, j: (0, 0, ip)),
    )
    return pl.pallas_call(
        kern,
        out_shape=out_shape,
        grid=(nrb + 2 * pb, N_SEG),
        in_specs=in_specs,
        out_specs=out_specs,
        scratch_shapes=[pltpu.VMEM((tm, D_MODEL), BF16)],
        compiler_params=_cparams(("arbitrary", "arbitrary")),
        name="inproj",
    )(x, a_norm.reshape(1, D_MODEL).astype(F32), w_in_p, gains, cos_t, sin_t, gmat)


def _flash_step(s, m_old, acc_ref, vT_t):
    m_new = jnp.maximum(m_old, jnp.max(s, axis=0, keepdims=True))
    alpha = jnp.exp2(m_old - m_new)
    p = jnp.exp2((s - m_new).astype(BF16))
    acc_ref[...] = acc_ref[...] * alpha + jnp.dot(vT_t, p, preferred_element_type=F32)
    return m_new


def _head_out(aT, gain_row, out_scale):
    a = aT.T
    ms = jnp.mean(a * a, axis=-1, keepdims=True)
    y = a * lax.rsqrt(ms + EPS) * gain_row
    if out_scale != 1.0:
        y = y * out_scale
    return y


def _diff_attn_kernel(lq1_ref, lk1_ref, lq2_ref, lk2_ref, qT_ref, k_ref, vT_ref, og_ref, gmat_ref, o_ref,
                      acc0, acc1, k0_scr, k1_scr, kmax_scr, *, lam_init, tk):
    tq = qT_ref.shape[1]
    seq = k_ref.shape[0]
    lane0, lane1 = MAP1_LANE, MAP0_LANE

    @pl.when(pl.program_id(1) == 0)
    def _():
        ck = min(1024, seq)

        def kchunk(c, mx):
            rows = pl.ds(pl.multiple_of(c * ck, ck), ck)
            kc = k_ref[rows, :]
            kf = kc.astype(F32)
            n2 = jnp.dot((kf * kf).astype(BF16), gmat_ref[...], preferred_element_type=F32)
            lane = lax.broadcasted_iota(jnp.int32, kc.shape, 1)
            one = jnp.ones_like(kc)
            k0_scr[rows, :] = jnp.where(lane == lane0, one, kc)
            k1_scr[rows, :] = jnp.where(lane == lane1, one, kc)
            return jnp.maximum(mx, jnp.max(n2, axis=0, keepdims=True))

        kn2 = lax.fori_loop(0, seq // ck, kchunk, jnp.zeros((1, HEAD_W), F32))
        kmax_scr[...] = jnp.sqrt(kn2)

    qT = qT_ref[...]
    row = lax.broadcasted_iota(jnp.int32, qT.shape, 0)
    in_map1 = ((row // 32) % 2) == 1
    zero = jnp.zeros_like(qT)
    qf = qT.astype(F32)
    q2 = qf * qf
    nq0 = jnp.sum(jnp.where(in_map1, 0.0, q2), axis=0, keepdims=True)
    nq1 = jnp.sum(jnp.where(in_map1, q2, 0.0), axis=0, keepdims=True)
    kmax = kmax_scr[...]
    b0 = jnp.sqrt(nq0) * kmax[:, MAP0_LANE:MAP0_LANE + 1]
    b1 = jnp.sqrt(nq1) * kmax[:, MAP1_LANE:MAP1_LANE + 1]
    q0 = jnp.where(in_map1, zero, qT)
    q1 = jnp.where(in_map1, qT, zero)
    q0s = jnp.where(row == lane0, jnp.broadcast_to(-b0, qf.shape).astype(BF16), q0)
    q1s = jnp.where(row == lane1, jnp.broadcast_to(-b1, qf.shape).astype(BF16), q1)
    acc0[...] = jnp.zeros(acc0.shape, F32)
    acc1[...] = jnp.zeros(acc1.shape, F32)

    def body(kt, carry):
        rows = pl.ds(pl.multiple_of(kt * tk, tk), tk)
        vT_t = vT_ref[:, rows]
        p0 = jnp.exp2(jnp.dot(k0_scr[rows, :], q0s, preferred_element_type=F32)).astype(BF16)
        acc0[...] += jnp.dot(vT_t, p0, preferred_element_type=F32)
        p1 = jnp.exp2(jnp.dot(k1_scr[rows, :], q1s, preferred_element_type=F32)).astype(BF16)
        acc1[...] += jnp.dot(vT_t, p1, preferred_element_type=F32)
        return carry

    lax.fori_loop(0, seq // tk, body, 0)

    lmin = jnp.min(jnp.minimum(acc0[HEAD_W:HEAD_W + 1, :], acc1[HEAD_W:HEAD_W + 1, :]))

    @pl.when(jnp.logical_not(lmin >= ROW_SUM_FLOOR))
    def _():
        acc0[...] = jnp.zeros(acc0.shape, F32)
        acc1[...] = jnp.zeros(acc1.shape, F32)

        def robust(kt, carry):
            m0, m1 = carry
            rows = pl.ds(pl.multiple_of(kt * tk, tk), tk)
            k_t = k_ref[rows, :]
            vT_t = vT_ref[:, rows]
            m0 = _flash_step(jnp.dot(k_t, q0, preferred_element_type=F32), m0, acc0, vT_t)
            m1 = _flash_step(jnp.dot(k_t, q1, preferred_element_type=F32), m1, acc1, vT_t)
            return m0, m1

        m_init = jnp.full((1, tq), NEG, F32)
        lax.fori_loop(0, seq // tk, robust, (m_init, m_init))

    lam = (jnp.exp(jnp.sum(lq1_ref[...] * lk1_ref[...], axis=-1, keepdims=True))
           - jnp.exp(jnp.sum(lq2_ref[...] * lk2_ref[...], axis=-1, keepdims=True)) + lam_init)
    o0 = acc0[0:HEAD_W, :] * (1.0 / acc0[HEAD_W:HEAD_W + 1, :])
    o1 = acc1[0:HEAD_W, :] * (1.0 / acc1[HEAD_W:HEAD_W + 1, :])
    aT = o0 - lam * o1
    o_ref[...] = _head_out(aT, og_ref[...], 1.0 - lam_init).astype(o_ref.dtype)


def _diff_attn(qaT, ka, vaT, lq1, lk1, lq2, lk2, og, gmat_diff, lam_init):
    seq = ka.shape[0]
    tq = min(ATT_TQ, seq)
    tk = min(ATT_TK, seq)
    kern = functools.partial(_diff_attn_kernel, lam_init=lam_init, tk=tk)
    vec = lambda v: v.reshape(1, -1).astype(F32)
    small = pl.BlockSpec((1, DIFF_DIM), lambda h, qi: (0, 0))
    return pl.pallas_call(
        kern,
        out_shape=jax.ShapeDtypeStruct((seq, SEG_W), BF16),
        grid=(N_HEADS, seq // tq),
        in_specs=[small, small, small, small,
                  pl.BlockSpec((None, HEAD_W, tq), lambda h, qi: (h, 0, qi)),
                  pl.BlockSpec((seq, HEAD_W), lambda h, qi: (0, h)),
                  pl.BlockSpec((None, V_ROWS, seq), lambda h, qi: (h, 0, 0)),
                  pl.BlockSpec((1, HEAD_W), lambda h, qi: (0, 0)),
                  pl.BlockSpec((HEAD_W, HEAD_W), lambda h, qi: (0, 0))],
        out_specs=pl.BlockSpec((tq, HEAD_W), lambda h, qi: (qi, h)),
        scratch_shapes=[pltpu.VMEM((V_ROWS, tq), F32), pltpu.VMEM((V_ROWS, tq), F32),
                        pltpu.VMEM((seq, HEAD_W), BF16), pltpu.VMEM((seq, HEAD_W), BF16),
                        pltpu.VMEM((1, HEAD_W), F32)],
        compiler_params=_cparams(("arbitrary", "arbitrary")),
        name="diff_attn",
    )(vec(lq1), vec(lk1), vec(lq2), vec(lk2), qaT, ka, vaT, vec(og), gmat_diff)


def _dil_chunks(tq):
    chunks, off = [], 0
    for _, dil in DIL_BRANCHES:
        pad = -(-(N_SIDE * dil) // 128) * 128
        total = tq + 2 * pad
        c0 = 0
        while c0 < total:
            nk = min(DIL_CHUNK, total - c0)
            chunks.append((dil, c0 - pad, nk, off))
            off += nk
            c0 += nk
    return chunks, off


def _dil_bias(tq):
    chunks, total = _dil_chunks(tq)
    bias = np.full((total, tq), NEG, np.float32)
    col = np.arange(tq)[None, :]
    for dil, rel, nk, off in chunks:
        delta = rel + np.arange(nk)[:, None] - col
        ok = (np.abs(delta) <= N_SIDE * dil) & (delta % dil == 0)
        bias[off:off + nk][ok] = 0.0
    return jnp.asarray(bias)


def _dil_attn_kernel(qT_ref, k_ref, vT_ref, bias_ref, og_ref, gmat_ref, o_ref, acc, kmax_scr, *, seq, chunks):
    tq = qT_ref.shape[1]
    i0 = pl.program_id(1) * tq

    @pl.when(pl.program_id(1) == 0)
    def _():
        ck = 1024

        def kchunk(c, mx):
            kf = k_ref[pl.ds(pl.multiple_of(c * ck, ck), ck), :].astype(F32)
            n2 = jnp.dot((kf * kf).astype(BF16), gmat_ref[...], preferred_element_type=F32)
            return jnp.maximum(mx, jnp.max(n2, axis=0, keepdims=True))

        kn2 = lax.fori_loop(0, k_ref.shape[0] // ck, kchunk, jnp.zeros((1, HEAD_W), F32))
        kmax_scr[...] = jnp.sqrt(kn2)

    qT = qT_ref[...]
    qf = qT.astype(F32)
    shift = jnp.sqrt(jnp.sum(qf * qf, axis=0, keepdims=True)) * kmax_scr[:, 0:1]

    def windows():
        for dil, rel, nk, off in chunks:
            start = pl.multiple_of(i0 + (DIL_PAD + rel), 128)
            yield rel, nk, off, k_ref[pl.ds(start, nk), :], vT_ref[:, pl.ds(start, nk)]

    num = jnp.zeros(acc.shape, F32)
    for rel, nk, off, k_t, vT_t in windows():
        s = jnp.dot(k_t, qT, preferred_element_type=F32) - shift + bias_ref[off:off + nk, :]
        num = num + jnp.dot(vT_t, jnp.exp2(s).astype(BF16), preferred_element_type=F32)
    acc[...] = num

    @pl.when(jnp.logical_not(jnp.min(num[HEAD_W:HEAD_W + 1, :]) >= ROW_SUM_FLOOR))
    def _():
        acc[...] = jnp.zeros(acc.shape, F32)
        m = jnp.full((1, tq), NEG, F32)
        for rel, nk, off, k_t, vT_t in windows():
            s = jnp.dot(k_t, qT, preferred_element_type=F32)
            kpos = lax.broadcasted_iota(jnp.int32, (nk, tq), 0) + (i0 + rel)
            valid = jnp.logical_and(kpos >= 0, kpos < seq)
            s = jnp.where(valid, s + bias_ref[off:off + nk, :], NEG)
            m = _flash_step(s, m, acc, vT_t)

    oT = acc[0:HEAD_W, :] * (1.0 / acc[HEAD_W:HEAD_W + 1, :])
    o_ref[...] = _head_out(oT, og_ref[...], 1.0).astype(o_ref.dtype)


def _dil_attn(qbT, kb, vbT, og, gmat_ones):
    seq = qbT.shape[2]
    seq_p = kb.shape[0]
    tq = min(DIL_TQ, seq)
    chunks, _ = _dil_chunks(tq)
    bias = _dil_bias(tq)
    kern = functools.partial(_dil_attn_kernel, seq=seq, chunks=chunks)
    return pl.pallas_call(
        kern,
        out_shape=jax.ShapeDtypeStruct((seq, SEG_W), BF16),
        grid=(N_HEADS, seq // tq),
        in_specs=[pl.BlockSpec((None, HEAD_W, tq), lambda h, qi: (h, 0, qi)),
                  pl.BlockSpec((seq_p, HEAD_W), lambda h, qi: (0, h)),
                  pl.BlockSpec((None, V_ROWS, seq_p), lambda h, qi: (h, 0, 0)),
                  pl.BlockSpec(bias.shape, lambda h, qi: (0, 0)),
                  pl.BlockSpec((1, HEAD_W), lambda h, qi: (0, 0)),
                  pl.BlockSpec((HEAD_W, HEAD_W), lambda h, qi: (0, 0))],
        out_specs=pl.BlockSpec((tq, HEAD_W), lambda h, qi: (qi, h)),
        scratch_shapes=[pltpu.VMEM((V_ROWS, tq), F32), pltpu.VMEM((1, HEAD_W), F32)],
        compiler_params=_cparams(("arbitrary", "arbitrary")),
        name="dil_attn",
    )(qbT, kb, vbT, bias, og.reshape(1, HEAD_W).astype(F32), gmat_ones)


def _outproj_kernel(*refs, with_router):
    if with_router:
        a_ref, b_ref, w_ref, x_ref, g_ref, r_hi_ref, r_lo_ref, x1_ref, h_ref, route_ref = refs
    else:
        a_ref, b_ref, w_ref, x_ref, g_ref, x1_ref, h_ref = refs
    acc = (jnp.dot(a_ref[...], w_ref[0:SEG_W, :], preferred_element_type=F32)
           + jnp.dot(b_ref[...], w_ref[SEG_W:2 * SEG_W, :], preferred_element_type=F32))
    x1 = x_ref[...] + acc
    x1_ref[...] = x1
    ms = jnp.mean(x1 * x1, axis=-1, keepdims=True)
    hn = x1 * lax.rsqrt(ms + EPS) * g_ref[...]
    if not with_router:
        h_ref[...] = hn.astype(BF16)
    else:
        tm = hn.shape[0]
        for s in range(ROW_CHUNKS):
            h_ref[pl.ds(s, tm, stride=ROW_CHUNKS), :] = hn[:, s * HEAD_W:(s + 1) * HEAD_W]
        h_hi = hn.astype(BF16)
        h_lo = (hn - h_hi.astype(F32)).astype(BF16)
        logits = (jnp.dot(h_hi, r_hi_ref[...], preferred_element_type=F32)
                  + jnp.dot(h_hi, r_lo_ref[...], preferred_element_type=F32)
                  + jnp.dot(h_lo, r_hi_ref[...], preferred_element_type=F32))
        lane = lax.broadcasted_iota(jnp.int32, logits.shape, 1)
        ninf = jnp.float32(-jnp.inf)
        lg = jnp.where(lane < N_EXPERTS, logits, ninf)
        v1 = jnp.max(lg, axis=-1, keepdims=True)
        i1 = jnp.min(jnp.where(lg == v1, lane, HEAD_W), axis=-1, keepdims=True)
        lg2 = jnp.where(lane == i1, ninf, lg)
        v2 = jnp.max(lg2, axis=-1, keepdims=True)
        i2 = jnp.min(jnp.where(lg2 == v2, lane, HEAD_W), axis=-1, keepdims=True)
        g1 = 1.0 / (1.0 + jnp.exp(v2 - v1))
        g2 = 1.0 - g1
        route_ref[...] = jnp.where(lane == 0, i1.astype(F32),
                                   jnp.where(lane == 1, i2.astype(F32),
                                             jnp.where(lane == 2, g1, jnp.where(lane == 3, g2, 0.0))))


def _outproj(out_a, out_b, w_out, x, f_norm, router=None):
    seq = x.shape[0]
    tm = ROW_TILE
    with_router = router is not None
    in_specs = [pl.BlockSpec((tm, SEG_W), lambda i: (i, 0)),
                pl.BlockSpec((tm, SEG_W), lambda i: (i, 0)),
                pl.BlockSpec((D_MODEL, D_MODEL), lambda i: (0, 0)),
                pl.BlockSpec((tm, D_MODEL), lambda i: (i, 0)),
                pl.BlockSpec((1, D_MODEL), lambda i: (0, 0))]
    args = [out_a, out_b, w_out.astype(BF16), x, f_norm.reshape(1, D_MODEL).astype(F32)]
    out_shape = [jax.ShapeDtypeStruct((seq, D_MODEL), F32)]
    out_specs = [pl.BlockSpec((tm, D_MODEL), lambda i: (i, 0))]
    if not with_router:
        out_shape.append(jax.ShapeDtypeStruct((seq, D_MODEL), BF16))
        out_specs.append(pl.BlockSpec((tm, D_MODEL), lambda i: (i, 0)))
    else:
        out_shape.append(jax.ShapeDtypeStruct((seq * ROW_CHUNKS, HEAD_W), F32))
        out_specs.append(pl.BlockSpec((tm * ROW_CHUNKS, HEAD_W), lambda i: (i, 0)))
        r = jnp.zeros((D_MODEL, HEAD_W), F32).at[:, :N_EXPERTS].set(router.astype(F32))
        r_hi = r.astype(BF16)
        r_lo = (r - r_hi.astype(F32)).astype(BF16)
        in_specs += [pl.BlockSpec((D_MODEL, HEAD_W), lambda i: (0, 0))] * 2
        args += [r_hi, r_lo]
        out_shape.append(jax.ShapeDtypeStruct((seq, HEAD_W), F32))
        out_specs.append(pl.BlockSpec((tm, HEAD_W), lambda i: (i, 0)))
    return pl.pallas_call(
        functools.partial(_outproj_kernel, with_router=with_router),
        out_shape=tuple(out_shape),
        grid=(seq // tm,),
        in_specs=in_specs,
        out_specs=tuple(out_specs),
        compiler_params=_cparams(("arbitrary",)),
        name="outproj",
    )(*args)


def _swiglu_part(h, wg, wu, wd):
    g = jnp.dot(h, wg, preferred_element_type=F32)
    u = jnp.dot(h, wu, preferred_element_type=F32)
    act = (g * (1.0 / (1.0 + jnp.exp(-g))) * u).astype(BF16)
    return jnp.dot(act, wd, preferred_element_type=F32)


def _dense_ffn_kernel(h_ref, x_ref, wg_ref, wu_ref, wd_ref, o_ref):
    j = pl.program_id(1)
    part = _swiglu_part(h_ref[...], wg_ref[...], wu_ref[...], wd_ref[...])

    @pl.when(j == 0)
    def _():
        o_ref[...] = x_ref[...] + part

    @pl.when(j != 0)
    def _():
        o_ref[...] += part


def _dense_ffn(h, x1, wg, wu, wd):
    seq = x1.shape[0]
    tm, tf = ROW_TILE, FF_TILE
    return pl.pallas_call(
        _dense_ffn_kernel,
        out_shape=jax.ShapeDtypeStruct((seq, D_MODEL), F32),
        grid=(seq // tm, D_FF // tf),
        in_specs=[pl.BlockSpec((tm, D_MODEL), lambda i, j: (i, 0)),
                  pl.BlockSpec((tm, D_MODEL), lambda i, j: (i, 0)),
                  pl.BlockSpec((D_MODEL, tf), lambda i, j: (0, j)),
                  pl.BlockSpec((D_MODEL, tf), lambda i, j: (0, j)),
                  pl.BlockSpec((tf, D_MODEL), lambda i, j: (j, 0))],
        out_specs=pl.BlockSpec((tm, D_MODEL), lambda i, j: (i, 0)),
        compiler_params=_cparams(("arbitrary", "arbitrary")),
        name="dense_ffn",
    )(h, x1, wg.astype(BF16), wu.astype(BF16), wd.astype(BF16))


def _moe_ffn_kernel(be_ref, nused_ref, nvalid_ref, src_ref, src_next_ref, dst_ref, dst_prev_ref,
                    h3_ref, wg_ref, wu_ref, wd_ref, y_ref, x3, xb, acc, o3, gsem, ssem):
    b = pl.program_id(0)
    j = pl.program_id(1)
    nb = pl.num_programs(0)
    nj = pl.num_programs(1)
    tm = xb.shape[0]
    n_used = nused_ref[0]
    used = b < n_used

    def slab(ref, row0):
        return ref.at[pl.ds(pl.multiple_of(row0, 8), ROW_CHUNKS), :]

    def start_gather(idx_ref, slot):
        def start(r, c):
            pltpu.make_async_copy(slab(h3_ref, idx_ref[0, r]), slab(x3.at[slot], r * SLAB_PITCH),
                                  gsem.at[slot]).start()
            return c

        lax.fori_loop(0, tm, start, 0)

    def scatter_copy(idx_ref, r):
        return pltpu.make_async_copy(slab(o3, r * SLAB_PITCH), slab(y_ref, idx_ref[0, r]), ssem)

    @pl.when(jnp.logical_and(j == 0, jnp.logical_and(b == 0, used)))
    def _():
        start_gather(src_ref, 0)

    @pl.when(jnp.logical_and(j == 0, used))
    def _():
        slot = b % 2
        n_rows = tm * ROW_CHUNKS
        pltpu.make_async_copy(h3_ref.at[pl.ds(0, n_rows), :], x3.at[slot, pl.ds(0, n_rows), :],
                              gsem.at[slot]).wait()
        for s in range(ROW_CHUNKS):
            xb[:, s * HEAD_W:(s + 1) * HEAD_W] = x3[slot, pl.ds(s, tm, stride=SLAB_PITCH), :].astype(BF16)

    @pl.when(jnp.logical_and(j == 0, b + 1 < n_used))
    def _():
        start_gather(src_next_ref, (b + 1) % 2)

    @pl.when(used)
    def _():
        part = _swiglu_part(xb[...], wg_ref[...], wu_ref[...], wd_ref[...])

        @pl.when(j == 0)
        def _():
            acc[...] = part

        @pl.when(j != 0)
        def _():
            acc[...] += part

    def wait_scatter(idx_ref, n):
        def wait(r, c):
            scatter_copy(idx_ref, r).wait()
            return c

        lax.fori_loop(0, n, wait, 0)

    @pl.when(jnp.logical_and(j == nj - 1, jnp.logical_and(b >= 1, b - 1 < n_used)))
    def _():
        wait_scatter(dst_prev_ref, nvalid_ref[jnp.maximum(b - 1, 0)])

    @pl.when(jnp.logical_and(j == nj - 1, used))
    def _():
        for s in range(ROW_CHUNKS):
            o3[pl.ds(s, tm, stride=SLAB_PITCH), :] = acc[:, s * HEAD_W:(s + 1) * HEAD_W]
        n_valid = nvalid_ref[b]

        def start(r, c):
            scatter_copy(dst_ref, r).start()
            return c

        lax.fori_loop(0, n_valid, start, 0)

        @pl.when(b == nb - 1)
        def _():
            wait_scatter(dst_ref, n_valid)


def _moe_ffn(h3, src_idx, dst_idx, blk_expert, n_used, n_valid, wg, wu, wd):
    seq = h3.shape[0] // ROW_CHUNKS
    n_blocks = src_idx.shape[0]
    tm, tf = MOE_TILE, MOE_FF_TILE
    nj = D_EXPERT // tf

    def jj(b, j, be, nu, nv):
        return jnp.where(b < nu[0], j, nj - 1)

    def idx_spec(shift):
        return pl.BlockSpec((None, 1, tm), lambda b, j, be, nu, nv: (jnp.clip(b + shift, 0, n_blocks - 1), 0, 0),
                            memory_space=pltpu.SMEM)

    return pl.pallas_call(
        _moe_ffn_kernel,
        out_shape=jax.ShapeDtypeStruct((2 * seq * ROW_CHUNKS, HEAD_W), F32),
        grid_spec=pltpu.PrefetchScalarGridSpec(
            num_scalar_prefetch=3,
            grid=(n_blocks, nj),
            in_specs=[idx_spec(0), idx_spec(1), idx_spec(0), idx_spec(-1),
                      pl.BlockSpec(memory_space=pl.ANY),
                      pl.BlockSpec((None, D_MODEL, tf), lambda b, j, be, nu, nv: (be[b], 0, jj(b, j, be, nu, nv))),
                      pl.BlockSpec((None, D_MODEL, tf), lambda b, j, be, nu, nv: (be[b], 0, jj(b, j, be, nu, nv))),
                      pl.BlockSpec((None, tf, D_MODEL), lambda b, j, be, nu, nv: (be[b], jj(b, j, be, nu, nv), 0))],
            out_specs=pl.BlockSpec(memory_space=pl.ANY),
            scratch_shapes=[pltpu.VMEM((2, tm * SLAB_PITCH, HEAD_W), F32),
                            pltpu.VMEM((tm, D_MODEL), BF16),
                            pltpu.VMEM((tm, D_MODEL), F32),
                            pltpu.VMEM((tm * SLAB_PITCH, HEAD_W), F32),
                            pltpu.SemaphoreType.DMA((2,)),
                            pltpu.SemaphoreType.DMA(())]),
        compiler_params=_cparams(("arbitrary", "arbitrary")),
        name="moe_ffn",
    )(blk_expert, n_used, n_valid, src_idx, src_idx, dst_idx, dst_idx, h3, wg, wu, wd)


def _combine_kernel(x_ref, ya_ref, yb_ref, route_ref, o_ref):
    r = route_ref[...]
    g1 = jnp.broadcast_to(r[:, 2:3], (r.shape[0], HEAD_W))
    g2 = jnp.broadcast_to(r[:, 3:4], (r.shape[0], HEAD_W))
    tm = r.shape[0]
    for s in range(ROW_CHUNKS):
        cols = slice(s * HEAD_W, (s + 1) * HEAD_W)
        rows = pl.ds(s, tm, stride=ROW_CHUNKS)
        o_ref[:, cols] = x_ref[:, cols] + g1 * ya_ref[rows, :] + g2 * yb_ref[rows, :]


def _moe_combine(x1, y, route):
    seq = x1.shape[0]
    tm = ROW_TILE
    nb = seq // tm
    big = pl.BlockSpec((tm, D_MODEL), lambda i: (i, 0))
    return pl.pallas_call(
        _combine_kernel,
        out_shape=jax.ShapeDtypeStruct((seq, D_MODEL), F32),
        grid=(nb,),
        in_specs=[big,
                  pl.BlockSpec((tm * ROW_CHUNKS, HEAD_W), lambda i: (i, 0)),
                  pl.BlockSpec((tm * ROW_CHUNKS, HEAD_W), lambda i: (i + nb, 0)),
                  pl.BlockSpec((tm, HEAD_W), lambda i: (i, 0))],
        out_specs=big,
        compiler_params=_cparams(("arbitrary",)),
        name="moe_combine",
    )(x1, y, y, route)


def _moe_plan(route, seq):
    tm = MOE_TILE
    n_assign = 2 * seq
    n_blocks = n_assign // tm + N_EXPERTS
    e = route[:, 0:2].astype(jnp.int32).reshape(-1)
    onehot = (e[:, None] == jnp.arange(N_EXPERTS, dtype=jnp.int32)[None, :]).astype(jnp.int32)
    csum = jnp.cumsum(onehot, axis=0)
    counts = csum[-1]
    rank = jnp.sum((csum - onehot) * onehot, axis=1)
    nblk = (counts + tm - 1) // tm
    cum_blk = jnp.cumsum(nblk)
    pad_start = (cum_blk - nblk) * tm
    pos = pad_start[e] + rank
    assign = jnp.arange(n_assign, dtype=jnp.int32)
    tok = assign // 2
    src_idx = jnp.zeros((n_blocks * tm,), jnp.int32).at[pos].set(tok * ROW_CHUNKS)
    dst_idx = jnp.zeros((n_blocks * tm,), jnp.int32).at[pos].set(((assign % 2) * seq + tok) * ROW_CHUNKS)
    n_used = cum_blk[-1].astype(jnp.int32)
    blk = jnp.arange(n_blocks, dtype=jnp.int32)
    blk_expert = jnp.searchsorted(cum_blk, jnp.minimum(blk, n_used - 1), side='right').astype(jnp.int32)
    blk_expert = jnp.minimum(blk_expert, N_EXPERTS - 1)
    first_blk = cum_blk - nblk
    n_valid = jnp.clip(counts[blk_expert] - (blk - first_blk[blk_expert]) * tm, 0, tm).astype(jnp.int32)
    return (src_idx.reshape(n_blocks, 1, tm), dst_idx.reshape(n_blocks, 1, tm), blk_expert,
            n_used.reshape(1), n_valid)


def _moe(h3, x1, route, wg, wu, wd):
    seq = x1.shape[0]
    src_idx, dst_idx, blk_expert, n_used, n_valid = _moe_plan(route, seq)
    y = _moe_ffn(h3, src_idx, dst_idx, blk_expert, n_used, n_valid,
                 wg.astype(BF16), wu.astype(BF16), wd.astype(BF16))
    return _moe_combine(x1, y, route)


def _layer(x, i, a_norm, w_in, dqn, dkn, lq1, lk1, lq2, lk2, don, bqn, bkn, bon, w_out, f_norm, ffn):
    seq = x.shape[0]
    lam_init = 0.8 - 0.6 * math.exp(-0.3 * i)
    gains, gmat, cos_t, sin_t = _segment_tables(seq, dqn, dkn, bqn, bkn)
    qaT, ka, vaT, qbT, kb, vbT = _inproj(x, a_norm, _permute_w_in(w_in), gains, gmat, cos_t, sin_t)
    out_a = _diff_attn(qaT, ka, vaT, lq1, lk1, lq2, lk2, don, gmat[0], lam_init)
    out_b = _dil_attn(qbT, kb, vbT, bon, gmat[1])
    if len(ffn) == 3:
        x1, h = _outproj(out_a, out_b, w_out, x, f_norm)
        return _dense_ffn(h, x1, *ffn)
    router, wg, wu, wd = ffn
    x1, h3, route = _outproj(out_a, out_b, w_out, x, f_norm, router)
    return _moe(h3, x1, route, wg, wu, wd)


def kernel(x, attn_norm_0, w_in_0, diff_q_norm_0, diff_k_norm_0, diff_lam_q1_0, diff_lam_k1_0, diff_lam_q2_0, diff_lam_k2_0, diff_out_norm_0, dil_q_norm_0, dil_k_norm_0, dil_out_norm_0, w_out_0, ffn_norm_0, ffn_w_gate_0, ffn_w_up_0, ffn_w_down_0, attn_norm_1, w_in_1, diff_q_norm_1, diff_k_norm_1, diff_lam_q1_1, diff_lam_k1_1, diff_lam_q2_1, diff_lam_k2_1, diff_out_norm_1, dil_q_norm_1, dil_k_norm_1, dil_out_norm_1, w_out_1, ffn_norm_1, router_1, moe_w_gate_1, moe_w_up_1, moe_w_down_1):
    b, seq, d = x.shape
    assert b == 1 and d == D_MODEL and seq % ROW_TILE == 0
    xs = x.reshape(seq, d)
    xs = _layer(xs, 0, attn_norm_0, w_in_0, diff_q_norm_0, diff_k_norm_0, diff_lam_q1_0, diff_lam_k1_0,
                diff_lam_q2_0, diff_lam_k2_0, diff_out_norm_0, dil_q_norm_0, dil_k_norm_0, dil_out_norm_0,
                w_out_0, ffn_norm_0, (ffn_w_gate_0, ffn_w_up_0, ffn_w_down_0))
    xs = _layer(xs, 1, attn_norm_1, w_in_1, diff_q_norm_1, diff_k_norm_1, diff_lam_q1_1, diff_lam_k1_1,
                diff_lam_q2_1, diff_lam_k2_1, diff_out_norm_1, dil_q_norm_1, dil_k_norm_1, dil_out_norm_1,
                w_out_1, ffn_norm_1, (router_1, moe_w_gate_1, moe_w_up_1, moe_w_down_1))
    return xs.reshape(b, seq, d)
```

```python
import functools
import math

import numpy as np
import jax
import jax.numpy as jnp
from jax import lax
from jax.experimental import pallas as pl
from jax.experimental.pallas import tpu as pltpu

F32 = jnp.float32
BF16 = jnp.bfloat16

D_MODEL = 2048
N_HEADS = 8
HEAD_W = 128
DIFF_DIM = 64
SEG_W = N_HEADS * HEAD_W
N_SEG = 6
ROW_CHUNKS = D_MODEL // HEAD_W
SLAB_PITCH = 24
DIL_BRANCHES = ((128, 1), (512, 4), (2048, 16))
N_SIDE = 64
D_FF = 5632
N_EXPERTS = 8
D_EXPERT = 7168
EPS = 1e-6
NEG = -1e30
LOG2E = 1.4426950408889634

MAP0_LANE = 0
MAP1_LANE = 32
ROW_SUM_FLOOR = 2.0 ** -80

V_ROWS = 144
DIL_PAD = 1024
VMEM_LIMIT = 56 * 1024 * 1024

ROW_TILE = 512
ATT_TQ = 512
ATT_TK = 2048
DIL_TQ = 256
DIL_SUBTILES = 2
DIL_CHUNK = 768
FF_TILE = 512
D_EXPERT_STEPS = 7
MOE_FF_TILE = D_EXPERT // D_EXPERT_STEPS
MOE_TILE = 512


def _cparams(sem):
    return pltpu.CompilerParams(dimension_semantics=sem, vmem_limit_bytes=VMEM_LIMIT)


def _diff_lane_perm():
    perm = np.zeros(HEAD_W, np.int32)
    for m in range(2):
        for t in range(DIFF_DIM):
            p = (t // 32) * 64 + m * 32 + (t % 32)
            perm[p] = m * DIFF_DIM + t
    return perm


def _segment_tables(seq, dqn, dkn, bqn, bkn):
    perm = _diff_lane_perm()
    t_of_lane = perm % DIFF_DIM
    ones = jnp.ones((SEG_W,), F32)
    g_qa = jnp.tile(dqn.astype(F32)[t_of_lane], N_HEADS)
    g_ka = jnp.tile(dkn.astype(F32)[t_of_lane], N_HEADS)
    g_qb = jnp.tile(bqn.astype(F32), N_HEADS)
    g_kb = jnp.tile(bkn.astype(F32), N_HEADS)
    gains = jnp.stack([g_qa, g_ka, ones, g_qb, g_kb, ones]).reshape(N_SEG, 1, SEG_W)

    lane = np.arange(HEAD_W)
    map_of_lane = (lane // 32) % 2
    g_diff = (map_of_lane[:, None] == map_of_lane[None, :]).astype(np.float32)
    g_dil = np.ones((HEAD_W, HEAD_W), np.float32)
    gmat = jnp.asarray(np.stack([g_diff, g_dil]), BF16)

    pos = jnp.arange(seq, dtype=F32)

    def ang(dim):
        inv = 10000.0 ** (-jnp.arange(0, dim, 2, dtype=F32) / dim)
        return pos[:, None] * inv[None, :]

    a32 = ang(DIFF_DIM)
    a64 = ang(HEAD_W)
    cos_a = jnp.tile(jnp.cos(a32), (1, 4))
    sin_a = jnp.tile(jnp.sin(a32), (1, 4))
    cos_b = jnp.tile(jnp.cos(a64), (1, 2))
    sin_b = jnp.tile(jnp.sin(a64), (1, 2))
    sign = jnp.where(jnp.arange(HEAD_W) < 64, -1.0, 1.0).astype(F32)[None, :]
    cos_t = jnp.stack([cos_a, cos_b])
    sin_t = jnp.stack([sin_a * sign, sin_b * sign])
    return gains, gmat, cos_t, sin_t


def _permute_w_in(w_in):
    perm = _diff_lane_perm()
    cols = np.arange(N_SEG * SEG_W)
    for seg in (0, 1):
        for h in range(N_HEADS):
            base = seg * SEG_W + h * HEAD_W
            cols[base:base + HEAD_W] = base + perm
    return w_in[:, cols].astype(BF16)


def _inproj_kernel(x_ref, g_ref, w_ref, gain_ref, cos_ref, sin_ref, gmat_ref,
                   qaT_ref, ka_ref, vaT_ref, qbT_ref, kb_ref, vbT_ref, h_scr,
                   *, n_row_blocks, pad_blocks):
    ip = pl.program_id(0)
    j = pl.program_id(1)
    real = jnp.logical_and(ip >= pad_blocks, ip < pad_blocks + n_row_blocks)
    tm = x_ref.shape[0]

    @pl.when(jnp.logical_and(real, j == 0))
    def _():
        x = x_ref[...]
        ms = jnp.mean(x * x, axis=-1, keepdims=True)
        h_scr[...] = (x * lax.rsqrt(ms + EPS) * g_ref[...]).astype(BF16)

    def norm_rope(y, c, n_group, scale):
        sq = y * y
        sq_hi = sq.astype(BF16)
        sq_lo = (sq - sq_hi.astype(F32)).astype(BF16)
        gm = gmat_ref[...]
        ss = (jnp.dot(sq_hi, gm, preferred_element_type=F32)
              + jnp.dot(sq_lo, gm, preferred_element_type=F32))
        yn = y * lax.rsqrt(ss * (1.0 / n_group) + EPS) * gain_ref[:, c * HEAD_W:(c + 1) * HEAD_W]
        out = yn * cos_ref[...] + pltpu.roll(yn, 64, 1) * sin_ref[...]
        if scale != 1.0:
            out = out * scale
        return out

    def aug_rows():
        row = lax.broadcasted_iota(jnp.int32, (V_ROWS - HEAD_W, tm), 0)
        return jnp.where(row == 0, 1.0, 0.0).astype(BF16)

    def segment(seg):
        acc = jnp.dot(h_scr[...], w_ref[...], preferred_element_type=F32)
        for c in range(N_HEADS):
            y = acc[:, c * HEAD_W:(c + 1) * HEAD_W]
            if seg == 0:
                qaT_ref[c] = norm_rope(y, c, DIFF_DIM, DIFF_DIM ** -0.5 * LOG2E).T.astype(BF16)
            elif seg == 1:
                ka_ref[:, c * HEAD_W:(c + 1) * HEAD_W] = norm_rope(y, c, DIFF_DIM, 1.0).astype(BF16)
            elif seg == 3:
                qbT_ref[c] = norm_rope(y, c, HEAD_W, HEAD_W ** -0.5 * LOG2E).T.astype(BF16)
            elif seg == 4:
                kb_ref[:, c * HEAD_W:(c + 1) * HEAD_W] = norm_rope(y, c, HEAD_W, 1.0).astype(BF16)
            elif seg == 2:
                vaT_ref[c] = y.T.astype(BF16)
            else:
                vbT_ref[c, 0:HEAD_W, :] = y.T.astype(BF16)
                vbT_ref[c, HEAD_W:V_ROWS, :] = aug_rows()

    for seg in range(N_SEG):
        pl.when(jnp.logical_and(real, j == seg))(functools.partial(segment, seg))

    @pl.when(jnp.logical_and(jnp.logical_not(real), j == 4))
    def _():
        kb_ref[...] = jnp.zeros(kb_ref.shape, BF16)

    @pl.when(jnp.logical_and(jnp.logical_not(real), j == 5))
    def _():
        vbT_ref[...] = jnp.zeros(vbT_ref.shape, BF16)


def _inproj(x, a_norm, w_in_p, gains, gmat, cos_t, sin_t):
    seq = x.shape[0]
    tm = ROW_TILE
    nrb = seq // tm
    pb = DIL_PAD // tm
    seq_p = seq + 2 * DIL_PAD

    def row(ip):
        return jnp.clip(ip - pb, 0, nrb - 1)

    kern = functools.partial(_inproj_kernel, n_row_blocks=nrb, pad_blocks=pb)
    out_shape = (
        jax.ShapeDtypeStruct((N_HEADS, HEAD_W, seq), BF16),
        jax.ShapeDtypeStruct((seq, SEG_W), BF16),
        jax.ShapeDtypeStruct((N_HEADS, HEAD_W, seq), BF16),
        jax.ShapeDtypeStruct((N_HEADS, HEAD_W, seq), BF16),
        jax.ShapeDtypeStruct((seq_p, SEG_W), BF16),
        jax.ShapeDtypeStruct((N_HEADS, V_ROWS, seq_p), BF16),
    )
    in_specs = [
        pl.BlockSpec((tm, D_MODEL), lambda ip, j: (row(ip), 0)),
        pl.BlockSpec((1, D_MODEL), lambda ip, j: (0, 0)),
        pl.BlockSpec((D_MODEL, SEG_W), lambda ip, j: (0, j)),
        pl.BlockSpec((None, 1, SEG_W), lambda ip, j: (j, 0, 0)),
        pl.BlockSpec((None, tm, HEAD_W), lambda ip, j: (j // 3, row(ip), 0)),
        pl.BlockSpec((None, tm, HEAD_W), lambda ip, j: (j // 3, row(ip), 0)),
        pl.BlockSpec((None, HEAD_W, HEAD_W), lambda ip, j: (j // 3, 0, 0)),
    ]
    out_specs = (
        pl.BlockSpec((N_HEADS, HEAD_W, tm), lambda ip, j: (0, 0, row(ip))),
        pl.BlockSpec((tm, SEG_W), lambda ip, j: (row(ip), 0)),
        pl.BlockSpec((N_HEADS, HEAD_W, tm), lambda ip, j: (0, 0, row(ip))),
        pl.BlockSpec((N_HEADS, HEAD_W, tm), lambda ip, j: (0, 0, row(ip))),
        pl.BlockSpec((tm, SEG_W), lambda ip, j: (ip, 0)),
        pl.BlockSpec((N_HEADS, V_ROWS, tm), lambda ip, j: (0, 0, ip)),
    )
    return pl.pallas_call(
        kern,
        out_shape=out_shape,
        grid=(nrb + 2 * pb, N_SEG),
        in_specs=in_specs,
        out_specs=out_specs,
        scratch_shapes=[pltpu.VMEM((tm, D_MODEL), BF16)],
        compiler_params=_cparams(("arbitrary", "arbitrary")),
        name="inproj",
    )(x, a_norm.reshape(1, D_MODEL).astype(F32), w_in_p, gains, cos_t, sin_t, gmat)


def _flash_step(s, m_old, acc_ref, vT_t):
    m_new = jnp.maximum(m_old, jnp.max(s, axis=0, keepdims=True))
    alpha = jnp.exp2(m_old - m_new)
    p = jnp.exp2((s - m_new).astype(BF16))
    acc_ref[...] = acc_ref[...] * alpha + jnp.dot(vT_t, p, preferred_element_type=F32)
    return m_new


def _flash_step_sum(s, m_old, l_old, acc_ref, vT_t):
    m_new = jnp.maximum(m_old, jnp.max(s, axis=0, keepdims=True))
    alpha = jnp.exp2(m_old - m_new)
    e = jnp.exp2(s - m_new)
    acc_ref[...] = acc_ref[...] * alpha + jnp.dot(vT_t, e.astype(BF16), preferred_element_type=F32)
    return m_new, l_old * alpha + jnp.sum(e, axis=0, keepdims=True)


def _head_out(aT, gain_row, out_scale):
    a = aT.T
    ms = jnp.mean(a * a, axis=-1, keepdims=True)
    y = a * lax.rsqrt(ms + EPS) * gain_row
    if out_scale != 1.0:
        y = y * out_scale
    return y


def _diff_attn_kernel(lq1_ref, lk1_ref, lq2_ref, lk2_ref, qT_ref, k_ref, vT_ref, og_ref, gmat_ref, o_ref,
                      acc0, acc1, lsum, k0_scr, k1_scr, kmax_scr, *, lam_init, tk):
    tq = qT_ref.shape[1]
    seq = k_ref.shape[0]
    lane0, lane1 = MAP1_LANE, MAP0_LANE

    @pl.when(pl.program_id(1) == 0)
    def _():
        ck = min(1024, seq)

        def kchunk(c, mx):
            rows = pl.ds(pl.multiple_of(c * ck, ck), ck)
            kc = k_ref[rows, :]
            kf = kc.astype(F32)
            n2 = jnp.dot((kf * kf).astype(BF16), gmat_ref[...], preferred_element_type=F32)
            lane = lax.broadcasted_iota(jnp.int32, kc.shape, 1)
            one = jnp.ones_like(kc)
            k0_scr[rows, :] = jnp.where(lane == lane0, one, kc)
            k1_scr[rows, :] = jnp.where(lane == lane1, one, kc)
            return jnp.maximum(mx, jnp.max(n2, axis=0, keepdims=True))

        kn2 = lax.fori_loop(0, seq // ck, kchunk, jnp.zeros((1, HEAD_W), F32))
        kmax_scr[...] = jnp.sqrt(kn2)

    qT = qT_ref[...]
    row = lax.broadcasted_iota(jnp.int32, qT.shape, 0)
    in_map1 = ((row // 32) % 2) == 1
    zero = jnp.zeros_like(qT)
    qf = qT.astype(F32)
    q2 = qf * qf
    nq0 = jnp.sum(jnp.where(in_map1, 0.0, q2), axis=0, keepdims=True)
    nq1 = jnp.sum(jnp.where(in_map1, q2, 0.0), axis=0, keepdims=True)
    kmax = kmax_scr[...]
    b0 = jnp.sqrt(nq0) * kmax[:, MAP0_LANE:MAP0_LANE + 1]
    b1 = jnp.sqrt(nq1) * kmax[:, MAP1_LANE:MAP1_LANE + 1]
    q0 = jnp.where(in_map1, zero, qT)
    q1 = jnp.where(in_map1, qT, zero)
    q0s = jnp.where(row == lane0, jnp.broadcast_to(-b0, qf.shape).astype(BF16), q0)
    q1s = jnp.where(row == lane1, jnp.broadcast_to(-b1, qf.shape).astype(BF16), q1)
    acc0[...] = jnp.zeros(acc0.shape, F32)
    acc1[...] = jnp.zeros(acc1.shape, F32)

    def body(kt, carry):
        l0, l1 = carry
        rows = pl.ds(pl.multiple_of(kt * tk, tk), tk)
        vT_t = vT_ref[:, rows]
        e0 = jnp.exp2(jnp.dot(k0_scr[rows, :], q0s, preferred_element_type=F32))
        acc0[...] += jnp.dot(vT_t, e0.astype(BF16), preferred_element_type=F32)
        e1 = jnp.exp2(jnp.dot(k1_scr[rows, :], q1s, preferred_element_type=F32))
        acc1[...] += jnp.dot(vT_t, e1.astype(BF16), preferred_element_type=F32)
        return l0 + jnp.sum(e0, axis=0, keepdims=True), l1 + jnp.sum(e1, axis=0, keepdims=True)

    l_init = jnp.zeros((1, tq), F32)
    l0, l1 = lax.fori_loop(0, seq // tk, body, (l_init, l_init))
    lsum[0:1, :] = l0
    lsum[1:2, :] = l1

    @pl.when(jnp.logical_not(jnp.min(jnp.minimum(l0, l1)) >= ROW_SUM_FLOOR))
    def _():
        acc0[...] = jnp.zeros(acc0.shape, F32)
        acc1[...] = jnp.zeros(acc1.shape, F32)

        def robust(kt, carry):
            m0, l0, m1, l1 = carry
            rows = pl.ds(pl.multiple_of(kt * tk, tk), tk)
            k_t = k_ref[rows, :]
            vT_t = vT_ref[:, rows]
            m0, l0 = _flash_step_sum(jnp.dot(k_t, q0, preferred_element_type=F32), m0, l0, acc0, vT_t)
            m1, l1 = _flash_step_sum(jnp.dot(k_t, q1, preferred_element_type=F32), m1, l1, acc1, vT_t)
            return m0, l0, m1, l1

        m_init = jnp.full((1, tq), NEG, F32)
        _, l0, _, l1 = lax.fori_loop(0, seq // tk, robust, (m_init, l_init, m_init, l_init))
        lsum[0:1, :] = l0
        lsum[1:2, :] = l1

    lam = (jnp.exp(jnp.sum(lq1_ref[...] * lk1_ref[...], axis=-1, keepdims=True))
           - jnp.exp(jnp.sum(lq2_ref[...] * lk2_ref[...], axis=-1, keepdims=True)) + lam_init)
    o0 = acc0[...] * (1.0 / lsum[0:1, :])
    o1 = acc1[...] * (1.0 / lsum[1:2, :])
    aT = o0 - lam * o1
    o_ref[...] = _head_out(aT, og_ref[...], 1.0 - lam_init).astype(o_ref.dtype)


def _diff_attn(qaT, ka, vaT, lq1, lk1, lq2, lk2, og, gmat_diff, lam_init):
    seq = ka.shape[0]
    tq = min(ATT_TQ, seq)
    tk = min(ATT_TK, seq)
    kern = functools.partial(_diff_attn_kernel, lam_init=lam_init, tk=tk)
    vec = lambda v: v.reshape(1, -1).astype(F32)
    small = pl.BlockSpec((1, DIFF_DIM), lambda h, qi: (0, 0))
    return pl.pallas_call(
        kern,
        out_shape=jax.ShapeDtypeStruct((seq, SEG_W), BF16),
        grid=(N_HEADS, seq // tq),
        in_specs=[small, small, small, small,
                  pl.BlockSpec((None, HEAD_W, tq), lambda h, qi: (h, 0, qi)),
                  pl.BlockSpec((seq, HEAD_W), lambda h, qi: (0, h)),
                  pl.BlockSpec((None, HEAD_W, seq), lambda h, qi: (h, 0, 0)),
                  pl.BlockSpec((1, HEAD_W), lambda h, qi: (0, 0)),
                  pl.BlockSpec((HEAD_W, HEAD_W), lambda h, qi: (0, 0))],
        out_specs=pl.BlockSpec((tq, HEAD_W), lambda h, qi: (qi, h)),
        scratch_shapes=[pltpu.VMEM((HEAD_W, tq), F32), pltpu.VMEM((HEAD_W, tq), F32),
                        pltpu.VMEM((8, tq), F32),
                        pltpu.VMEM((seq, HEAD_W), BF16), pltpu.VMEM((seq, HEAD_W), BF16),
                        pltpu.VMEM((1, HEAD_W), F32)],
        compiler_params=_cparams(("arbitrary", "arbitrary")),
        name="diff_attn",
    )(vec(lq1), vec(lk1), vec(lq2), vec(lk2), qaT, ka, vaT, vec(og), gmat_diff)


def _dil_chunks(tq):
    chunks, off = [], 0
    for _, dil in DIL_BRANCHES:
        pad = -(-(N_SIDE * dil) // 128) * 128
        total = tq + 2 * pad
        c0 = 0
        while c0 < total:
            nk = min(DIL_CHUNK, total - c0)
            chunks.append((dil, c0 - pad, nk, off))
            off += nk
            c0 += nk
    return chunks, off


def _dil_bias(tq):
    chunks, total = _dil_chunks(tq)
    bias = np.full((total, tq), NEG, np.float32)
    col = np.arange(tq)[None, :]
    for dil, rel, nk, off in chunks:
        delta = rel + np.arange(nk)[:, None] - col
        ok = (np.abs(delta) <= N_SIDE * dil) & (delta % dil == 0)
        bias[off:off + nk][ok] = 0.0
    return jnp.asarray(bias)


def _dil_attn_kernel(qT_ref, k_ref, vT_ref, bias_ref, og_ref, gmat_ref, o_ref, acc, kmax_scr, *, seq, chunks):
    tq = acc.shape[2]
    n_sub = acc.shape[0]
    base = pl.program_id(1) * (tq * n_sub)

    @pl.when(pl.program_id(1) == 0)
    def _():
        ck = 1024

        def kchunk(c, mx):
            kf = k_ref[pl.ds(pl.multiple_of(c * ck, ck), ck), :].astype(F32)
            n2 = jnp.dot((kf * kf).astype(BF16), gmat_ref[...], preferred_element_type=F32)
            return jnp.maximum(mx, jnp.max(n2, axis=0, keepdims=True))

        kn2 = lax.fori_loop(0, k_ref.shape[0] // ck, kchunk, jnp.zeros((1, HEAD_W), F32))
        kmax_scr[...] = jnp.sqrt(kn2)

    def windows(i0):
        for dil, rel, nk, off in chunks:
            start = pl.multiple_of(i0 + (DIL_PAD + rel), 128)
            yield rel, nk, off, k_ref[pl.ds(start, nk), :], vT_ref[:, pl.ds(start, nk)]

    lmin = None
    for t in range(n_sub):
        qT = qT_ref[:, t * tq:(t + 1) * tq]
        qf = qT.astype(F32)
        shift = jnp.sqrt(jnp.sum(qf * qf, axis=0, keepdims=True)) * kmax_scr[:, 0:1]
        num = jnp.zeros((V_ROWS, tq), F32)
        for rel, nk, off, k_t, vT_t in windows(base + t * tq):
            s = jnp.dot(k_t, qT, preferred_element_type=F32) - shift + bias_ref[off:off + nk, :]
            num = num + jnp.dot(vT_t, jnp.exp2(s).astype(BF16), preferred_element_type=F32)
        acc[t] = num
        l_t = jnp.min(num[HEAD_W:HEAD_W + 1, :])
        lmin = l_t if lmin is None else jnp.minimum(lmin, l_t)

    @pl.when(jnp.logical_not(lmin >= ROW_SUM_FLOOR))
    def _():
        for t in range(n_sub):
            i0 = base + t * tq
            qT = qT_ref[:, t * tq:(t + 1) * tq]
            acc[t] = jnp.zeros((V_ROWS, tq), F32)
            m = jnp.full((1, tq), NEG, F32)
            for rel, nk, off, k_t, vT_t in windows(i0):
                s = jnp.dot(k_t, qT, preferred_element_type=F32)
                kpos = lax.broadcasted_iota(jnp.int32, (nk, tq), 0) + (i0 + rel)
                valid = jnp.logical_and(kpos >= 0, kpos < seq)
                s = jnp.where(valid, s + bias_ref[off:off + nk, :], NEG)
                m = _flash_step(s, m, acc.at[t], vT_t)

    for t in range(n_sub):
        oT = acc[t, 0:HEAD_W, :] * (1.0 / acc[t, HEAD_W:HEAD_W + 1, :])
        o_ref[t * tq:(t + 1) * tq, :] = _head_out(oT, og_ref[...], 1.0).astype(o_ref.dtype)


def _dil_attn(qbT, kb, vbT, og, gmat_ones):
    seq = qbT.shape[2]
    seq_p = kb.shape[0]
    tq = min(DIL_TQ, seq)
    n_sub = DIL_SUBTILES
    chunks, _ = _dil_chunks(tq)
    bias = _dil_bias(tq)
    kern = functools.partial(_dil_attn_kernel, seq=seq, chunks=chunks)
    return pl.pallas_call(
        kern,
        out_shape=jax.ShapeDtypeStruct((seq, SEG_W), BF16),
        grid=(N_HEADS, seq // (tq * n_sub)),
        in_specs=[pl.BlockSpec((None, HEAD_W, tq * n_sub), lambda h, qi: (h, 0, qi)),
                  pl.BlockSpec((seq_p, HEAD_W), lambda h, qi: (0, h)),
                  pl.BlockSpec((None, V_ROWS, seq_p), lambda h, qi: (h, 0, 0)),
                  pl.BlockSpec(bias.shape, lambda h, qi: (0, 0)),
                  pl.BlockSpec((1, HEAD_W), lambda h, qi: (0, 0)),
                  pl.BlockSpec((HEAD_W, HEAD_W), lambda h, qi: (0, 0))],
        out_specs=pl.BlockSpec((tq * n_sub, HEAD_W), lambda h, qi: (qi, h)),
        scratch_shapes=[pltpu.VMEM((n_sub, V_ROWS, tq), F32), pltpu.VMEM((1, HEAD_W), F32)],
        compiler_params=_cparams(("arbitrary", "arbitrary")),
        name="dil_attn",
    )(qbT, kb, vbT, bias, og.reshape(1, HEAD_W).astype(F32), gmat_ones)


def _outproj_kernel(*refs, with_router):
    if with_router:
        a_ref, b_ref, w_ref, x_ref, g_ref, r_hi_ref, r_lo_ref, x1_ref, h_ref, route_ref = refs
    else:
        a_ref, b_ref, w_ref, x_ref, g_ref, x1_ref, h_ref = refs
    acc = (jnp.dot(a_ref[...], w_ref[0:SEG_W, :], preferred_element_type=F32)
           + jnp.dot(b_ref[...], w_ref[SEG_W:2 * SEG_W, :], preferred_element_type=F32))
    x1 = x_ref[...] + acc
    x1_ref[...] = x1
    ms = jnp.mean(x1 * x1, axis=-1, keepdims=True)
    hn = x1 * lax.rsqrt(ms + EPS) * g_ref[...]
    if not with_router:
        h_ref[...] = hn.astype(BF16)
    else:
        tm = hn.shape[0]
        for s in range(ROW_CHUNKS):
            h_ref[pl.ds(s, tm, stride=ROW_CHUNKS), :] = hn[:, s * HEAD_W:(s + 1) * HEAD_W]
        h_hi = hn.astype(BF16)
        h_lo = (hn - h_hi.astype(F32)).astype(BF16)
        logits = (jnp.dot(h_hi, r_hi_ref[...], preferred_element_type=F32)
                  + jnp.dot(h_hi, r_lo_ref[...], preferred_element_type=F32)
                  + jnp.dot(h_lo, r_hi_ref[...], preferred_element_type=F32))
        lane = lax.broadcasted_iota(jnp.int32, logits.shape, 1)
        ninf = jnp.float32(-jnp.inf)
        lg = jnp.where(lane < N_EXPERTS, logits, ninf)
        v1 = jnp.max(lg, axis=-1, keepdims=True)
        i1 = jnp.min(jnp.where(lg == v1, lane, HEAD_W), axis=-1, keepdims=True)
        lg2 = jnp.where(lane == i1, ninf, lg)
        v2 = jnp.max(lg2, axis=-1, keepdims=True)
        i2 = jnp.min(jnp.where(lg2 == v2, lane, HEAD_W), axis=-1, keepdims=True)
        g1 = 1.0 / (1.0 + jnp.exp(v2 - v1))
        g2 = 1.0 - g1
        route_ref[...] = jnp.where(lane == 0, i1.astype(F32),
                                   jnp.where(lane == 1, i2.astype(F32),
                                             jnp.where(lane == 2, g1, jnp.where(lane == 3, g2, 0.0))))


def _outproj(out_a, out_b, w_out, x, f_norm, router=None):
    seq = x.shape[0]
    tm = ROW_TILE
    with_router = router is not None
    in_specs = [pl.BlockSpec((tm, SEG_W), lambda i: (i, 0)),
                pl.BlockSpec((tm, SEG_W), lambda i: (i, 0)),
                pl.BlockSpec((D_MODEL, D_MODEL), lambda i: (0, 0)),
                pl.BlockSpec((tm, D_MODEL), lambda i: (i, 0)),
                pl.BlockSpec((1, D_MODEL), lambda i: (0, 0))]
    args = [out_a, out_b, w_out.astype(BF16), x, f_norm.reshape(1, D_MODEL).astype(F32)]
    out_shape = [jax.ShapeDtypeStruct((seq, D_MODEL), F32)]
    out_specs = [pl.BlockSpec((tm, D_MODEL), lambda i: (i, 0))]
    if not with_router:
        out_shape.append(jax.ShapeDtypeStruct((seq, D_MODEL), BF16))
        out_specs.append(pl.BlockSpec((tm, D_MODEL), lambda i: (i, 0)))
    else:
        out_shape.append(jax.ShapeDtypeStruct((seq * ROW_CHUNKS, HEAD_W), F32))
        out_specs.append(pl.BlockSpec((tm * ROW_CHUNKS, HEAD_W), lambda i: (i, 0)))
        r = jnp.zeros((D_MODEL, HEAD_W), F32).at[:, :N_EXPERTS].set(router.astype(F32))
        r_hi = r.astype(BF16)
        r_lo = (r - r_hi.astype(F32)).astype(BF16)
        in_specs += [pl.BlockSpec((D_MODEL, HEAD_W), lambda i: (0, 0))] * 2
        args += [r_hi, r_lo]
        out_shape.append(jax.ShapeDtypeStruct((seq, HEAD_W), F32))
        out_specs.append(pl.BlockSpec((tm, HEAD_W), lambda i: (i, 0)))
    return pl.pallas_call(
        functools.partial(_outproj_kernel, with_router=with_router),
        out_shape=tuple(out_shape),
        grid=(seq // tm,),
        in_specs=in_specs,
        out_specs=tuple(out_specs),
        compiler_params=_cparams(("arbitrary",)),
        name="outproj",
    )(*args)


def _swiglu_part(h, wg, wu, wd):
    g = jnp.dot(h, wg, preferred_element_type=F32)
    u = jnp.dot(h, wu, preferred_element_type=F32)
    act = (g * (1.0 / (1.0 + jnp.exp(-g))) * u).astype(BF16)
    return jnp.dot(act, wd, preferred_element_type=F32)


def _dense_ffn_kernel(h_ref, x_ref, wg_ref, wu_ref, wd_ref, o_ref):
    j = pl.program_id(1)
    part = _swiglu_part(h_ref[...], wg_ref[...], wu_ref[...], wd_ref[...])

    @pl.when(j == 0)
    def _():
        o_ref[...] = x_ref[...] + part

    @pl.when(j != 0)
    def _():
        o_ref[...] += part


def _dense_ffn(h, x1, wg, wu, wd):
    seq = x1.shape[0]
    tm, tf = ROW_TILE, FF_TILE
    return pl.pallas_call(
        _dense_ffn_kernel,
        out_shape=jax.ShapeDtypeStruct((seq, D_MODEL), F32),
        grid=(seq // tm, D_FF // tf),
        in_specs=[pl.BlockSpec((tm, D_MODEL), lambda i, j: (i, 0)),
                  pl.BlockSpec((tm, D_MODEL), lambda i, j: (i, 0)),
                  pl.BlockSpec((D_MODEL, tf), lambda i, j: (0, j)),
                  pl.BlockSpec((D_MODEL, tf), lambda i, j: (0, j)),
                  pl.BlockSpec((tf, D_MODEL), lambda i, j: (j, 0))],
        out_specs=pl.BlockSpec((tm, D_MODEL), lambda i, j: (i, 0)),
        compiler_params=_cparams(("arbitrary", "arbitrary")),
        name="dense_ffn",
    )(h, x1, wg.astype(BF16), wu.astype(BF16), wd.astype(BF16))


def _moe_ffn_kernel(be_ref, nused_ref, nvalid_ref, src_ref, src_next_ref, dst_ref, dst_prev_ref,
                    h3_ref, wg_ref, wu_ref, wd_ref, y_ref, x3, xb, acc, o3, gsem, ssem):
    b = pl.program_id(0)
    j = pl.program_id(1)
    nb = pl.num_programs(0)
    nj = pl.num_programs(1)
    tm = xb.shape[0]
    n_used = nused_ref[0]
    used = b < n_used

    def slab(ref, row0):
        return ref.at[pl.ds(pl.multiple_of(row0, 8), ROW_CHUNKS), :]

    def start_gather(idx_ref, slot):
        def start(r, c):
            pltpu.make_async_copy(slab(h3_ref, idx_ref[0, r]), slab(x3.at[slot], r * SLAB_PITCH),
                                  gsem.at[slot]).start()
            return c

        lax.fori_loop(0, tm, start, 0)

    def scatter_copy(idx_ref, r):
        return pltpu.make_async_copy(slab(o3, r * SLAB_PITCH), slab(y_ref, idx_ref[0, r]), ssem)

    def wait_gather(slot):
        n_rows = tm * ROW_CHUNKS
        pltpu.make_async_copy(h3_ref.at[pl.ds(0, n_rows), :], x3.at[slot, pl.ds(0, n_rows), :],
                              gsem.at[slot]).wait()

    @pl.when(jnp.logical_and(j == 0, jnp.logical_and(b == 0, used)))
    def _():
        start_gather(src_ref, 0)

    @pl.when(jnp.logical_and(j == 0, jnp.where(b == 0, used, b - 1 < n_used)))
    def _():
        wait_gather(b % 2)

    @pl.when(jnp.logical_and(j == 0, used))
    def _():
        slot = b % 2
        for s in range(ROW_CHUNKS):
            xb[:, s * HEAD_W:(s + 1) * HEAD_W] = x3[slot, pl.ds(s, tm, stride=SLAB_PITCH), :].astype(BF16)

    @pl.when(used)
    def _():
        per_step = tm // D_EXPERT_STEPS
        nslot = (b + 1) % 2

        def request(r):
            pltpu.make_async_copy(slab(h3_ref, src_next_ref[0, r]), slab(x3.at[nslot], r * SLAB_PITCH),
                                  gsem.at[nslot]).start()

        for i in range(per_step):
            request(j * per_step + i)

        @pl.when(j == 0)
        def _():
            for r in range(per_step * D_EXPERT_STEPS, tm):
                request(r)

        part = _swiglu_part(xb[...], wg_ref[...], wu_ref[...], wd_ref[...])

        @pl.when(j == 0)
        def _():
            acc[...] = part

        @pl.when(j != 0)
        def _():
            acc[...] += part

    def wait_scatter(idx_ref, n):
        def wait(r, c):
            scatter_copy(idx_ref, r).wait()
            return c

        lax.fori_loop(0, n, wait, 0)

    @pl.when(jnp.logical_and(j == nj - 1, jnp.logical_and(b >= 1, b - 1 < n_used)))
    def _():
        wait_scatter(dst_prev_ref, nvalid_ref[jnp.maximum(b - 1, 0)])

    @pl.when(jnp.logical_and(j == nj - 1, used))
    def _():
        for s in range(ROW_CHUNKS):
            o3[pl.ds(s, tm, stride=SLAB_PITCH), :] = acc[:, s * HEAD_W:(s + 1) * HEAD_W]
        n_valid = nvalid_ref[b]

        def start(r, c):
            scatter_copy(dst_ref, r).start()
            return c

        lax.fori_loop(0, n_valid, start, 0)

        @pl.when(b == nb - 1)
        def _():
            wait_scatter(dst_ref, n_valid)
            wait_gather((b + 1) % 2)


def _moe_ffn(h3, src_idx, dst_idx, blk_expert, n_used, n_valid, wg, wu, wd):
    seq = h3.shape[0] // ROW_CHUNKS
    n_blocks = src_idx.shape[0]
    tm, tf = MOE_TILE, MOE_FF_TILE
    nj = D_EXPERT // tf

    def jj(b, j, be, nu, nv):
        return jnp.where(b < nu[0], j, nj - 1)

    def idx_spec(shift):
        return pl.BlockSpec((None, 1, tm), lambda b, j, be, nu, nv: (jnp.clip(b + shift, 0, n_blocks - 1), 0, 0),
                            memory_space=pltpu.SMEM)

    return pl.pallas_call(
        _moe_ffn_kernel,
        out_shape=jax.ShapeDtypeStruct((2 * seq * ROW_CHUNKS, HEAD_W), F32),
        grid_spec=pltpu.PrefetchScalarGridSpec(
            num_scalar_prefetch=3,
            grid=(n_blocks, nj),
            in_specs=[idx_spec(0), idx_spec(1), idx_spec(0), idx_spec(-1),
                      pl.BlockSpec(memory_space=pl.ANY),
                      pl.BlockSpec((None, D_MODEL, tf), lambda b, j, be, nu, nv: (be[b], 0, jj(b, j, be, nu, nv))),
                      pl.BlockSpec((None, D_MODEL, tf), lambda b, j, be, nu, nv: (be[b], 0, jj(b, j, be, nu, nv))),
                      pl.BlockSpec((None, tf, D_MODEL), lambda b, j, be, nu, nv: (be[b], jj(b, j, be, nu, nv), 0))],
            out_specs=pl.BlockSpec(memory_space=pl.ANY),
            scratch_shapes=[pltpu.VMEM((2, tm * SLAB_PITCH, HEAD_W), F32),
                            pltpu.VMEM((tm, D_MODEL), BF16),
                            pltpu.VMEM((tm, D_MODEL), F32),
                            pltpu.VMEM((tm * SLAB_PITCH, HEAD_W), F32),
                            pltpu.SemaphoreType.DMA((2,)),
                            pltpu.SemaphoreType.DMA(())]),
        compiler_params=_cparams(("arbitrary", "arbitrary")),
        name="moe_ffn",
    )(blk_expert, n_used, n_valid, src_idx, src_idx, dst_idx, dst_idx, h3, wg, wu, wd)


def _combine_kernel(x_ref, ya_ref, yb_ref, route_ref, o_ref):
    r = route_ref[...]
    g1 = jnp.broadcast_to(r[:, 2:3], (r.shape[0], HEAD_W))
    g2 = jnp.broadcast_to(r[:, 3:4], (r.shape[0], HEAD_W))
    tm = r.shape[0]
    for s in range(ROW_CHUNKS):
        cols = slice(s * HEAD_W, (s + 1) * HEAD_W)
        rows = pl.ds(s, tm, stride=ROW_CHUNKS)
        o_ref[:, cols] = x_ref[:, cols] + g1 * ya_ref[rows, :] + g2 * yb_ref[rows, :]


def _moe_combine(x1, y, route):
    seq = x1.shape[0]
    tm = ROW_TILE
    nb = seq // tm
    big = pl.BlockSpec((tm, D_MODEL), lambda i: (i, 0))
    return pl.pallas_call(
        _combine_kernel,
        out_shape=jax.ShapeDtypeStruct((seq, D_MODEL), F32),
        grid=(nb,),
        in_specs=[big,
                  pl.BlockSpec((tm * ROW_CHUNKS, HEAD_W), lambda i: (i, 0)),
                  pl.BlockSpec((tm * ROW_CHUNKS, HEAD_W), lambda i: (i + nb, 0)),
                  pl.BlockSpec((tm, HEAD_W), lambda i: (i, 0))],
        out_specs=big,
        compiler_params=_cparams(("arbitrary",)),
        name="moe_combine",
    )(x1, y, y, route)


def _moe_plan(route, seq):
    tm = MOE_TILE
    n_assign = 2 * seq
    n_blocks = n_assign // tm + N_EXPERTS
    e = route[:, 0:2].astype(jnp.int32).reshape(-1)
    onehot = (e[:, None] == jnp.arange(N_EXPERTS, dtype=jnp.int32)[None, :]).astype(jnp.int32)
    csum = jnp.cumsum(onehot, axis=0)
    counts = csum[-1]
    rank = jnp.sum((csum - onehot) * onehot, axis=1)
    nblk = (counts + tm - 1) // tm
    cum_blk = jnp.cumsum(nblk)
    pad_start = (cum_blk - nblk) * tm
    pos = pad_start[e] + rank
    assign = jnp.arange(n_assign, dtype=jnp.int32)
    slot_assign = jnp.zeros((n_blocks * tm,), jnp.int32).at[pos].set(assign)
    tok = slot_assign // 2
    src_idx = tok * ROW_CHUNKS
    dst_idx = ((slot_assign % 2) * seq + tok) * ROW_CHUNKS
    n_used = cum_blk[-1].astype(jnp.int32)
    blk = jnp.arange(n_blocks, dtype=jnp.int32)
    blk_expert = jnp.searchsorted(cum_blk, jnp.minimum(blk, n_used - 1), side='right').astype(jnp.int32)
    blk_expert = jnp.minimum(blk_expert, N_EXPERTS - 1)
    first_blk = cum_blk - nblk
    n_valid = jnp.clip(counts[blk_expert] - (blk - first_blk[blk_expert]) * tm, 0, tm).astype(jnp.int32)
    return (src_idx.reshape(n_blocks, 1, tm), dst_idx.reshape(n_blocks, 1, tm), blk_expert,
            n_used.reshape(1), n_valid)


def _moe(h3, x1, route, wg, wu, wd):
    seq = x1.shape[0]
    src_idx, dst_idx, blk_expert, n_used, n_valid = _moe_plan(route, seq)
    y = _moe_ffn(h3, src_idx, dst_idx, blk_expert, n_used, n_valid,
                 wg.astype(BF16), wu.astype(BF16), wd.astype(BF16))
    return _moe_combine(x1, y, route)


def _layer(x, i, a_norm, w_in, dqn, dkn, lq1, lk1, lq2, lk2, don, bqn, bkn, bon, w_out, f_norm, ffn):
    seq = x.shape[0]
    lam_init = 0.8 - 0.6 * math.exp(-0.3 * i)
    gains, gmat, cos_t, sin_t = _segment_tables(seq, dqn, dkn, bqn, bkn)
    qaT, ka, vaT, qbT, kb, vbT = _inproj(x, a_norm, _permute_w_in(w_in), gains, gmat, cos_t, sin_t)
    out_a = _diff_attn(qaT, ka, vaT, lq1, lk1, lq2, lk2, don, gmat[0], lam_init)
    out_b = _dil_attn(qbT, kb, vbT, bon, gmat[1])
    if len(ffn) == 3:
        x1, h = _outproj(out_a, out_b, w_out, x, f_norm)
        return _dense_ffn(h, x1, *ffn)
    router, wg, wu, wd = ffn
    x1, h3, route = _outproj(out_a, out_b, w_out, x, f_norm, router)
    return _moe(h3, x1, route, wg, wu, wd)


def kernel(x, attn_norm_0, w_in_0, diff_q_norm_0, diff_k_norm_0, diff_lam_q1_0, diff_lam_k1_0, diff_lam_q2_0, diff_lam_k2_0, diff_out_norm_0, dil_q_norm_0, dil_k_norm_0, dil_out_norm_0, w_out_0, ffn_norm_0, ffn_w_gate_0, ffn_w_up_0, ffn_w_down_0, attn_norm_1, w_in_1, diff_q_norm_1, diff_k_norm_1, diff_lam_q1_1, diff_lam_k1_1, diff_lam_q2_1, diff_lam_k2_1, diff_out_norm_1, dil_q_norm_1, dil_k_norm_1, dil_out_norm_1, w_out_1, ffn_norm_1, router_1, moe_w_gate_1, moe_w_up_1, moe_w_down_1):
    b, seq, d = x.shape
    assert b == 1 and d == D_MODEL and seq % ROW_TILE == 0
    xs = x.reshape(seq, d)
    xs = _layer(xs, 0, attn_norm_0, w_in_0, diff_q_norm_0, diff_k_norm_0, diff_lam_q1_0, diff_lam_k1_0,
                diff_lam_q2_0, diff_lam_k2_0, diff_out_norm_0, dil_q_norm_0, dil_k_norm_0, dil_out_norm_0,
                w_out_0, ffn_norm_0, (ffn_w_gate_0, ffn_w_up_0, ffn_w_down_0))
    xs = _layer(xs, 1, attn_norm_1, w_in_1, diff_q_norm_1, diff_k_norm_1, diff_lam_q1_1, diff_lam_k1_1,
                diff_lam_q2_1, diff_lam_k2_1, diff_out_norm_1, dil_q_norm_1, dil_k_norm_1, dil_out_norm_1,
                w_out_1, ffn_norm_1, (router_1, moe_w_gate_1, moe_w_up_1, moe_w_down_1))
    return xs.reshape(b, seq, d)
```

```python
import functools
import math

import numpy as np
import jax
import jax.numpy as jnp
from jax import lax
from jax.experimental import pallas as pl
from jax.experimental.pallas import tpu as pltpu

F32 = jnp.float32
BF16 = jnp.bfloat16

D_MODEL = 2048
N_HEADS = 8
HEAD_W = 128
DIFF_DIM = 64
SEG_W = N_HEADS * HEAD_W
N_SEG = 6
ROW_CHUNKS = D_MODEL // HEAD_W
SLAB_PITCH = 24
DIL_BRANCHES = ((128, 1), (512, 4), (2048, 16))
N_SIDE = 64
D_FF = 5632
N_EXPERTS = 8
D_EXPERT = 7168
EPS = 1e-6
NEG = -1e30
LOG2E = 1.4426950408889634

MAP0_LANE = 0
MAP1_LANE = 32
ROW_SUM_FLOOR = 2.0 ** -80

V_ROWS = 144
DIL_PAD = 1024
VMEM_LIMIT = 56 * 1024 * 1024

ROW_TILE = 512
ATT_TQ = 512
ATT_TK = 8192
DIL_TQ = 256
DIL_SUBTILES = 4
DIL_CHUNK = 768
FF_TILE = 512
D_EXPERT_STEPS = 7
MOE_FF_TILE = D_EXPERT // D_EXPERT_STEPS
MOE_TILE = 512


def _cparams(sem):
    return pltpu.CompilerParams(dimension_semantics=sem, vmem_limit_bytes=VMEM_LIMIT)


def _diff_lane_perm():
    perm = np.zeros(HEAD_W, np.int32)
    for m in range(2):
        for t in range(DIFF_DIM):
            p = (t // 32) * 64 + m * 32 + (t % 32)
            perm[p] = m * DIFF_DIM + t
    return perm


def _segment_tables(seq, dqn, dkn, bqn, bkn):
    perm = _diff_lane_perm()
    t_of_lane = perm % DIFF_DIM
    ones = jnp.ones((SEG_W,), F32)
    g_qa = jnp.tile(dqn.astype(F32)[t_of_lane], N_HEADS)
    g_ka = jnp.tile(dkn.astype(F32)[t_of_lane], N_HEADS)
    g_qb = jnp.tile(bqn.astype(F32), N_HEADS)
    g_kb = jnp.tile(bkn.astype(F32), N_HEADS)
    gains = jnp.stack([g_qa, g_ka, ones, g_qb, g_kb, ones]).reshape(N_SEG, 1, SEG_W)

    lane = np.arange(HEAD_W)
    map_of_lane = (lane // 32) % 2
    g_diff = (map_of_lane[:, None] == map_of_lane[None, :]).astype(np.float32)
    g_dil = np.ones((HEAD_W, HEAD_W), np.float32)
    gmat = jnp.asarray(np.stack([g_diff, g_dil]), BF16)

    pos = jnp.arange(seq, dtype=F32)

    def ang(dim):
        inv = 10000.0 ** (-jnp.arange(0, dim, 2, dtype=F32) / dim)
        return pos[:, None] * inv[None, :]

    a32 = ang(DIFF_DIM)
    a64 = ang(HEAD_W)
    cos_a = jnp.tile(jnp.cos(a32), (1, 4))
    sin_a = jnp.tile(jnp.sin(a32), (1, 4))
    cos_b = jnp.tile(jnp.cos(a64), (1, 2))
    sin_b = jnp.tile(jnp.sin(a64), (1, 2))
    sign = jnp.where(jnp.arange(HEAD_W) < 64, -1.0, 1.0).astype(F32)[None, :]
    cos_t = jnp.stack([cos_a, cos_b])
    sin_t = jnp.stack([sin_a * sign, sin_b * sign])
    return gains, gmat, cos_t, sin_t


def _permute_w_in(w_in):
    perm = _diff_lane_perm()
    cols = np.arange(N_SEG * SEG_W)
    for seg in (0, 1):
        for h in range(N_HEADS):
            base = seg * SEG_W + h * HEAD_W
            cols[base:base + HEAD_W] = base + perm
    return w_in[:, cols].astype(BF16)


def _inproj_kernel(x_ref, g_ref, w_ref, gain_ref, cos_ref, sin_ref, gmat_ref,
                   qaT_ref, ka_ref, vaT_ref, qbT_ref, kb_ref, vbT_ref, h_scr,
                   *, n_row_blocks, pad_blocks):
    ip = pl.program_id(0)
    j = pl.program_id(1)
    real = jnp.logical_and(ip >= pad_blocks, ip < pad_blocks + n_row_blocks)
    tm = x_ref.shape[0]

    @pl.when(jnp.logical_and(real, j == 0))
    def _():
        x = x_ref[...]
        ms = jnp.mean(x * x, axis=-1, keepdims=True)
        h_scr[...] = (x * lax.rsqrt(ms + EPS) * g_ref[...]).astype(BF16)

    def norm_rope(y, c, n_group, scale):
        sq = y * y
        sq_hi = sq.astype(BF16)
        sq_lo = (sq - sq_hi.astype(F32)).astype(BF16)
        gm = gmat_ref[...]
        ss = (jnp.dot(sq_hi, gm, preferred_element_type=F32)
              + jnp.dot(sq_lo, gm, preferred_element_type=F32))
        yn = y * lax.rsqrt(ss * (1.0 / n_group) + EPS) * gain_ref[:, c * HEAD_W:(c + 1) * HEAD_W]
        out = yn * cos_ref[...] + pltpu.roll(yn, 64, 1) * sin_ref[...]
        if scale != 1.0:
            out = out * scale
        return out

    def aug_rows():
        row = lax.broadcasted_iota(jnp.int32, (V_ROWS - HEAD_W, tm), 0)
        return jnp.where(row == 0, 1.0, 0.0).astype(BF16)

    def segment(seg):
        acc = jnp.dot(h_scr[...], w_ref[...], preferred_element_type=F32)
        for c in range(N_HEADS):
            y = acc[:, c * HEAD_W:(c + 1) * HEAD_W]
            if seg == 0:
                qaT_ref[c] = norm_rope(y, c, DIFF_DIM, DIFF_DIM ** -0.5 * LOG2E).T.astype(BF16)
            elif seg == 1:
                ka_ref[:, c * HEAD_W:(c + 1) * HEAD_W] = norm_rope(y, c, DIFF_DIM, 1.0).astype(BF16)
            elif seg == 3:
                qbT_ref[c] = norm_rope(y, c, HEAD_W, HEAD_W ** -0.5 * LOG2E).T.astype(BF16)
            elif seg == 4:
                kb_ref[:, c * HEAD_W:(c + 1) * HEAD_W] = norm_rope(y, c, HEAD_W, 1.0).astype(BF16)
            elif seg == 2:
                vaT_ref[c] = y.T.astype(BF16)
            else:
                vbT_ref[c, 0:HEAD_W, :] = y.T.astype(BF16)
                vbT_ref[c, HEAD_W:V_ROWS, :] = aug_rows()

    for seg in range(N_SEG):
        pl.when(jnp.logical_and(real, j == seg))(functools.partial(segment, seg))

    @pl.when(jnp.logical_and(jnp.logical_not(real), j == 4))
    def _():
        kb_ref[...] = jnp.zeros(kb_ref.shape, BF16)

    @pl.when(jnp.logical_and(jnp.logical_not(real), j == 5))
    def _():
        vbT_ref[...] = jnp.zeros(vbT_ref.shape, BF16)


def _inproj(x, a_norm, w_in_p, gains, gmat, cos_t, sin_t):
    seq = x.shape[0]
    tm = ROW_TILE
    nrb = seq // tm
    pb = DIL_PAD // tm
    seq_p = seq + 2 * DIL_PAD

    def row(ip):
        return jnp.clip(ip - pb, 0, nrb - 1)

    kern = functools.partial(_inproj_kernel, n_row_blocks=nrb, pad_blocks=pb)
    out_shape = (
        jax.ShapeDtypeStruct((N_HEADS, HEAD_W, seq), BF16),
        jax.ShapeDtypeStruct((seq, SEG_W), BF16),
        jax.ShapeDtypeStruct((N_HEADS, HEAD_W, seq), BF16),
        jax.ShapeDtypeStruct((N_HEADS, HEAD_W, seq), BF16),
        jax.ShapeDtypeStruct((seq_p, SEG_W), BF16),
        jax.ShapeDtypeStruct((N_HEADS, V_ROWS, seq_p), BF16),
    )
    in_specs = [
        pl.BlockSpec((tm, D_MODEL), lambda ip, j: (row(ip), 0)),
        pl.BlockSpec((1, D_MODEL), lambda ip, j: (0, 0)),
        pl.BlockSpec((D_MODEL, SEG_W), lambda ip, j: (0, j)),
        pl.BlockSpec((None, 1, SEG_W), lambda ip, j: (j, 0, 0)),
        pl.BlockSpec((None, tm, HEAD_W), lambda ip, j: (j // 3, row(ip), 0)),
        pl.BlockSpec((None, tm, HEAD_W), lambda ip, j: (j // 3, row(ip), 0)),
        pl.BlockSpec((None, HEAD_W, HEAD_W), lambda ip, j: (j // 3, 0, 0)),
    ]
    out_specs = (
        pl.BlockSpec((N_HEADS, HEAD_W, tm), lambda ip, j: (0, 0, row(ip))),
        pl.BlockSpec((tm, SEG_W), lambda ip, j: (row(ip), 0)),
        pl.BlockSpec((N_HEADS, HEAD_W, tm), lambda ip, j: (0, 0, row(ip))),
        pl.BlockSpec((N_HEADS, HEAD_W, tm), lambda ip, j: (0, 0, row(ip))),
        pl.BlockSpec((tm, SEG_W), lambda ip, j: (ip, 0)),
        pl.BlockSpec((N_HEADS, V_ROWS, tm), lambda ip, j: (0, 0, ip)),
    )
    return pl.pallas_call(
        kern,
        out_shape=out_shape,
        grid=(nrb + 2 * pb, N_SEG),
        in_specs=in_specs,
        out_specs=out_specs,
        scratch_shapes=[pltpu.VMEM((tm, D_MODEL), BF16)],
        compiler_params=_cparams(("arbitrary", "arbitrary")),
        name="inproj",
    )(x, a_norm.reshape(1, D_MODEL).astype(F32), w_in_p, gains, cos_t, sin_t, gmat)


def _flash_step(s, m_old, acc_ref, vT_t):
    m_new = jnp.maximum(m_old, jnp.max(s, axis=0, keepdims=True))
    alpha = jnp.exp2(m_old - m_new)
    p = jnp.exp2((s - m_new).astype(BF16))
    acc_ref[...] = acc_ref[...] * alpha + jnp.dot(vT_t, p, preferred_element_type=F32)
    return m_new


def _flash_step_sum(s, m_old, l_old, acc_ref, vT_t):
    m_new = jnp.maximum(m_old, jnp.max(s, axis=0, keepdims=True))
    alpha = jnp.exp2(m_old - m_new)
    e = jnp.exp2(s - m_new)
    acc_ref[...] = acc_ref[...] * alpha + jnp.dot(vT_t, e.astype(BF16), preferred_element_type=F32)
    return m_new, l_old * alpha + jnp.sum(e, axis=0, keepdims=True)


def _head_out(aT, gain_row, out_scale):
    a = aT.T
    ms = jnp.mean(a * a, axis=-1, keepdims=True)
    y = a * lax.rsqrt(ms + EPS) * gain_row
    if out_scale != 1.0:
        y = y * out_scale
    return y


def _diff_attn_kernel(lq1_ref, lk1_ref, lq2_ref, lk2_ref, qT_ref, k_ref, vT_ref, og_ref, gmat_ref, o_ref,
                      acc0, acc1, lsum, k0_scr, k1_scr, kmax_scr, *, lam_init, tk):
    tq = qT_ref.shape[1]
    seq = k_ref.shape[0]
    lane0, lane1 = MAP1_LANE, MAP0_LANE

    @pl.when(pl.program_id(1) == 0)
    def _():
        ck = min(1024, seq)

        def kchunk(c, mx):
            rows = pl.ds(pl.multiple_of(c * ck, ck), ck)
            kc = k_ref[rows, :]
            kf = kc.astype(F32)
            n2 = jnp.dot((kf * kf).astype(BF16), gmat_ref[...], preferred_element_type=F32)
            lane = lax.broadcasted_iota(jnp.int32, kc.shape, 1)
            one = jnp.ones_like(kc)
            k0_scr[rows, :] = jnp.where(lane == lane0, one, kc)
            k1_scr[rows, :] = jnp.where(lane == lane1, one, kc)
            return jnp.maximum(mx, jnp.max(n2, axis=0, keepdims=True))

        kn2 = lax.fori_loop(0, seq // ck, kchunk, jnp.zeros((1, HEAD_W), F32))
        kmax_scr[...] = jnp.sqrt(kn2)

    qT = qT_ref[...]
    row = lax.broadcasted_iota(jnp.int32, qT.shape, 0)
    in_map1 = ((row // 32) % 2) == 1
    zero = jnp.zeros_like(qT)
    qf = qT.astype(F32)
    q2 = qf * qf
    nq0 = jnp.sum(jnp.where(in_map1, 0.0, q2), axis=0, keepdims=True)
    nq1 = jnp.sum(jnp.where(in_map1, q2, 0.0), axis=0, keepdims=True)
    kmax = kmax_scr[...]
    b0 = jnp.sqrt(nq0) * kmax[:, MAP0_LANE:MAP0_LANE + 1]
    b1 = jnp.sqrt(nq1) * kmax[:, MAP1_LANE:MAP1_LANE + 1]
    q0 = jnp.where(in_map1, zero, qT)
    q1 = jnp.where(in_map1, qT, zero)
    q0s = jnp.where(row == lane0, jnp.broadcast_to(-b0, qf.shape).astype(BF16), q0)
    q1s = jnp.where(row == lane1, jnp.broadcast_to(-b1, qf.shape).astype(BF16), q1)
    acc0[...] = jnp.zeros(acc0.shape, F32)
    acc1[...] = jnp.zeros(acc1.shape, F32)

    def body(kt, carry):
        l0, l1 = carry
        rows = pl.ds(pl.multiple_of(kt * tk, tk), tk)
        vT_t = vT_ref[:, rows]
        e0 = jnp.exp2(jnp.dot(k0_scr[rows, :], q0s, preferred_element_type=F32))
        acc0[...] += jnp.dot(vT_t, e0.astype(BF16), preferred_element_type=F32)
        e1 = jnp.exp2(jnp.dot(k1_scr[rows, :], q1s, preferred_element_type=F32))
        acc1[...] += jnp.dot(vT_t, e1.astype(BF16), preferred_element_type=F32)
        return l0 + jnp.sum(e0, axis=0, keepdims=True), l1 + jnp.sum(e1, axis=0, keepdims=True)

    l_init = jnp.zeros((1, tq), F32)
    l0, l1 = lax.fori_loop(0, seq // tk, body, (l_init, l_init))
    lsum[0:1, :] = l0
    lsum[1:2, :] = l1

    @pl.when(jnp.logical_not(jnp.min(jnp.minimum(l0, l1)) >= ROW_SUM_FLOOR))
    def _():
        acc0[...] = jnp.zeros(acc0.shape, F32)
        acc1[...] = jnp.zeros(acc1.shape, F32)

        def robust(kt, carry):
            m0, l0, m1, l1 = carry
            rows = pl.ds(pl.multiple_of(kt * tk, tk), tk)
            k_t = k_ref[rows, :]
            vT_t = vT_ref[:, rows]
            m0, l0 = _flash_step_sum(jnp.dot(k_t, q0, preferred_element_type=F32), m0, l0, acc0, vT_t)
            m1, l1 = _flash_step_sum(jnp.dot(k_t, q1, preferred_element_type=F32), m1, l1, acc1, vT_t)
            return m0, l0, m1, l1

        m_init = jnp.full((1, tq), NEG, F32)
        _, l0, _, l1 = lax.fori_loop(0, seq // tk, robust, (m_init, l_init, m_init, l_init))
        lsum[0:1, :] = l0
        lsum[1:2, :] = l1

    lam = (jnp.exp(jnp.sum(lq1_ref[...] * lk1_ref[...], axis=-1, keepdims=True))
           - jnp.exp(jnp.sum(lq2_ref[...] * lk2_ref[...], axis=-1, keepdims=True)) + lam_init)
    o0 = acc0[...] * (1.0 / lsum[0:1, :])
    o1 = acc1[...] * (1.0 / lsum[1:2, :])
    aT = o0 - lam * o1
    o_ref[...] = _head_out(aT, og_ref[...], 1.0 - lam_init).astype(o_ref.dtype)


def _diff_attn(qaT, ka, vaT, lq1, lk1, lq2, lk2, og, gmat_diff, lam_init):
    seq = ka.shape[0]
    tq = min(ATT_TQ, seq)
    tk = min(ATT_TK, seq)
    kern = functools.partial(_diff_attn_kernel, lam_init=lam_init, tk=tk)
    vec = lambda v: v.reshape(1, -1).astype(F32)
    small = pl.BlockSpec((1, DIFF_DIM), lambda h, qi: (0, 0))
    return pl.pallas_call(
        kern,
        out_shape=jax.ShapeDtypeStruct((seq, SEG_W), BF16),
        grid=(N_HEADS, seq // tq),
        in_specs=[small, small, small, small,
                  pl.BlockSpec((None, HEAD_W, tq), lambda h, qi: (h, 0, qi)),
                  pl.BlockSpec((seq, HEAD_W), lambda h, qi: (0, h)),
                  pl.BlockSpec((None, HEAD_W, seq), lambda h, qi: (h, 0, 0)),
                  pl.BlockSpec((1, HEAD_W), lambda h, qi: (0, 0)),
                  pl.BlockSpec((HEAD_W, HEAD_W), lambda h, qi: (0, 0))],
        out_specs=pl.BlockSpec((tq, HEAD_W), lambda h, qi: (qi, h)),
        scratch_shapes=[pltpu.VMEM((HEAD_W, tq), F32), pltpu.VMEM((HEAD_W, tq), F32),
                        pltpu.VMEM((8, tq), F32),
                        pltpu.VMEM((seq, HEAD_W), BF16), pltpu.VMEM((seq, HEAD_W), BF16),
                        pltpu.VMEM((1, HEAD_W), F32)],
        compiler_params=_cparams(("arbitrary", "arbitrary")),
        name="diff_attn",
    )(vec(lq1), vec(lk1), vec(lq2), vec(lk2), qaT, ka, vaT, vec(og), gmat_diff)


def _dil_chunks(tq):
    chunks, off = [], 0
    for _, dil in DIL_BRANCHES:
        pad = -(-(N_SIDE * dil) // 128) * 128
        total = tq + 2 * pad
        c0 = 0
        while c0 < total:
            nk = min(DIL_CHUNK, total - c0)
            chunks.append((dil, c0 - pad, nk, off))
            off += nk
            c0 += nk
    return chunks, off


def _dil_bias(tq):
    chunks, total = _dil_chunks(tq)
    bias = np.full((total, tq), NEG, np.float32)
    col = np.arange(tq)[None, :]
    for dil, rel, nk, off in chunks:
        delta = rel + np.arange(nk)[:, None] - col
        ok = (np.abs(delta) <= N_SIDE * dil) & (delta % dil == 0)
        bias[off:off + nk][ok] = 0.0
    return jnp.asarray(bias)


def _dil_attn_kernel(qT_ref, k_ref, vT_ref, bias_ref, og_ref, gmat_ref, o_ref, acc, kmax_scr, *, seq, chunks):
    tq = acc.shape[2]
    n_sub = acc.shape[0]
    base = pl.program_id(1) * (tq * n_sub)

    @pl.when(pl.program_id(1) == 0)
    def _():
        ck = 1024

        def kchunk(c, mx):
            kf = k_ref[pl.ds(pl.multiple_of(c * ck, ck), ck), :].astype(F32)
            n2 = jnp.dot((kf * kf).astype(BF16), gmat_ref[...], preferred_element_type=F32)
            return jnp.maximum(mx, jnp.max(n2, axis=0, keepdims=True))

        kn2 = lax.fori_loop(0, k_ref.shape[0] // ck, kchunk, jnp.zeros((1, HEAD_W), F32))
        kmax_scr[...] = jnp.sqrt(kn2)

    def windows(i0):
        for dil, rel, nk, off in chunks:
            start = pl.multiple_of(i0 + (DIL_PAD + rel), 128)
            yield rel, nk, off, k_ref[pl.ds(start, nk), :], vT_ref[:, pl.ds(start, nk)]

    lmin = None
    for t in range(n_sub):
        qT = qT_ref[:, t * tq:(t + 1) * tq]
        qf = qT.astype(F32)
        shift = jnp.sqrt(jnp.sum(qf * qf, axis=0, keepdims=True)) * kmax_scr[:, 0:1]
        num = jnp.zeros((V_ROWS, tq), F32)
        for rel, nk, off, k_t, vT_t in windows(base + t * tq):
            s = jnp.dot(k_t, qT, preferred_element_type=F32) - shift + bias_ref[off:off + nk, :]
            num = num + jnp.dot(vT_t, jnp.exp2(s).astype(BF16), preferred_element_type=F32)
        acc[t] = num
        l_t = jnp.min(num[HEAD_W:HEAD_W + 1, :])
        lmin = l_t if lmin is None else jnp.minimum(lmin, l_t)

    @pl.when(jnp.logical_not(lmin >= ROW_SUM_FLOOR))
    def _():
        for t in range(n_sub):
            i0 = base + t * tq
            qT = qT_ref[:, t * tq:(t + 1) * tq]
            acc[t] = jnp.zeros((V_ROWS, tq), F32)
            m = jnp.full((1, tq), NEG, F32)
            for rel, nk, off, k_t, vT_t in windows(i0):
                s = jnp.dot(k_t, qT, preferred_element_type=F32)
                kpos = lax.broadcasted_iota(jnp.int32, (nk, tq), 0) + (i0 + rel)
                valid = jnp.logical_and(kpos >= 0, kpos < seq)
                s = jnp.where(valid, s + bias_ref[off:off + nk, :], NEG)
                m = _flash_step(s, m, acc.at[t], vT_t)

    for t in range(n_sub):
        oT = acc[t, 0:HEAD_W, :] * (1.0 / acc[t, HEAD_W:HEAD_W + 1, :])
        o_ref[t * tq:(t + 1) * tq, :] = _head_out(oT, og_ref[...], 1.0).astype(o_ref.dtype)


def _dil_attn(qbT, kb, vbT, og, gmat_ones):
    seq = qbT.shape[2]
    seq_p = kb.shape[0]
    tq = min(DIL_TQ, seq)
    n_sub = DIL_SUBTILES
    chunks, _ = _dil_chunks(tq)
    bias = _dil_bias(tq)
    kern = functools.partial(_dil_attn_kernel, seq=seq, chunks=chunks)
    return pl.pallas_call(
        kern,
        out_shape=jax.ShapeDtypeStruct((seq, SEG_W), BF16),
        grid=(N_HEADS, seq // (tq * n_sub)),
        in_specs=[pl.BlockSpec((None, HEAD_W, tq * n_sub), lambda h, qi: (h, 0, qi)),
                  pl.BlockSpec((seq_p, HEAD_W), lambda h, qi: (0, h)),
                  pl.BlockSpec((None, V_ROWS, seq_p), lambda h, qi: (h, 0, 0)),
                  pl.BlockSpec(bias.shape, lambda h, qi: (0, 0)),
                  pl.BlockSpec((1, HEAD_W), lambda h, qi: (0, 0)),
                  pl.BlockSpec((HEAD_W, HEAD_W), lambda h, qi: (0, 0))],
        out_specs=pl.BlockSpec((tq * n_sub, HEAD_W), lambda h, qi: (qi, h)),
        scratch_shapes=[pltpu.VMEM((n_sub, V_ROWS, tq), F32), pltpu.VMEM((1, HEAD_W), F32)],
        compiler_params=_cparams(("arbitrary", "arbitrary")),
        name="dil_attn",
    )(qbT, kb, vbT, bias, og.reshape(1, HEAD_W).astype(F32), gmat_ones)


def _outproj_kernel(*refs, with_router):
    if with_router:
        a_ref, b_ref, w_ref, x_ref, g_ref, r_hi_ref, r_lo_ref, x1_ref, h_ref, route_ref = refs
    else:
        a_ref, b_ref, w_ref, x_ref, g_ref, x1_ref, h_ref = refs
    acc = (jnp.dot(a_ref[...], w_ref[0:SEG_W, :], preferred_element_type=F32)
           + jnp.dot(b_ref[...], w_ref[SEG_W:2 * SEG_W, :], preferred_element_type=F32))
    x1 = x_ref[...] + acc
    x1_ref[...] = x1
    ms = jnp.mean(x1 * x1, axis=-1, keepdims=True)
    hn = x1 * lax.rsqrt(ms + EPS) * g_ref[...]
    if not with_router:
        h_ref[...] = hn.astype(BF16)
    else:
        tm = hn.shape[0]
        for s in range(ROW_CHUNKS):
            h_ref[pl.ds(s, tm, stride=ROW_CHUNKS), :] = hn[:, s * HEAD_W:(s + 1) * HEAD_W]
        h_hi = hn.astype(BF16)
        h_lo = (hn - h_hi.astype(F32)).astype(BF16)
        logits = (jnp.dot(h_hi, r_hi_ref[...], preferred_element_type=F32)
                  + jnp.dot(h_hi, r_lo_ref[...], preferred_element_type=F32)
                  + jnp.dot(h_lo, r_hi_ref[...], preferred_element_type=F32))
        lane = lax.broadcasted_iota(jnp.int32, logits.shape, 1)
        ninf = jnp.float32(-jnp.inf)
        lg = jnp.where(lane < N_EXPERTS, logits, ninf)
        v1 = jnp.max(lg, axis=-1, keepdims=True)
        i1 = jnp.min(jnp.where(lg == v1, lane, HEAD_W), axis=-1, keepdims=True)
        lg2 = jnp.where(lane == i1, ninf, lg)
        v2 = jnp.max(lg2, axis=-1, keepdims=True)
        i2 = jnp.min(jnp.where(lg2 == v2, lane, HEAD_W), axis=-1, keepdims=True)
        g1 = 1.0 / (1.0 + jnp.exp(v2 - v1))
        g2 = 1.0 - g1
        route_ref[...] = jnp.where(lane == 0, i1.astype(F32),
                                   jnp.where(lane == 1, i2.astype(F32),
                                             jnp.where(lane == 2, g1, jnp.where(lane == 3, g2, 0.0))))


def _outproj(out_a, out_b, w_out, x, f_norm, router=None):
    seq = x.shape[0]
    tm = ROW_TILE
    with_router = router is not None
    in_specs = [pl.BlockSpec((tm, SEG_W), lambda i: (i, 0)),
                pl.BlockSpec((tm, SEG_W), lambda i: (i, 0)),
                pl.BlockSpec((D_MODEL, D_MODEL), lambda i: (0, 0)),
                pl.BlockSpec((tm, D_MODEL), lambda i: (i, 0)),
                pl.BlockSpec((1, D_MODEL), lambda i: (0, 0))]
    args = [out_a, out_b, w_out.astype(BF16), x, f_norm.reshape(1, D_MODEL).astype(F32)]
    out_shape = [jax.ShapeDtypeStruct((seq, D_MODEL), F32)]
    out_specs = [pl.BlockSpec((tm, D_MODEL), lambda i: (i, 0))]
    if not with_router:
        out_shape.append(jax.ShapeDtypeStruct((seq, D_MODEL), BF16))
        out_specs.append(pl.BlockSpec((tm, D_MODEL), lambda i: (i, 0)))
    else:
        out_shape.append(jax.ShapeDtypeStruct((seq * ROW_CHUNKS, HEAD_W), F32))
        out_specs.append(pl.BlockSpec((tm * ROW_CHUNKS, HEAD_W), lambda i: (i, 0)))
        r = jnp.zeros((D_MODEL, HEAD_W), F32).at[:, :N_EXPERTS].set(router.astype(F32))
        r_hi = r.astype(BF16)
        r_lo = (r - r_hi.astype(F32)).astype(BF16)
        in_specs += [pl.BlockSpec((D_MODEL, HEAD_W), lambda i: (0, 0))] * 2
        args += [r_hi, r_lo]
        out_shape.append(jax.ShapeDtypeStruct((seq, HEAD_W), F32))
        out_specs.append(pl.BlockSpec((tm, HEAD_W), lambda i: (i, 0)))
    return pl.pallas_call(
        functools.partial(_outproj_kernel, with_router=with_router),
        out_shape=tuple(out_shape),
        grid=(seq // tm,),
        in_specs=in_specs,
        out_specs=tuple(out_specs),
        compiler_params=_cparams(("arbitrary",)),
        name="outproj",
    )(*args)


def _swiglu_part(h, wg, wu, wd):
    g = jnp.dot(h, wg, preferred_element_type=F32)
    u = jnp.dot(h, wu, preferred_element_type=F32)
    act = (g * (1.0 / (1.0 + jnp.exp(-g))) * u).astype(BF16)
    return jnp.dot(act, wd, preferred_element_type=F32)


def _dense_ffn_kernel(h_ref, x_ref, wg_ref, wu_ref, wd_ref, o_ref):
    @pl.when(pl.program_id(1) == 0)
    def _():
        o_ref[...] = x_ref[...]

    o_ref[...] += _swiglu_part(h_ref[...], wg_ref[...], wu_ref[...], wd_ref[...])


def _dense_ffn(h, x1, wg, wu, wd):
    seq = x1.shape[0]
    tm, tf = ROW_TILE, FF_TILE
    return pl.pallas_call(
        _dense_ffn_kernel,
        out_shape=jax.ShapeDtypeStruct((seq, D_MODEL), F32),
        grid=(seq // tm, D_FF // tf),
        in_specs=[pl.BlockSpec((tm, D_MODEL), lambda i, j: (i, 0)),
                  pl.BlockSpec((tm, D_MODEL), lambda i, j: (i, 0)),
                  pl.BlockSpec((D_MODEL, tf), lambda i, j: (0, j)),
                  pl.BlockSpec((D_MODEL, tf), lambda i, j: (0, j)),
                  pl.BlockSpec((tf, D_MODEL), lambda i, j: (j, 0))],
        out_specs=pl.BlockSpec((tm, D_MODEL), lambda i, j: (i, 0)),
        compiler_params=_cparams(("arbitrary", "arbitrary")),
        name="dense_ffn",
    )(h, x1, wg.astype(BF16), wu.astype(BF16), wd.astype(BF16))


def _moe_ffn_kernel(be_ref, nused_ref, src_ref, src_next_ref, dst_ref, dst_prev_ref,
                    h3_ref, wg_ref, wu_ref, wd_ref, y_ref, x3, xb, acc, o3, gsem, ssem):
    b = pl.program_id(0)
    j = pl.program_id(1)
    nb = pl.num_programs(0)
    nj = pl.num_programs(1)
    tm = xb.shape[0]
    n_used = nused_ref[0]
    used = b < n_used

    def slab(ref, row0):
        return ref.at[pl.ds(pl.multiple_of(row0, 8), ROW_CHUNKS), :]

    def start_gather(idx_ref, slot):
        def start(r, c):
            pltpu.make_async_copy(slab(h3_ref, idx_ref[0, r]), slab(x3.at[slot], r * SLAB_PITCH),
                                  gsem.at[slot]).start()
            return c

        lax.fori_loop(0, tm, start, 0)

    def scatter_copy(idx_ref, r):
        return pltpu.make_async_copy(slab(o3, r * SLAB_PITCH), slab(y_ref, idx_ref[0, r]), ssem)

    def wait_gather(slot):
        n_rows = tm * ROW_CHUNKS
        pltpu.make_async_copy(h3_ref.at[pl.ds(0, n_rows), :], x3.at[slot, pl.ds(0, n_rows), :],
                              gsem.at[slot]).wait()

    def wait_scatter():
        n_rows = tm * ROW_CHUNKS
        pltpu.make_async_copy(o3.at[pl.ds(0, n_rows), :], y_ref.at[pl.ds(0, n_rows), :], ssem).wait()

    def flush_scatter(idx_ref):
        def start(r, c):
            scatter_copy(idx_ref, r).start()
            return c

        lax.fori_loop(0, tm, start, 0)
        wait_scatter()

    @pl.when(jnp.logical_and(j == 0, jnp.logical_and(b == 0, used)))
    def _():
        start_gather(src_ref, 0)
        o3[...] = jnp.zeros(o3.shape, F32)

    @pl.when(jnp.logical_and(j == 0, jnp.logical_and(b >= 1, b == n_used)))
    def _():
        flush_scatter(dst_prev_ref)

    @pl.when(jnp.logical_and(j == 0, jnp.where(b == 0, used, b - 1 < n_used)))
    def _():
        wait_gather(b % 2)

    @pl.when(jnp.logical_and(j == 0, used))
    def _():
        slot = b % 2
        for s in range(ROW_CHUNKS):
            xb[:, s * HEAD_W:(s + 1) * HEAD_W] = x3[slot, pl.ds(s, tm, stride=SLAB_PITCH), :].astype(BF16)
        acc[...] = jnp.zeros(acc.shape, F32)

    @pl.when(used)
    def _():
        per_step = tm // D_EXPERT_STEPS
        nslot = (b + 1) % 2

        def move(r):
            pltpu.make_async_copy(slab(h3_ref, src_next_ref[0, r]), slab(x3.at[nslot], r * SLAB_PITCH),
                                  gsem.at[nslot]).start()
            scatter_copy(dst_prev_ref, r).start()

        for i in range(per_step):
            move(j * per_step + i)

        @pl.when(j == 0)
        def _():
            for r in range(per_step * D_EXPERT_STEPS, tm):
                move(r)

        acc[...] += _swiglu_part(xb[...], wg_ref[...], wu_ref[...], wd_ref[...])

    @pl.when(jnp.logical_and(j == nj - 1, used))
    def _():
        wait_scatter()
        for s in range(ROW_CHUNKS):
            o3[pl.ds(s, tm, stride=SLAB_PITCH), :] = acc[:, s * HEAD_W:(s + 1) * HEAD_W]

        @pl.when(b == nb - 1)
        def _():
            flush_scatter(dst_ref)
            wait_gather((b + 1) % 2)


def _moe_ffn(h3, src_idx, dst_idx, dst_prev_idx, blk_expert, n_used, wg, wu, wd):
    seq = h3.shape[0] // ROW_CHUNKS
    n_blocks = src_idx.shape[0]
    tm, tf = MOE_TILE, MOE_FF_TILE
    nj = D_EXPERT // tf

    def jj(b, j, be, nu):
        return jnp.where(b < nu[0], j, nj - 1)

    def idx_spec(shift):
        return pl.BlockSpec((None, 1, tm), lambda b, j, be, nu: (jnp.clip(b + shift, 0, n_blocks - 1), 0, 0),
                            memory_space=pltpu.SMEM)

    return pl.pallas_call(
        _moe_ffn_kernel,
        out_shape=jax.ShapeDtypeStruct(((2 * seq + tm) * ROW_CHUNKS, HEAD_W), F32),
        grid_spec=pltpu.PrefetchScalarGridSpec(
            num_scalar_prefetch=2,
            grid=(n_blocks, nj),
            in_specs=[idx_spec(0), idx_spec(1), idx_spec(0), idx_spec(0),
                      pl.BlockSpec(memory_space=pl.ANY),
                      pl.BlockSpec((None, D_MODEL, tf), lambda b, j, be, nu: (be[b], 0, jj(b, j, be, nu))),
                      pl.BlockSpec((None, D_MODEL, tf), lambda b, j, be, nu: (be[b], 0, jj(b, j, be, nu))),
                      pl.BlockSpec((None, tf, D_MODEL), lambda b, j, be, nu: (be[b], jj(b, j, be, nu), 0))],
            out_specs=pl.BlockSpec(memory_space=pl.ANY),
            scratch_shapes=[pltpu.VMEM((2, tm * SLAB_PITCH, HEAD_W), F32),
                            pltpu.VMEM((tm, D_MODEL), BF16),
                            pltpu.VMEM((tm, D_MODEL), F32),
                            pltpu.VMEM((tm * SLAB_PITCH, HEAD_W), F32),
                            pltpu.SemaphoreType.DMA((2,)),
                            pltpu.SemaphoreType.DMA(())]),
        compiler_params=_cparams(("arbitrary", "arbitrary")),
        name="moe_ffn",
    )(blk_expert, n_used, src_idx, src_idx, dst_idx, dst_prev_idx, h3, wg, wu, wd)


def _combine_kernel(x_ref, ya_ref, yb_ref, route_ref, o_ref):
    r = route_ref[...]
    g1 = jnp.broadcast_to(r[:, 2:3], (r.shape[0], HEAD_W))
    g2 = jnp.broadcast_to(r[:, 3:4], (r.shape[0], HEAD_W))
    tm = r.shape[0]
    for s in range(ROW_CHUNKS):
        cols = slice(s * HEAD_W, (s + 1) * HEAD_W)
        rows = pl.ds(s, tm, stride=ROW_CHUNKS)
        o_ref[:, cols] = x_ref[:, cols] + g1 * ya_ref[rows, :] + g2 * yb_ref[rows, :]


def _moe_combine(x1, y, route):
    seq = x1.shape[0]
    tm = ROW_TILE
    nb = seq // tm
    big = pl.BlockSpec((tm, D_MODEL), lambda i: (i, 0))
    return pl.pallas_call(
        _combine_kernel,
        out_shape=jax.ShapeDtypeStruct((seq, D_MODEL), F32),
        grid=(nb,),
        in_specs=[big,
                  pl.BlockSpec((tm * ROW_CHUNKS, HEAD_W), lambda i: (i, 0)),
                  pl.BlockSpec((tm * ROW_CHUNKS, HEAD_W), lambda i: (i + nb, 0)),
                  pl.BlockSpec((tm, HEAD_W), lambda i: (i, 0))],
        out_specs=big,
        compiler_params=_cparams(("arbitrary",)),
        name="moe_combine",
    )(x1, y, y, route)


def _moe_plan(route, seq):
    tm = MOE_TILE
    n_assign = 2 * seq
    n_blocks = n_assign // tm + N_EXPERTS
    e = route[:, 0:2].astype(jnp.int32).reshape(-1)
    onehot = (e[:, None] == jnp.arange(N_EXPERTS, dtype=jnp.int32)[None, :]).astype(jnp.int32)
    csum = jnp.cumsum(onehot, axis=0)
    counts = csum[-1]
    rank = jnp.sum((csum - onehot) * onehot, axis=1)
    nblk = (counts + tm - 1) // tm
    cum_blk = jnp.cumsum(nblk)
    pad_start = (cum_blk - nblk) * tm
    pos = pad_start[e] + rank
    assign = jnp.arange(n_assign, dtype=jnp.int32)
    slot_assign = jnp.full((n_blocks * tm,), -1, jnp.int32).at[pos].set(assign)
    real = slot_assign >= 0
    tok = jnp.maximum(slot_assign, 0) // 2
    in_block = jnp.arange(n_blocks * tm, dtype=jnp.int32) % tm
    src_idx = tok * ROW_CHUNKS
    dst_idx = jnp.where(real, (slot_assign % 2) * seq + tok, n_assign + in_block) * ROW_CHUNKS
    src_idx = src_idx.reshape(n_blocks, 1, tm)
    dst_idx = dst_idx.reshape(n_blocks, 1, tm)
    scratch_blk = ((n_assign + jnp.arange(tm, dtype=jnp.int32)) * ROW_CHUNKS).reshape(1, 1, tm)
    dst_prev_idx = jnp.concatenate([scratch_blk, dst_idx[:-1]], axis=0)
    n_used = cum_blk[-1].astype(jnp.int32)
    blk = jnp.arange(n_blocks, dtype=jnp.int32)
    blk_expert = jnp.searchsorted(cum_blk, jnp.minimum(blk, n_used - 1), side='right').astype(jnp.int32)
    blk_expert = jnp.minimum(blk_expert, N_EXPERTS - 1)
    return src_idx, dst_idx, dst_prev_idx, blk_expert, n_used.reshape(1)


def _moe(h3, x1, route, wg, wu, wd):
    seq = x1.shape[0]
    src_idx, dst_idx, dst_prev_idx, blk_expert, n_used = _moe_plan(route, seq)
    y = _moe_ffn(h3, src_idx, dst_idx, dst_prev_idx, blk_expert, n_used,
                 wg.astype(BF16), wu.astype(BF16), wd.astype(BF16))
    return _moe_combine(x1, y, route)


def _layer(x, i, a_norm, w_in, dqn, dkn, lq1, lk1, lq2, lk2, don, bqn, bkn, bon, w_out, f_norm, ffn):
    seq = x.shape[0]
    lam_init = 0.8 - 0.6 * math.exp(-0.3 * i)
    gains, gmat, cos_t, sin_t = _segment_tables(seq, dqn, dkn, bqn, bkn)
    qaT, ka, vaT, qbT, kb, vbT = _inproj(x, a_norm, _permute_w_in(w_in), gains, gmat, cos_t, sin_t)
    out_a = _diff_attn(qaT, ka, vaT, lq1, lk1, lq2, lk2, don, gmat[0], lam_init)
    out_b = _dil_attn(qbT, kb, vbT, bon, gmat[1])
    if len(ffn) == 3:
        x1, h = _outproj(out_a, out_b, w_out, x, f_norm)
        return _dense_ffn(h, x1, *ffn)
    router, wg, wu, wd = ffn
    x1, h3, route = _outproj(out_a, out_b, w_out, x, f_norm, router)
    return _moe(h3, x1, route, wg, wu, wd)


def kernel(x, attn_norm_0, w_in_0, diff_q_norm_0, diff_k_norm_0, diff_lam_q1_0, diff_lam_k1_0, diff_lam_q2_0, diff_lam_k2_0, diff_out_norm_0, dil_q_norm_0, dil_k_norm_0, dil_out_norm_0, w_out_0, ffn_norm_0, ffn_w_gate_0, ffn_w_up_0, ffn_w_down_0, attn_norm_1, w_in_1, diff_q_norm_1, diff_k_norm_1, diff_lam_q1_1, diff_lam_k1_1, diff_lam_q2_1, diff_lam_k2_1, diff_out_norm_1, dil_q_norm_1, dil_k_norm_1, dil_out_norm_1, w_out_1, ffn_norm_1, router_1, moe_w_gate_1, moe_w_up_1, moe_w_down_1):
    b, seq, d = x.shape
    assert b == 1 and d == D_MODEL and seq % ROW_TILE == 0
    xs = x.reshape(seq, d)
    xs = _layer(xs, 0, attn_norm_0, w_in_0, diff_q_norm_0, diff_k_norm_0, diff_lam_q1_0, diff_lam_k1_0,
                diff_lam_q2_0, diff_lam_k2_0, diff_out_norm_0, dil_q_norm_0, dil_k_norm_0, dil_out_norm_0,
                w_out_0, ffn_norm_0, (ffn_w_gate_0, ffn_w_up_0, ffn_w_down_0))
    xs = _layer(xs, 1, attn_norm_1, w_in_1, diff_q_norm_1, diff_k_norm_1, diff_lam_q1_1, diff_lam_k1_1,
                diff_lam_q2_1, diff_lam_k2_1, diff_out_norm_1, dil_q_norm_1, dil_k_norm_1, dil_out_norm_1,
                w_out_1, ffn_norm_1, (router_1, moe_w_gate_1, moe_w_up_1, moe_w_down_1))
    return xs.reshape(b, seq, d)
```

```python
import functools
import math

import numpy as np
import jax
import jax.numpy as jnp
from jax import lax
from jax.experimental import pallas as pl
from jax.experimental.pallas import tpu as pltpu

F32 = jnp.float32
BF16 = jnp.bfloat16

D_MODEL = 2048
N_HEADS = 8
HEAD_W = 128
DIFF_DIM = 64
SEG_W = N_HEADS * HEAD_W
N_SEG = 6
ROW_CHUNKS = D_MODEL // HEAD_W
SLAB_PITCH = 24
DIL_BRANCHES = ((128, 1), (512, 4), (2048, 16))
N_SIDE = 64
D_FF = 5632
N_EXPERTS = 8
D_EXPERT = 7168
EPS = 1e-6
NEG = -1e30
LOG2E = 1.4426950408889634

MAP0_LANE = 0
MAP1_LANE = 32
ROW_SUM_FLOOR = 2.0 ** -80

V_ROWS = 144
DIL_PAD = 1024
VMEM_LIMIT = 56 * 1024 * 1024

ROW_TILE = 512
ATT_TQ = 512
ATT_TK = 8192
DIL_TQ = 256
DIL_SUBTILES = 4
DIL_CHUNK = 768
FF_TILE = 512
D_EXPERT_STEPS = 7
MOE_FF_TILE = D_EXPERT // D_EXPERT_STEPS
MOE_TILE = 512


def _cparams(sem):
    return pltpu.CompilerParams(dimension_semantics=sem, vmem_limit_bytes=VMEM_LIMIT)


def _diff_lane_perm():
    perm = np.zeros(HEAD_W, np.int32)
    for m in range(2):
        for t in range(DIFF_DIM):
            p = (t // 32) * 64 + m * 32 + (t % 32)
            perm[p] = m * DIFF_DIM + t
    return perm


def _segment_tables(seq, dqn, dkn, bqn, bkn):
    perm = _diff_lane_perm()
    t_of_lane = perm % DIFF_DIM
    ones = jnp.ones((SEG_W,), F32)
    g_qa = jnp.tile(dqn.astype(F32)[t_of_lane], N_HEADS)
    g_ka = jnp.tile(dkn.astype(F32)[t_of_lane], N_HEADS)
    g_qb = jnp.tile(bqn.astype(F32), N_HEADS)
    g_kb = jnp.tile(bkn.astype(F32), N_HEADS)
    gains = jnp.stack([g_qa, g_ka, ones, g_qb, g_kb, ones]).reshape(N_SEG, 1, SEG_W)

    lane = np.arange(HEAD_W)
    map_of_lane = (lane // 32) % 2
    g_diff = (map_of_lane[:, None] == map_of_lane[None, :]).astype(np.float32)
    g_dil = np.ones((HEAD_W, HEAD_W), np.float32)
    pair = np.eye(2, dtype=np.float32)
    gmat = jnp.asarray(np.stack([np.kron(pair, g_diff), np.kron(pair, g_dil)]), BF16)

    pos = jnp.arange(seq, dtype=F32)

    def ang(dim):
        inv = 10000.0 ** (-jnp.arange(0, dim, 2, dtype=F32) / dim)
        return pos[:, None] * inv[None, :]

    a32 = ang(DIFF_DIM)
    a64 = ang(HEAD_W)
    cos_a = jnp.tile(jnp.cos(a32), (1, 4))
    sin_a = jnp.tile(jnp.sin(a32), (1, 4))
    cos_b = jnp.tile(jnp.cos(a64), (1, 2))
    sin_b = jnp.tile(jnp.sin(a64), (1, 2))
    sign = jnp.where(jnp.arange(HEAD_W) < 64, -1.0, 1.0).astype(F32)[None, :]
    cos_t = jnp.stack([cos_a, cos_b])
    sin_t = jnp.stack([sin_a * sign, sin_b * sign])
    return gains, gmat, cos_t, sin_t


def _permute_w_in(w_in):
    perm = _diff_lane_perm()
    cols = np.arange(N_SEG * SEG_W)
    for seg in (0, 1):
        for h in range(N_HEADS):
            base = seg * SEG_W + h * HEAD_W
            cols[base:base + HEAD_W] = base + perm
    return w_in[:, cols].astype(BF16)


def _inproj_kernel(x_ref, g_ref, w_ref, gain_ref, cos_ref, sin_ref, gmat_ref,
                   qaT_ref, ka_ref, vaT_ref, qbT_ref, kb_ref, vbT_ref, h_scr,
                   *, n_row_blocks, pad_blocks):
    ip = pl.program_id(0)
    j = pl.program_id(1)
    real = jnp.logical_and(ip >= pad_blocks, ip < pad_blocks + n_row_blocks)
    tm = x_ref.shape[0]

    @pl.when(jnp.logical_and(real, j == 0))
    def _():
        x = x_ref[...]
        ms = jnp.mean(x * x, axis=-1, keepdims=True)
        h_scr[...] = (x * lax.rsqrt(ms + EPS) * g_ref[...]).astype(BF16)

    def norm_rope(y, ss, c, n_group, scale):
        yn = y * lax.rsqrt(ss * (1.0 / n_group) + EPS) * gain_ref[:, c * HEAD_W:(c + 1) * HEAD_W]
        out = yn * cos_ref[...] + pltpu.roll(yn, 64, 1) * sin_ref[...]
        if scale != 1.0:
            out = out * scale
        return out

    def aug_rows():
        row = lax.broadcasted_iota(jnp.int32, (V_ROWS - HEAD_W, tm), 0)
        return jnp.where(row == 0, 1.0, 0.0).astype(BF16)

    def segment(seg):
        acc = jnp.dot(h_scr[...], w_ref[...], preferred_element_type=F32)
        for c in range(N_HEADS):
            y = acc[:, c * HEAD_W:(c + 1) * HEAD_W]
            if seg in (0, 1, 3, 4) and c % 2 == 0:
                pair = acc[:, c * HEAD_W:(c + 2) * HEAD_W]
                ss2 = jnp.dot((pair * pair).astype(BF16), gmat_ref[...], preferred_element_type=F32)
            if seg in (0, 1, 3, 4):
                ss = ss2[:, (c % 2) * HEAD_W:(c % 2 + 1) * HEAD_W]
            if seg == 0:
                qaT_ref[c] = norm_rope(y, ss, c, DIFF_DIM, DIFF_DIM ** -0.5 * LOG2E).T.astype(BF16)
            elif seg == 1:
                ka_ref[:, c * HEAD_W:(c + 1) * HEAD_W] = norm_rope(y, ss, c, DIFF_DIM, 1.0).astype(BF16)
            elif seg == 3:
                qbT_ref[c] = norm_rope(y, ss, c, HEAD_W, HEAD_W ** -0.5 * LOG2E).T.astype(BF16)
            elif seg == 4:
                kb_ref[:, c * HEAD_W:(c + 1) * HEAD_W] = norm_rope(y, ss, c, HEAD_W, 1.0).astype(BF16)
            elif seg == 2:
                vaT_ref[c] = y.T.astype(BF16)
            else:
                vbT_ref[c, 0:HEAD_W, :] = y.T.astype(BF16)
                vbT_ref[c, HEAD_W:V_ROWS, :] = aug_rows()

    for seg in range(N_SEG):
        pl.when(jnp.logical_and(real, j == seg))(functools.partial(segment, seg))

    @pl.when(jnp.logical_and(jnp.logical_not(real), j == 4))
    def _():
        kb_ref[...] = jnp.zeros(kb_ref.shape, BF16)

    @pl.when(jnp.logical_and(jnp.logical_not(real), j == 5))
    def _():
        vbT_ref[...] = jnp.zeros(vbT_ref.shape, BF16)


def _inproj(x, a_norm, w_in_p, gains, gmat, cos_t, sin_t):
    seq = x.shape[0]
    tm = ROW_TILE
    nrb = seq // tm
    pb = DIL_PAD // tm
    seq_p = seq + 2 * DIL_PAD

    def row(ip):
        return jnp.clip(ip - pb, 0, nrb - 1)

    kern = functools.partial(_inproj_kernel, n_row_blocks=nrb, pad_blocks=pb)
    out_shape = (
        jax.ShapeDtypeStruct((N_HEADS, HEAD_W, seq), BF16),
        jax.ShapeDtypeStruct((seq, SEG_W), BF16),
        jax.ShapeDtypeStruct((N_HEADS, HEAD_W, seq), BF16),
        jax.ShapeDtypeStruct((N_HEADS, HEAD_W, seq), BF16),
        jax.ShapeDtypeStruct((seq_p, SEG_W), BF16),
        jax.ShapeDtypeStruct((N_HEADS, V_ROWS, seq_p), BF16),
    )
    in_specs = [
        pl.BlockSpec((tm, D_MODEL), lambda ip, j: (row(ip), 0)),
        pl.BlockSpec((1, D_MODEL), lambda ip, j: (0, 0)),
        pl.BlockSpec((D_MODEL, SEG_W), lambda ip, j: (0, j)),
        pl.BlockSpec((None, 1, SEG_W), lambda ip, j: (j, 0, 0)),
        pl.BlockSpec((None, tm, HEAD_W), lambda ip, j: (j // 3, row(ip), 0)),
        pl.BlockSpec((None, tm, HEAD_W), lambda ip, j: (j // 3, row(ip), 0)),
        pl.BlockSpec((None, 2 * HEAD_W, 2 * HEAD_W), lambda ip, j: (j // 3, 0, 0)),
    ]
    out_specs = (
        pl.BlockSpec((N_HEADS, HEAD_W, tm), lambda ip, j: (0, 0, row(ip))),
        pl.BlockSpec((tm, SEG_W), lambda ip, j: (row(ip), 0)),
        pl.BlockSpec((N_HEADS, HEAD_W, tm), lambda ip, j: (0, 0, row(ip))),
        pl.BlockSpec((N_HEADS, HEAD_W, tm), lambda ip, j: (0, 0, row(ip))),
        pl.BlockSpec((tm, SEG_W), lambda ip, j: (ip, 0)),
        pl.BlockSpec((N_HEADS, V_ROWS, tm), lambda ip, j: (0, 0, ip)),
    )
    return pl.pallas_call(
        kern,
        out_shape=out_shape,
        grid=(nrb + 2 * pb, N_SEG),
        in_specs=in_specs,
        out_specs=out_specs,
        scratch_shapes=[pltpu.VMEM((tm, D_MODEL), BF16)],
        compiler_params=_cparams(("arbitrary", "arbitrary")),
        name="inproj",
    )(x, a_norm.reshape(1, D_MODEL).astype(F32), w_in_p, gains, cos_t, sin_t, gmat)


def _flash_step(s, m_old, acc_ref, vT_t):
    m_new = jnp.maximum(m_old, jnp.max(s, axis=0, keepdims=True))
    alpha = jnp.exp2(m_old - m_new)
    p = jnp.exp2((s - m_new).astype(BF16))
    acc_ref[...] = acc_ref[...] * alpha + jnp.dot(vT_t, p, preferred_element_type=F32)
    return m_new


def _flash_step_sum(s, m_old, l_old, acc_ref, vT_t):
    m_new = jnp.maximum(m_old, jnp.max(s, axis=0, keepdims=True))
    alpha = jnp.exp2(m_old - m_new)
    e = jnp.exp2(s - m_new)
    acc_ref[...] = acc_ref[...] * alpha + jnp.dot(vT_t, e.astype(BF16), preferred_element_type=F32)
    return m_new, l_old * alpha + jnp.sum(e, axis=0, keepdims=True)


def _head_out(aT, gain_row, out_scale):
    a = aT.T
    ms = jnp.mean(a * a, axis=-1, keepdims=True)
    y = a * lax.rsqrt(ms + EPS) * gain_row
    if out_scale != 1.0:
        y = y * out_scale
    return y


def _diff_attn_kernel(*refs, lam_init, tk, n_riders):
    lq1_ref, lk1_ref, lq2_ref, lk2_ref, qT_ref, k_ref, vT_ref, og_ref, gmat_ref = refs[:9]
    rider_in = refs[9:9 + n_riders]
    o_ref = refs[9 + n_riders]
    rider_out = refs[10 + n_riders:10 + 2 * n_riders]
    acc0, acc1, lsum, k0_scr, k1_scr, kmax_scr = refs[10 + 2 * n_riders:]
    tq = qT_ref.shape[1]
    seq = k_ref.shape[0]
    lane0, lane1 = MAP1_LANE, MAP0_LANE

    @pl.when(pl.program_id(1) == 0)
    def _():
        ck = min(1024, seq)

        def kchunk(c, mx):
            rows = pl.ds(pl.multiple_of(c * ck, ck), ck)
            kc = k_ref[rows, :]
            kf = kc.astype(F32)
            n2 = jnp.dot((kf * kf).astype(BF16), gmat_ref[...], preferred_element_type=F32)
            lane = lax.broadcasted_iota(jnp.int32, kc.shape, 1)
            one = jnp.ones_like(kc)
            k0_scr[rows, :] = jnp.where(lane == lane0, one, kc)
            k1_scr[rows, :] = jnp.where(lane == lane1, one, kc)
            return jnp.maximum(mx, jnp.max(n2, axis=0, keepdims=True))

        kn2 = lax.fori_loop(0, seq // ck, kchunk, jnp.zeros((1, HEAD_W), F32))
        kmax_scr[...] = jnp.sqrt(kn2)

    qT = qT_ref[...]
    row = lax.broadcasted_iota(jnp.int32, qT.shape, 0)
    in_map1 = ((row // 32) % 2) == 1
    zero = jnp.zeros_like(qT)
    qf = qT.astype(F32)
    q2 = qf * qf
    nq0 = jnp.sum(jnp.where(in_map1, 0.0, q2), axis=0, keepdims=True)
    nq1 = jnp.sum(jnp.where(in_map1, q2, 0.0), axis=0, keepdims=True)
    kmax = kmax_scr[...]
    b0 = jnp.sqrt(nq0) * kmax[:, MAP0_LANE:MAP0_LANE + 1]
    b1 = jnp.sqrt(nq1) * kmax[:, MAP1_LANE:MAP1_LANE + 1]
    q0 = jnp.where(in_map1, zero, qT)
    q1 = jnp.where(in_map1, qT, zero)
    q0s = jnp.where(row == lane0, jnp.broadcast_to(-b0, qf.shape).astype(BF16), q0)
    q1s = jnp.where(row == lane1, jnp.broadcast_to(-b1, qf.shape).astype(BF16), q1)
    acc0[...] = jnp.zeros(acc0.shape, F32)
    acc1[...] = jnp.zeros(acc1.shape, F32)

    def body(kt, carry):
        l0, l1 = carry
        rows = pl.ds(pl.multiple_of(kt * tk, tk), tk)
        vT_t = vT_ref[:, rows]
        e0 = jnp.exp2(jnp.dot(k0_scr[rows, :], q0s, preferred_element_type=F32))
        acc0[...] += jnp.dot(vT_t, e0.astype(BF16), preferred_element_type=F32)
        e1 = jnp.exp2(jnp.dot(k1_scr[rows, :], q1s, preferred_element_type=F32))
        acc1[...] += jnp.dot(vT_t, e1.astype(BF16), preferred_element_type=F32)
        for w_ref, wb_ref in zip(rider_in, rider_out):
            r = w_ref.shape[0] // (seq // tk)
            part = pl.ds(pl.multiple_of(kt * r, 16), r)
            wb_ref[part, :] = w_ref[part, :].astype(BF16)
        return l0 + jnp.sum(e0, axis=0, keepdims=True), l1 + jnp.sum(e1, axis=0, keepdims=True)

    l_init = jnp.zeros((1, tq), F32)
    l0, l1 = lax.fori_loop(0, seq // tk, body, (l_init, l_init))
    lsum[0:1, :] = l0
    lsum[1:2, :] = l1

    @pl.when(jnp.logical_not(jnp.min(jnp.minimum(l0, l1)) >= ROW_SUM_FLOOR))
    def _():
        acc0[...] = jnp.zeros(acc0.shape, F32)
        acc1[...] = jnp.zeros(acc1.shape, F32)

        def robust(kt, carry):
            m0, l0, m1, l1 = carry
            rows = pl.ds(pl.multiple_of(kt * tk, tk), tk)
            k_t = k_ref[rows, :]
            vT_t = vT_ref[:, rows]
            m0, l0 = _flash_step_sum(jnp.dot(k_t, q0, preferred_element_type=F32), m0, l0, acc0, vT_t)
            m1, l1 = _flash_step_sum(jnp.dot(k_t, q1, preferred_element_type=F32), m1, l1, acc1, vT_t)
            return m0, l0, m1, l1

        m_init = jnp.full((1, tq), NEG, F32)
        _, l0, _, l1 = lax.fori_loop(0, seq // tk, robust, (m_init, l_init, m_init, l_init))
        lsum[0:1, :] = l0
        lsum[1:2, :] = l1

    lam = (jnp.exp(jnp.sum(lq1_ref[...] * lk1_ref[...], axis=-1, keepdims=True))
           - jnp.exp(jnp.sum(lq2_ref[...] * lk2_ref[...], axis=-1, keepdims=True)) + lam_init)
    o0 = acc0[...] * (1.0 / lsum[0:1, :])
    o1 = acc1[...] * (1.0 / lsum[1:2, :])
    aT = o0 - lam * o1
    o_ref[...] = _head_out(aT, og_ref[...], 1.0 - lam_init).astype(o_ref.dtype)


def _diff_attn(qaT, ka, vaT, lq1, lk1, lq2, lk2, og, gmat_diff, lam_init, riders=()):
    seq = ka.shape[0]
    tq = min(ATT_TQ, seq)
    tk = min(ATT_TK, seq)
    nq = seq // tq
    n_steps = N_HEADS * nq
    kern = functools.partial(_diff_attn_kernel, lam_init=lam_init, tk=tk, n_riders=len(riders))
    vec = lambda v: v.reshape(1, -1).astype(F32)
    small = pl.BlockSpec((1, DIFF_DIM), lambda h, qi: (0, 0))
    once = pl.Buffered(1)
    rider_specs = []
    for w in riders:
        n_e, n_r, n_c = w.shape
        per_e = n_steps // n_e
        rider_specs.append(pl.BlockSpec(
            (None, n_r // per_e, n_c),
            lambda h, qi, per_e=per_e: ((h * nq + qi) // per_e, (h * nq + qi) % per_e, 0)))
    outs = pl.pallas_call(
        kern,
        out_shape=(jax.ShapeDtypeStruct((seq, SEG_W), BF16),
                   *[jax.ShapeDtypeStruct(w.shape, BF16) for w in riders]),
        grid=(N_HEADS, nq),
        in_specs=[small, small, small, small,
                  pl.BlockSpec((None, HEAD_W, tq), lambda h, qi: (h, 0, qi)),
                  pl.BlockSpec((seq, HEAD_W), lambda h, qi: (0, h), pipeline_mode=once),
                  pl.BlockSpec((None, HEAD_W, seq), lambda h, qi: (h, 0, 0), pipeline_mode=once),
                  pl.BlockSpec((1, HEAD_W), lambda h, qi: (0, 0)),
                  pl.BlockSpec((HEAD_W, HEAD_W), lambda h, qi: (0, 0)),
                  *rider_specs],
        out_specs=(pl.BlockSpec((tq, HEAD_W), lambda h, qi: (qi, h)), *rider_specs),
        scratch_shapes=[pltpu.VMEM((HEAD_W, tq), F32), pltpu.VMEM((HEAD_W, tq), F32),
                        pltpu.VMEM((8, tq), F32),
                        pltpu.VMEM((seq, HEAD_W), BF16), pltpu.VMEM((seq, HEAD_W), BF16),
                        pltpu.VMEM((1, HEAD_W), F32)],
        compiler_params=_cparams(("arbitrary", "arbitrary")),
        name="diff_attn",
    )(vec(lq1), vec(lk1), vec(lq2), vec(lk2), qaT, ka, vaT, vec(og), gmat_diff, *riders)
    return outs[0], tuple(outs[1:])


def _dil_chunks(tq):
    chunks, off = [], 0
    for _, dil in DIL_BRANCHES:
        pad = -(-(N_SIDE * dil) // 128) * 128
        total = tq + 2 * pad
        c0 = 0
        while c0 < total:
            nk = min(DIL_CHUNK, total - c0)
            chunks.append((dil, c0 - pad, nk, off))
            off += nk
            c0 += nk
    return chunks, off


def _dil_bias(tq):
    chunks, total = _dil_chunks(tq)
    bias = np.full((total, tq), NEG, np.float32)
    col = np.arange(tq)[None, :]
    for dil, rel, nk, off in chunks:
        delta = rel + np.arange(nk)[:, None] - col
        ok = (np.abs(delta) <= N_SIDE * dil) & (delta % dil == 0)
        bias[off:off + nk][ok] = 0.0
    return jnp.asarray(bias)


def _dil_attn_kernel(qT_ref, k_ref, vT_ref, bias_ref, og_ref, gmat_ref, o_ref, acc, kmax_scr, *, seq, chunks):
    tq = acc.shape[2]
    n_sub = acc.shape[0]
    base = pl.program_id(1) * (tq * n_sub)

    @pl.when(pl.program_id(1) == 0)
    def _():
        ck = 1024

        def kchunk(c, mx):
            kf = k_ref[pl.ds(pl.multiple_of(c * ck, ck), ck), :].astype(F32)
            n2 = jnp.dot((kf * kf).astype(BF16), gmat_ref[...], preferred_element_type=F32)
            return jnp.maximum(mx, jnp.max(n2, axis=0, keepdims=True))

        kn2 = lax.fori_loop(0, k_ref.shape[0] // ck, kchunk, jnp.zeros((1, HEAD_W), F32))
        kmax_scr[...] = jnp.sqrt(kn2)

    def windows(i0):
        for dil, rel, nk, off in chunks:
            start = pl.multiple_of(i0 + (DIL_PAD + rel), 128)
            yield rel, nk, off, k_ref[pl.ds(start, nk), :], vT_ref[:, pl.ds(start, nk)]

    lmin = None
    for t in range(n_sub):
        qT = qT_ref[:, t * tq:(t + 1) * tq]
        qf = qT.astype(F32)
        shift = jnp.sqrt(jnp.sum(qf * qf, axis=0, keepdims=True)) * kmax_scr[:, 0:1]
        num = jnp.zeros((V_ROWS, tq), F32)
        for rel, nk, off, k_t, vT_t in windows(base + t * tq):
            s = jnp.dot(k_t, qT, preferred_element_type=F32) - shift + bias_ref[off:off + nk, :]
            num = num + jnp.dot(vT_t, jnp.exp2(s).astype(BF16), preferred_element_type=F32)
        acc[t] = num
        l_t = jnp.min(num[HEAD_W:HEAD_W + 1, :])
        lmin = l_t if lmin is None else jnp.minimum(lmin, l_t)

    @pl.when(jnp.logical_not(lmin >= ROW_SUM_FLOOR))
    def _():
        for t in range(n_sub):
            i0 = base + t * tq
            qT = qT_ref[:, t * tq:(t + 1) * tq]
            acc[t] = jnp.zeros((V_ROWS, tq), F32)
            m = jnp.full((1, tq), NEG, F32)
            for rel, nk, off, k_t, vT_t in windows(i0):
                s = jnp.dot(k_t, qT, preferred_element_type=F32)
                kpos = lax.broadcasted_iota(jnp.int32, (nk, tq), 0) + (i0 + rel)
                valid = jnp.logical_and(kpos >= 0, kpos < seq)
                s = jnp.where(valid, s + bias_ref[off:off + nk, :], NEG)
                m = _flash_step(s, m, acc.at[t], vT_t)

    for t in range(n_sub):
        oT = acc[t, 0:HEAD_W, :] * (1.0 / acc[t, HEAD_W:HEAD_W + 1, :])
        o_ref[t * tq:(t + 1) * tq, :] = _head_out(oT, og_ref[...], 1.0).astype(o_ref.dtype)


def _dil_attn(qbT, kb, vbT, og, gmat_ones):
    seq = qbT.shape[2]
    seq_p = kb.shape[0]
    tq = min(DIL_TQ, seq)
    n_sub = DIL_SUBTILES
    chunks, _ = _dil_chunks(tq)
    bias = _dil_bias(tq)
    kern = functools.partial(_dil_attn_kernel, seq=seq, chunks=chunks)
    return pl.pallas_call(
        kern,
        out_shape=jax.ShapeDtypeStruct((seq, SEG_W), BF16),
        grid=(N_HEADS, seq // (tq * n_sub)),
        in_specs=[pl.BlockSpec((None, HEAD_W, tq * n_sub), lambda h, qi: (h, 0, qi)),
                  pl.BlockSpec((seq_p, HEAD_W), lambda h, qi: (0, h)),
                  pl.BlockSpec((None, V_ROWS, seq_p), lambda h, qi: (h, 0, 0)),
                  pl.BlockSpec(bias.shape, lambda h, qi: (0, 0)),
                  pl.BlockSpec((1, HEAD_W), lambda h, qi: (0, 0)),
                  pl.BlockSpec((HEAD_W, HEAD_W), lambda h, qi: (0, 0))],
        out_specs=pl.BlockSpec((tq * n_sub, HEAD_W), lambda h, qi: (qi, h)),
        scratch_shapes=[pltpu.VMEM((n_sub, V_ROWS, tq), F32), pltpu.VMEM((1, HEAD_W), F32)],
        compiler_params=_cparams(("arbitrary", "arbitrary")),
        name="dil_attn",
    )(qbT, kb, vbT, bias, og.reshape(1, HEAD_W).astype(F32), gmat_ones)


def _outproj_kernel(*refs, with_router):
    if with_router:
        a_ref, b_ref, w_ref, x_ref, g_ref, r_hi_ref, r_lo_ref, x1_ref, h_ref, route_ref = refs
    else:
        a_ref, b_ref, w_ref, x_ref, g_ref, x1_ref, h_ref = refs
    acc = (jnp.dot(a_ref[...], w_ref[0:SEG_W, :], preferred_element_type=F32)
           + jnp.dot(b_ref[...], w_ref[SEG_W:2 * SEG_W, :], preferred_element_type=F32))
    x1 = x_ref[...] + acc
    x1_ref[...] = x1
    ms = jnp.mean(x1 * x1, axis=-1, keepdims=True)
    hn = x1 * lax.rsqrt(ms + EPS) * g_ref[...]
    if not with_router:
        h_ref[...] = hn.astype(BF16)
    else:
        tm = hn.shape[0]
        for s in range(ROW_CHUNKS):
            h_ref[pl.ds(s, tm, stride=ROW_CHUNKS), :] = hn[:, s * HEAD_W:(s + 1) * HEAD_W]
        h_hi = hn.astype(BF16)
        h_lo = (hn - h_hi.astype(F32)).astype(BF16)
        logits = (jnp.dot(h_hi, r_hi_ref[...], preferred_element_type=F32)
                  + jnp.dot(h_hi, r_lo_ref[...], preferred_element_type=F32)
                  + jnp.dot(h_lo, r_hi_ref[...], preferred_element_type=F32))
        lane = lax.broadcasted_iota(jnp.int32, logits.shape, 1)
        ninf = jnp.float32(-jnp.inf)
        lg = jnp.where(lane < N_EXPERTS, logits, ninf)
        v1 = jnp.max(lg, axis=-1, keepdims=True)
        i1 = jnp.min(jnp.where(lg == v1, lane, HEAD_W), axis=-1, keepdims=True)
        lg2 = jnp.where(lane == i1, ninf, lg)
        v2 = jnp.max(lg2, axis=-1, keepdims=True)
        i2 = jnp.min(jnp.where(lg2 == v2, lane, HEAD_W), axis=-1, keepdims=True)
        g1 = 1.0 / (1.0 + jnp.exp(v2 - v1))
        g2 = 1.0 - g1
        route_ref[...] = jnp.where(lane == 0, i1.astype(F32),
                                   jnp.where(lane == 1, i2.astype(F32),
                                             jnp.where(lane == 2, g1, jnp.where(lane == 3, g2, 0.0))))


def _outproj(out_a, out_b, w_out, x, f_norm, router=None):
    seq = x.shape[0]
    tm = ROW_TILE
    with_router = router is not None
    in_specs = [pl.BlockSpec((tm, SEG_W), lambda i: (i, 0)),
                pl.BlockSpec((tm, SEG_W), lambda i: (i, 0)),
                pl.BlockSpec((D_MODEL, D_MODEL), lambda i: (0, 0)),
                pl.BlockSpec((tm, D_MODEL), lambda i: (i, 0)),
                pl.BlockSpec((1, D_MODEL), lambda i: (0, 0))]
    args = [out_a, out_b, w_out.astype(BF16), x, f_norm.reshape(1, D_MODEL).astype(F32)]
    out_shape = [jax.ShapeDtypeStruct((seq, D_MODEL), F32)]
    out_specs = [pl.BlockSpec((tm, D_MODEL), lambda i: (i, 0))]
    if not with_router:
        out_shape.append(jax.ShapeDtypeStruct((seq, D_MODEL), BF16))
        out_specs.append(pl.BlockSpec((tm, D_MODEL), lambda i: (i, 0)))
    else:
        out_shape.append(jax.ShapeDtypeStruct((seq * ROW_CHUNKS, HEAD_W), F32))
        out_specs.append(pl.BlockSpec((tm * ROW_CHUNKS, HEAD_W), lambda i: (i, 0)))
        r = jnp.zeros((D_MODEL, HEAD_W), F32).at[:, :N_EXPERTS].set(router.astype(F32))
        r_hi = r.astype(BF16)
        r_lo = (r - r_hi.astype(F32)).astype(BF16)
        in_specs += [pl.BlockSpec((D_MODEL, HEAD_W), lambda i: (0, 0))] * 2
        args += [r_hi, r_lo]
        out_shape.append(jax.ShapeDtypeStruct((seq, HEAD_W), F32))
        out_specs.append(pl.BlockSpec((tm, HEAD_W), lambda i: (i, 0)))
    return pl.pallas_call(
        functools.partial(_outproj_kernel, with_router=with_router),
        out_shape=tuple(out_shape),
        grid=(seq // tm,),
        in_specs=in_specs,
        out_specs=tuple(out_specs),
        compiler_params=_cparams(("arbitrary",)),
        name="outproj",
    )(*args)


def _swiglu_part(h, wg, wu, wd):
    g = jnp.dot(h, wg, preferred_element_type=F32)
    u = jnp.dot(h, wu, preferred_element_type=F32)
    act = (g * (1.0 / (1.0 + jnp.exp(-g))) * u).astype(BF16)
    return jnp.dot(act, wd, preferred_element_type=F32)


def _dense_ffn_kernel(h_ref, x_ref, wg_ref, wu_ref, wd_ref, o_ref):
    @pl.when(pl.program_id(1) == 0)
    def _():
        o_ref[...] = x_ref[...]

    o_ref[...] += _swiglu_part(h_ref[...], wg_ref[...], wu_ref[...], wd_ref[...])


def _dense_ffn(h, x1, wg, wu, wd):
    seq = x1.shape[0]
    tm, tf = ROW_TILE, FF_TILE
    return pl.pallas_call(
        _dense_ffn_kernel,
        out_shape=jax.ShapeDtypeStruct((seq, D_MODEL), F32),
        grid=(seq // tm, D_FF // tf),
        in_specs=[pl.BlockSpec((tm, D_MODEL), lambda i, j: (i, 0)),
                  pl.BlockSpec((tm, D_MODEL), lambda i, j: (i, 0)),
                  pl.BlockSpec((D_MODEL, tf), lambda i, j: (0, j)),
                  pl.BlockSpec((D_MODEL, tf), lambda i, j: (0, j)),
                  pl.BlockSpec((tf, D_MODEL), lambda i, j: (j, 0))],
        out_specs=pl.BlockSpec((tm, D_MODEL), lambda i, j: (i, 0)),
        compiler_params=_cparams(("arbitrary", "arbitrary")),
        name="dense_ffn",
    )(h, x1, wg.astype(BF16), wu.astype(BF16), wd.astype(BF16))


def _moe_ffn_kernel(be_ref, nused_ref, src_ref, src_next_ref, dst_ref, dst_prev_ref,
                    h3_ref, wg_ref, wu_ref, wd_ref, y_ref, x3, xb, acc, o3, gsem, ssem):
    b = pl.program_id(0)
    j = pl.program_id(1)
    nb = pl.num_programs(0)
    nj = pl.num_programs(1)
    tm = xb.shape[0]
    n_used = nused_ref[0]
    used = b < n_used

    def slab(ref, row0):
        return ref.at[pl.ds(pl.multiple_of(row0, 8), ROW_CHUNKS), :]

    def start_gather(idx_ref, slot):
        def start(r, c):
            pltpu.make_async_copy(slab(h3_ref, idx_ref[0, r]), slab(x3.at[slot], r * SLAB_PITCH),
                                  gsem.at[slot]).start()
            return c

        lax.fori_loop(0, tm, start, 0)

    def scatter_copy(idx_ref, r):
        return pltpu.make_async_copy(slab(o3, r * SLAB_PITCH), slab(y_ref, idx_ref[0, r]), ssem)

    def wait_gather(slot):
        n_rows = tm * ROW_CHUNKS
        pltpu.make_async_copy(h3_ref.at[pl.ds(0, n_rows), :], x3.at[slot, pl.ds(0, n_rows), :],
                              gsem.at[slot]).wait()

    def wait_scatter():
        n_rows = tm * ROW_CHUNKS
        pltpu.make_async_copy(o3.at[pl.ds(0, n_rows), :], y_ref.at[pl.ds(0, n_rows), :], ssem).wait()

    def flush_scatter(idx_ref):
        def start(r, c):
            scatter_copy(idx_ref, r).start()
            return c

        lax.fori_loop(0, tm, start, 0)
        wait_scatter()

    @pl.when(jnp.logical_and(j == 0, jnp.logical_and(b == 0, used)))
    def _():
        start_gather(src_ref, 0)
        o3[...] = jnp.zeros(o3.shape, F32)

    @pl.when(jnp.logical_and(j == 0, jnp.logical_and(b >= 1, b == n_used)))
    def _():
        flush_scatter(dst_prev_ref)

    @pl.when(jnp.logical_and(j == 0, jnp.where(b == 0, used, b - 1 < n_used)))
    def _():
        wait_gather(b % 2)

    @pl.when(jnp.logical_and(j == 0, used))
    def _():
        slot = b % 2
        for s in range(ROW_CHUNKS):
            xb[:, s * HEAD_W:(s + 1) * HEAD_W] = x3[slot, pl.ds(s, tm, stride=SLAB_PITCH), :].astype(BF16)
        acc[...] = jnp.zeros(acc.shape, F32)

    @pl.when(used)
    def _():
        per_step = tm // D_EXPERT_STEPS
        nslot = (b + 1) % 2

        def move(r):
            pltpu.make_async_copy(slab(h3_ref, src_next_ref[0, r]), slab(x3.at[nslot], r * SLAB_PITCH),
                                  gsem.at[nslot]).start()
            scatter_copy(dst_prev_ref, r).start()

        for i in range(per_step):
            move(j * per_step + i)

        @pl.when(j == 0)
        def _():
            for r in range(per_step * D_EXPERT_STEPS, tm):
                move(r)

        acc[...] += _swiglu_part(xb[...], wg_ref[...], wu_ref[...], wd_ref[...])

    @pl.when(jnp.logical_and(j == nj - 1, used))
    def _():
        wait_scatter()
        for s in range(ROW_CHUNKS):
            o3[pl.ds(s, tm, stride=SLAB_PITCH), :] = acc[:, s * HEAD_W:(s + 1) * HEAD_W]

        @pl.when(b == nb - 1)
        def _():
            flush_scatter(dst_ref)
            wait_gather((b + 1) % 2)


def _moe_ffn(h3, src_idx, dst_idx, dst_prev_idx, blk_expert, n_used, wg, wu, wd):
    seq = h3.shape[0] // ROW_CHUNKS
    n_blocks = src_idx.shape[0]
    tm, tf = MOE_TILE, MOE_FF_TILE
    nj = D_EXPERT // tf

    def jj(b, j, be, nu):
        return jnp.where(b < nu[0], j, nj - 1)

    def idx_spec(shift):
        return pl.BlockSpec((None, 1, tm), lambda b, j, be, nu: (jnp.clip(b + shift, 0, n_blocks - 1), 0, 0),
                            memory_space=pltpu.SMEM)

    return pl.pallas_call(
        _moe_ffn_kernel,
        out_shape=jax.ShapeDtypeStruct(((2 * seq + tm) * ROW_CHUNKS, HEAD_W), F32),
        grid_spec=pltpu.PrefetchScalarGridSpec(
            num_scalar_prefetch=2,
            grid=(n_blocks, nj),
            in_specs=[idx_spec(0), idx_spec(1), idx_spec(0), idx_spec(0),
                      pl.BlockSpec(memory_space=pl.ANY),
                      pl.BlockSpec((None, D_MODEL, tf), lambda b, j, be, nu: (be[b], 0, jj(b, j, be, nu))),
                      pl.BlockSpec((None, D_MODEL, tf), lambda b, j, be, nu: (be[b], 0, jj(b, j, be, nu))),
                      pl.BlockSpec((None, tf, D_MODEL), lambda b, j, be, nu: (be[b], jj(b, j, be, nu), 0))],
            out_specs=pl.BlockSpec(memory_space=pl.ANY),
            scratch_shapes=[pltpu.VMEM((2, tm * SLAB_PITCH, HEAD_W), F32),
                            pltpu.VMEM((tm, D_MODEL), BF16),
                            pltpu.VMEM((tm, D_MODEL), F32),
                            pltpu.VMEM((tm * SLAB_PITCH, HEAD_W), F32),
                            pltpu.SemaphoreType.DMA((2,)),
                            pltpu.SemaphoreType.DMA(())]),
        compiler_params=_cparams(("arbitrary", "arbitrary")),
        name="moe_ffn",
    )(blk_expert, n_used, src_idx, src_idx, dst_idx, dst_prev_idx, h3, wg, wu, wd)


def _combine_kernel(x_ref, ya_ref, yb_ref, route_ref, o_ref):
    r = route_ref[...]
    g1 = jnp.broadcast_to(r[:, 2:3], (r.shape[0], HEAD_W))
    g2 = jnp.broadcast_to(r[:, 3:4], (r.shape[0], HEAD_W))
    tm = r.shape[0]
    for s in range(ROW_CHUNKS):
        cols = slice(s * HEAD_W, (s + 1) * HEAD_W)
        rows = pl.ds(s, tm, stride=ROW_CHUNKS)
        o_ref[:, cols] = x_ref[:, cols] + g1 * ya_ref[rows, :] + g2 * yb_ref[rows, :]


def _moe_combine(x1, y, route):
    seq = x1.shape[0]
    tm = ROW_TILE
    nb = seq // tm
    big = pl.BlockSpec((tm, D_MODEL), lambda i: (i, 0))
    return pl.pallas_call(
        _combine_kernel,
        out_shape=jax.ShapeDtypeStruct((seq, D_MODEL), F32),
        grid=(nb,),
        in_specs=[big,
                  pl.BlockSpec((tm * ROW_CHUNKS, HEAD_W), lambda i: (i, 0)),
                  pl.BlockSpec((tm * ROW_CHUNKS, HEAD_W), lambda i: (i + nb, 0)),
                  pl.BlockSpec((tm, HEAD_W), lambda i: (i, 0))],
        out_specs=big,
        compiler_params=_cparams(("arbitrary",)),
        name="moe_combine",
    )(x1, y, y, route)


def _moe_plan(route, seq):
    tm = MOE_TILE
    n_assign = 2 * seq
    n_blocks = n_assign // tm + N_EXPERTS
    e = route[:, 0:2].astype(jnp.int32).reshape(-1)
    onehot = (e[:, None] == jnp.arange(N_EXPERTS, dtype=jnp.int32)[None, :]).astype(jnp.int32)
    csum = jnp.cumsum(onehot, axis=0)
    counts = csum[-1]
    rank = jnp.sum((csum - onehot) * onehot, axis=1)
    nblk = (counts + tm - 1) // tm
    cum_blk = jnp.cumsum(nblk)
    pad_start = (cum_blk - nblk) * tm
    pos = pad_start[e] + rank
    assign = jnp.arange(n_assign, dtype=jnp.int32)
    slot_assign = jnp.full((n_blocks * tm,), -1, jnp.int32).at[pos].set(assign)
    real = slot_assign >= 0
    tok = jnp.maximum(slot_assign, 0) // 2
    in_block = jnp.arange(n_blocks * tm, dtype=jnp.int32) % tm
    src_idx = tok * ROW_CHUNKS
    dst_idx = jnp.where(real, (slot_assign % 2) * seq + tok, n_assign + in_block) * ROW_CHUNKS
    src_idx = src_idx.reshape(n_blocks, 1, tm)
    dst_idx = dst_idx.reshape(n_blocks, 1, tm)
    scratch_blk = ((n_assign + jnp.arange(tm, dtype=jnp.int32)) * ROW_CHUNKS).reshape(1, 1, tm)
    dst_prev_idx = jnp.concatenate([scratch_blk, dst_idx[:-1]], axis=0)
    n_used = cum_blk[-1].astype(jnp.int32)
    blk = jnp.arange(n_blocks, dtype=jnp.int32)
    blk_expert = jnp.searchsorted(cum_blk, jnp.minimum(blk, n_used - 1), side='right').astype(jnp.int32)
    blk_expert = jnp.minimum(blk_expert, N_EXPERTS - 1)
    return src_idx, dst_idx, dst_prev_idx, blk_expert, n_used.reshape(1)


def _moe(h3, x1, route, wg, wu, wd):
    seq = x1.shape[0]
    src_idx, dst_idx, dst_prev_idx, blk_expert, n_used = _moe_plan(route, seq)
    y = _moe_ffn(h3, src_idx, dst_idx, dst_prev_idx, blk_expert, n_used, wg, wu, wd)
    return _moe_combine(x1, y, route)


def _layer(x, i, a_norm, w_in, dqn, dkn, lq1, lk1, lq2, lk2, don, bqn, bkn, bon, w_out, f_norm, ffn, riders=()):
    seq = x.shape[0]
    lam_init = 0.8 - 0.6 * math.exp(-0.3 * i)
    gains, gmat, cos_t, sin_t = _segment_tables(seq, dqn, dkn, bqn, bkn)
    qaT, ka, vaT, qbT, kb, vbT = _inproj(x, a_norm, _permute_w_in(w_in), gains, gmat, cos_t, sin_t)
    out_a, casted = _diff_attn(qaT, ka, vaT, lq1, lk1, lq2, lk2, don, gmat[0, :HEAD_W, :HEAD_W], lam_init, riders)
    out_b = _dil_attn(qbT, kb, vbT, bon, gmat[1, :HEAD_W, :HEAD_W])
    if len(ffn) == 3:
        x1, h = _outproj(out_a, out_b, w_out, x, f_norm)
        return _dense_ffn(h, x1, *ffn), casted
    router, wg, wu, wd = ffn
    x1, h3, route = _outproj(out_a, out_b, w_out, x, f_norm, router)
    return _moe(h3, x1, route, wg, wu, wd), casted


def kernel(x, attn_norm_0, w_in_0, diff_q_norm_0, diff_k_norm_0, diff_lam_q1_0, diff_lam_k1_0, diff_lam_q2_0, diff_lam_k2_0, diff_out_norm_0, dil_q_norm_0, dil_k_norm_0, dil_out_norm_0, w_out_0, ffn_norm_0, ffn_w_gate_0, ffn_w_up_0, ffn_w_down_0, attn_norm_1, w_in_1, diff_q_norm_1, diff_k_norm_1, diff_lam_q1_1, diff_lam_k1_1, diff_lam_q2_1, diff_lam_k2_1, diff_out_norm_1, dil_q_norm_1, dil_k_norm_1, dil_out_norm_1, w_out_1, ffn_norm_1, router_1, moe_w_gate_1, moe_w_up_1, moe_w_down_1):
    b, seq, d = x.shape
    assert b == 1 and d == D_MODEL and seq % ROW_TILE == 0
    xs = x.reshape(seq, d)
    xs, moe_w = _layer(xs, 0, attn_norm_0, w_in_0, diff_q_norm_0, diff_k_norm_0, diff_lam_q1_0, diff_lam_k1_0,
                       diff_lam_q2_0, diff_lam_k2_0, diff_out_norm_0, dil_q_norm_0, dil_k_norm_0, dil_out_norm_0,
                       w_out_0, ffn_norm_0, (ffn_w_gate_0, ffn_w_up_0, ffn_w_down_0),
                       riders=(moe_w_gate_1, moe_w_up_1, moe_w_down_1))
    xs, _ = _layer(xs, 1, attn_norm_1, w_in_1, diff_q_norm_1, diff_k_norm_1, diff_lam_q1_1, diff_lam_k1_1,
                   diff_lam_q2_1, diff_lam_k2_1, diff_out_norm_1, dil_q_norm_1, dil_k_norm_1, dil_out_norm_1,
                   w_out_1, ffn_norm_1, (router_1, *moe_w))
    return xs.reshape(b, seq, d)
```

```python
import functools
import math

import numpy as np
import jax
import jax.numpy as jnp
from jax import lax
from jax.experimental import pallas as pl
from jax.experimental.pallas import tpu as pltpu

F32 = jnp.float32
BF16 = jnp.bfloat16

D_MODEL = 2048
N_HEADS = 8
HEAD_W = 128
DIFF_DIM = 64
SEG_W = N_HEADS * HEAD_W
N_SEG = 6
ROW_CHUNKS = D_MODEL // HEAD_W
SLAB_PITCH = 24
DIL_BRANCHES = ((128, 1), (512, 4), (2048, 16))
N_SIDE = 64
D_FF = 5632
N_EXPERTS = 8
D_EXPERT = 7168
EPS = 1e-6
NEG = -1e30
LOG2E = 1.4426950408889634

MAP0_LANE = 0
MAP1_LANE = 32
ROW_SUM_FLOOR = 2.0 ** -80

V_ROWS = 144
DIL_PAD = 1024
V7X_VMEM_BYTES = 64 * 1024 * 1024
VMEM_LIMIT = V7X_VMEM_BYTES * 7 // 8

ROW_TILE = 512
ATT_TQ = 512
ATT_TK = 8192
DIL_TQ = 256
DIL_SUBTILES = 4
DIL_CHUNK = 768
FF_TILE = 512
D_EXPERT_STEPS = 7
MOE_FF_TILE = D_EXPERT // D_EXPERT_STEPS
MOE_TILE = 512


def _cparams(sem):
    return pltpu.CompilerParams(dimension_semantics=sem, vmem_limit_bytes=VMEM_LIMIT)


def _diff_lane_perm():
    perm = np.zeros(HEAD_W, np.int32)
    for m in range(2):
        for t in range(DIFF_DIM):
            p = (t // 32) * 64 + m * 32 + (t % 32)
            perm[p] = m * DIFF_DIM + t
    return perm


def _segment_tables(seq, dqn, dkn, bqn, bkn):
    perm = _diff_lane_perm()
    t_of_lane = perm % DIFF_DIM
    ones = jnp.ones((SEG_W,), F32)
    g_qa = jnp.tile(dqn.astype(F32)[t_of_lane], N_HEADS)
    g_ka = jnp.tile(dkn.astype(F32)[t_of_lane], N_HEADS)
    g_qb = jnp.tile(bqn.astype(F32), N_HEADS)
    g_kb = jnp.tile(bkn.astype(F32), N_HEADS)
    gains = jnp.stack([g_qa, g_ka, ones, g_qb, g_kb, ones]).reshape(N_SEG, 1, SEG_W)

    lane = np.arange(HEAD_W)
    map_of_lane = (lane // 32) % 2
    g_diff = (map_of_lane[:, None] == map_of_lane[None, :]).astype(np.float32)
    g_dil = np.ones((HEAD_W, HEAD_W), np.float32)
    pair = np.eye(2, dtype=np.float32)
    gmat = jnp.asarray(np.stack([np.kron(pair, g_diff), np.kron(pair, g_dil)]), BF16)

    pos = jnp.arange(seq, dtype=F32)

    def ang(dim):
        inv = 10000.0 ** (-jnp.arange(0, dim, 2, dtype=F32) / dim)
        return pos[:, None] * inv[None, :]

    a32 = ang(DIFF_DIM)
    a64 = ang(HEAD_W)
    cos_a = jnp.tile(jnp.cos(a32), (1, 4))
    sin_a = jnp.tile(jnp.sin(a32), (1, 4))
    cos_b = jnp.tile(jnp.cos(a64), (1, 2))
    sin_b = jnp.tile(jnp.sin(a64), (1, 2))
    sign = jnp.where(jnp.arange(HEAD_W) < 64, -1.0, 1.0).astype(F32)[None, :]
    cos_t = jnp.stack([cos_a, cos_b])
    sin_t = jnp.stack([sin_a * sign, sin_b * sign])
    return gains, gmat, cos_t, sin_t


def _permute_w_in(w_in):
    w = w_in.astype(BF16)
    qk = w[:, :2 * SEG_W].reshape(D_MODEL, 2 * N_HEADS, 2, 2, 32)
    qk = qk.transpose(0, 1, 3, 2, 4).reshape(D_MODEL, 2 * SEG_W)
    return jnp.concatenate([qk, w[:, 2 * SEG_W:]], axis=1)


def _inproj_kernel(x_ref, g_ref, w_ref, gain_ref, cos_ref, sin_ref, gmat_ref,
                   qaT_ref, ka_ref, vaT_ref, qbT_ref, kb_ref, vbT_ref, h_scr,
                   *, n_row_blocks, pad_blocks):
    ip = pl.program_id(0)
    j = pl.program_id(1)
    real = jnp.logical_and(ip >= pad_blocks, ip < pad_blocks + n_row_blocks)
    tm = x_ref.shape[0]

    @pl.when(jnp.logical_and(real, j == 0))
    def _():
        x = x_ref[...]
        ms = jnp.mean(x * x, axis=-1, keepdims=True)
        h_scr[...] = (x * lax.rsqrt(ms + EPS) * g_ref[...]).astype(BF16)

    def norm_rope(y, ss, c, n_group, scale):
        yn = y * lax.rsqrt(ss * (1.0 / n_group) + EPS) * gain_ref[:, c * HEAD_W:(c + 1) * HEAD_W]
        out = yn * cos_ref[...] + pltpu.roll(yn, 64, 1) * sin_ref[...]
        if scale != 1.0:
            out = out * scale
        return out

    def aug_rows():
        row = lax.broadcasted_iota(jnp.int32, (V_ROWS - HEAD_W, tm), 0)
        return jnp.where(row == 0, 1.0, 0.0).astype(BF16)

    def segment(seg):
        acc = jnp.dot(h_scr[...], w_ref[...], preferred_element_type=F32)
        for c in range(N_HEADS):
            y = acc[:, c * HEAD_W:(c + 1) * HEAD_W]
            if seg in (0, 1, 3, 4) and c % 2 == 0:
                pair = acc[:, c * HEAD_W:(c + 2) * HEAD_W]
                ss2 = jnp.dot((pair * pair).astype(BF16), gmat_ref[...], preferred_element_type=F32)
            if seg in (0, 1, 3, 4):
                ss = ss2[:, (c % 2) * HEAD_W:(c % 2 + 1) * HEAD_W]
            if seg == 0:
                qaT_ref[c] = norm_rope(y, ss, c, DIFF_DIM, DIFF_DIM ** -0.5 * LOG2E).T.astype(BF16)
            elif seg == 1:
                ka_ref[:, c * HEAD_W:(c + 1) * HEAD_W] = norm_rope(y, ss, c, DIFF_DIM, 1.0).astype(BF16)
            elif seg == 3:
                qbT_ref[c] = norm_rope(y, ss, c, HEAD_W, HEAD_W ** -0.5 * LOG2E).T.astype(BF16)
            elif seg == 4:
                kb_ref[:, c * HEAD_W:(c + 1) * HEAD_W] = norm_rope(y, ss, c, HEAD_W, 1.0).astype(BF16)
            elif seg == 2:
                vaT_ref[c] = y.T.astype(BF16)
            else:
                vbT_ref[c, 0:HEAD_W, :] = y.T.astype(BF16)
                vbT_ref[c, HEAD_W:V_ROWS, :] = aug_rows()

    for seg in range(N_SEG):
        pl.when(jnp.logical_and(real, j == seg))(functools.partial(segment, seg))

    @pl.when(jnp.logical_and(jnp.logical_not(real), j == 4))
    def _():
        kb_ref[...] = jnp.zeros(kb_ref.shape, BF16)

    @pl.when(jnp.logical_and(jnp.logical_not(real), j == 5))
    def _():
        vbT_ref[...] = jnp.zeros(vbT_ref.shape, BF16)


def _inproj(x, a_norm, w_in_p, gains, gmat, cos_t, sin_t):
    seq = x.shape[0]
    tm = ROW_TILE
    nrb = seq // tm
    pb = DIL_PAD // tm
    seq_p = seq + 2 * DIL_PAD

    def row(ip):
        return jnp.clip(ip - pb, 0, nrb - 1)

    kern = functools.partial(_inproj_kernel, n_row_blocks=nrb, pad_blocks=pb)
    out_shape = (
        jax.ShapeDtypeStruct((N_HEADS, HEAD_W, seq), BF16),
        jax.ShapeDtypeStruct((seq, SEG_W), BF16),
        jax.ShapeDtypeStruct((N_HEADS, HEAD_W, seq), BF16),
        jax.ShapeDtypeStruct((N_HEADS, HEAD_W, seq), BF16),
        jax.ShapeDtypeStruct((seq_p, SEG_W), BF16),
        jax.ShapeDtypeStruct((N_HEADS, V_ROWS, seq_p), BF16),
    )
    in_specs = [
        pl.BlockSpec((tm, D_MODEL), lambda ip, j: (row(ip), 0)),
        pl.BlockSpec((1, D_MODEL), lambda ip, j: (0, 0)),
        pl.BlockSpec((D_MODEL, SEG_W), lambda ip, j: (0, j)),
        pl.BlockSpec((None, 1, SEG_W), lambda ip, j: (j, 0, 0)),
        pl.BlockSpec((None, tm, HEAD_W), lambda ip, j: (j // 3, row(ip), 0)),
        pl.BlockSpec((None, tm, HEAD_W), lambda ip, j: (j // 3, row(ip), 0)),
        pl.BlockSpec((None, 2 * HEAD_W, 2 * HEAD_W), lambda ip, j: (j // 3, 0, 0)),
    ]
    out_specs = (
        pl.BlockSpec((N_HEADS, HEAD_W, tm), lambda ip, j: (0, 0, row(ip))),
        pl.BlockSpec((tm, SEG_W), lambda ip, j: (row(ip), 0)),
        pl.BlockSpec((N_HEADS, HEAD_W, tm), lambda ip, j: (0, 0, row(ip))),
        pl.BlockSpec((N_HEADS, HEAD_W, tm), lambda ip, j: (0, 0, row(ip))),
        pl.BlockSpec((tm, SEG_W), lambda ip, j: (ip, 0)),
        pl.BlockSpec((N_HEADS, V_ROWS, tm), lambda ip, j: (0, 0, ip)),
    )
    return pl.pallas_call(
        kern,
        out_shape=out_shape,
        grid=(nrb + 2 * pb, N_SEG),
        in_specs=in_specs,
        out_specs=out_specs,
        scratch_shapes=[pltpu.VMEM((tm, D_MODEL), BF16)],
        compiler_params=_cparams(("arbitrary", "arbitrary")),
        name="inproj",
    )(x, a_norm.reshape(1, D_MODEL).astype(F32), w_in_p, gains, cos_t, sin_t, gmat)


def _flash_step(s, m_old, acc_ref, vT_t):
    m_new = jnp.maximum(m_old, jnp.max(s, axis=0, keepdims=True))
    alpha = jnp.exp2(m_old - m_new)
    p = jnp.exp2((s - m_new).astype(BF16))
    acc_ref[...] = acc_ref[...] * alpha + jnp.dot(vT_t, p, preferred_element_type=F32)
    return m_new


def _flash_step_sum(s, m_old, l_old, acc_ref, vT_t):
    m_new = jnp.maximum(m_old, jnp.max(s, axis=0, keepdims=True))
    alpha = jnp.exp2(m_old - m_new)
    e = jnp.exp2(s - m_new)
    acc_ref[...] = acc_ref[...] * alpha + jnp.dot(vT_t, e.astype(BF16), preferred_element_type=F32)
    return m_new, l_old * alpha + jnp.sum(e, axis=0, keepdims=True)


def _head_out(aT, gain_row, out_scale):
    a = aT.T
    ms = jnp.mean(a * a, axis=-1, keepdims=True)
    y = a * lax.rsqrt(ms + EPS) * gain_row
    if out_scale != 1.0:
        y = y * out_scale
    return y


def _diff_attn_kernel(*refs, lam_init, tk, n_riders):
    lq1_ref, lk1_ref, lq2_ref, lk2_ref, qT_ref, k_ref, vT_ref, og_ref, gmat_ref = refs[:9]
    rider_in = refs[9:9 + n_riders]
    o_ref = refs[9 + n_riders]
    rider_out = refs[10 + n_riders:10 + 2 * n_riders]
    acc0, acc1, lsum, k0_scr, k1_scr, kmax_scr = refs[10 + 2 * n_riders:]
    tq = qT_ref.shape[1]
    seq = k_ref.shape[0]
    lane0, lane1 = MAP1_LANE, MAP0_LANE

    @pl.when(pl.program_id(1) == 0)
    def _():
        ck = min(1024, seq)

        def kchunk(c, mx):
            rows = pl.ds(pl.multiple_of(c * ck, ck), ck)
            kc = k_ref[rows, :]
            kf = kc.astype(F32)
            n2 = jnp.dot((kf * kf).astype(BF16), gmat_ref[...], preferred_element_type=F32)
            lane = lax.broadcasted_iota(jnp.int32, kc.shape, 1)
            one = jnp.ones_like(kc)
            k0_scr[rows, :] = jnp.where(lane == lane0, one, kc)
            k1_scr[rows, :] = jnp.where(lane == lane1, one, kc)
            return jnp.maximum(mx, jnp.max(n2, axis=0, keepdims=True))

        kn2 = lax.fori_loop(0, seq // ck, kchunk, jnp.zeros((1, HEAD_W), F32))
        kmax_scr[...] = jnp.sqrt(kn2)

    qT = qT_ref[...]
    row = lax.broadcasted_iota(jnp.int32, qT.shape, 0)
    in_map1 = ((row // 32) % 2) == 1
    zero = jnp.zeros_like(qT)
    qf = qT.astype(F32)
    q2 = qf * qf
    nq0 = jnp.sum(jnp.where(in_map1, 0.0, q2), axis=0, keepdims=True)
    nq1 = jnp.sum(jnp.where(in_map1, q2, 0.0), axis=0, keepdims=True)
    kmax = kmax_scr[...]
    b0 = jnp.sqrt(nq0) * kmax[:, MAP0_LANE:MAP0_LANE + 1]
    b1 = jnp.sqrt(nq1) * kmax[:, MAP1_LANE:MAP1_LANE + 1]
    q0 = jnp.where(in_map1, zero, qT)
    q1 = jnp.where(in_map1, qT, zero)
    q0s = jnp.where(row == lane0, jnp.broadcast_to(-b0, qf.shape).astype(BF16), q0)
    q1s = jnp.where(row == lane1, jnp.broadcast_to(-b1, qf.shape).astype(BF16), q1)
    acc0[...] = jnp.zeros(acc0.shape, F32)
    acc1[...] = jnp.zeros(acc1.shape, F32)

    def body(kt, carry):
        l0, l1 = carry
        rows = pl.ds(pl.multiple_of(kt * tk, tk), tk)
        vT_t = vT_ref[:, rows]
        e0 = jnp.exp2(jnp.dot(k0_scr[rows, :], q0s, preferred_element_type=F32))
        acc0[...] += jnp.dot(vT_t, e0.astype(BF16), preferred_element_type=F32)
        e1 = jnp.exp2(jnp.dot(k1_scr[rows, :], q1s, preferred_element_type=F32))
        acc1[...] += jnp.dot(vT_t, e1.astype(BF16), preferred_element_type=F32)
        for w_ref, wb_ref in zip(rider_in, rider_out):
            r = w_ref.shape[0] // (seq // tk)
            part = pl.ds(pl.multiple_of(kt * r, 16), r)
            wb_ref[part, :] = w_ref[part, :].astype(BF16)
        return l0 + jnp.sum(e0, axis=0, keepdims=True), l1 + jnp.sum(e1, axis=0, keepdims=True)

    l_init = jnp.zeros((1, tq), F32)
    l0, l1 = lax.fori_loop(0, seq // tk, body, (l_init, l_init))
    lsum[0:1, :] = l0
    lsum[1:2, :] = l1

    @pl.when(jnp.logical_not(jnp.min(jnp.minimum(l0, l1)) >= ROW_SUM_FLOOR))
    def _():
        acc0[...] = jnp.zeros(acc0.shape, F32)
        acc1[...] = jnp.zeros(acc1.shape, F32)

        def robust(kt, carry):
            m0, l0, m1, l1 = carry
            rows = pl.ds(pl.multiple_of(kt * tk, tk), tk)
            k_t = k_ref[rows, :]
            vT_t = vT_ref[:, rows]
            m0, l0 = _flash_step_sum(jnp.dot(k_t, q0, preferred_element_type=F32), m0, l0, acc0, vT_t)
            m1, l1 = _flash_step_sum(jnp.dot(k_t, q1, preferred_element_type=F32), m1, l1, acc1, vT_t)
            return m0, l0, m1, l1

        m_init = jnp.full((1, tq), NEG, F32)
        _, l0, _, l1 = lax.fori_loop(0, seq // tk, robust, (m_init, l_init, m_init, l_init))
        lsum[0:1, :] = l0
        lsum[1:2, :] = l1

    lam = (jnp.exp(jnp.sum(lq1_ref[...] * lk1_ref[...], axis=-1, keepdims=True))
           - jnp.exp(jnp.sum(lq2_ref[...] * lk2_ref[...], axis=-1, keepdims=True)) + lam_init)
    o0 = acc0[...] * (1.0 / lsum[0:1, :])
    o1 = acc1[...] * (1.0 / lsum[1:2, :])
    aT = o0 - lam * o1
    o_ref[...] = _head_out(aT, og_ref[...], 1.0 - lam_init).astype(o_ref.dtype)


def _diff_attn(qaT, ka, vaT, lq1, lk1, lq2, lk2, og, gmat_diff, lam_init, riders=()):
    seq = ka.shape[0]
    tq = min(ATT_TQ, seq)
    tk = min(ATT_TK, seq)
    nq = seq // tq
    n_steps = N_HEADS * nq
    kern = functools.partial(_diff_attn_kernel, lam_init=lam_init, tk=tk, n_riders=len(riders))
    vec = lambda v: v.reshape(1, -1).astype(F32)
    small = pl.BlockSpec((1, DIFF_DIM), lambda h, qi: (0, 0))
    once = pl.Buffered(1)
    rider_specs = []
    for w in riders:
        n_e, n_r, n_c = w.shape
        per_e = n_steps // n_e
        rider_specs.append(pl.BlockSpec(
            (None, n_r // per_e, n_c),
            lambda h, qi, per_e=per_e: ((h * nq + qi) // per_e, (h * nq + qi) % per_e, 0)))
    outs = pl.pallas_call(
        kern,
        out_shape=(jax.ShapeDtypeStruct((seq, SEG_W), BF16),
                   *[jax.ShapeDtypeStruct(w.shape, BF16) for w in riders]),
        grid=(N_HEADS, nq),
        in_specs=[small, small, small, small,
                  pl.BlockSpec((None, HEAD_W, tq), lambda h, qi: (h, 0, qi)),
                  pl.BlockSpec((seq, HEAD_W), lambda h, qi: (0, h), pipeline_mode=once),
                  pl.BlockSpec((None, HEAD_W, seq), lambda h, qi: (h, 0, 0), pipeline_mode=once),
                  pl.BlockSpec((1, HEAD_W), lambda h, qi: (0, 0)),
                  pl.BlockSpec((HEAD_W, HEAD_W), lambda h, qi: (0, 0)),
                  *rider_specs],
        out_specs=(pl.BlockSpec((tq, HEAD_W), lambda h, qi: (qi, h)), *rider_specs),
        scratch_shapes=[pltpu.VMEM((HEAD_W, tq), F32), pltpu.VMEM((HEAD_W, tq), F32),
                        pltpu.VMEM((8, tq), F32),
                        pltpu.VMEM((seq, HEAD_W), BF16), pltpu.VMEM((seq, HEAD_W), BF16),
                        pltpu.VMEM((1, HEAD_W), F32)],
        compiler_params=_cparams(("arbitrary", "arbitrary")),
        name="diff_attn",
    )(vec(lq1), vec(lk1), vec(lq2), vec(lk2), qaT, ka, vaT, vec(og), gmat_diff, *riders)
    return outs[0], tuple(outs[1:])


def _dil_chunks(tq):
    chunks, off = [], 0
    for _, dil in DIL_BRANCHES:
        pad = -(-(N_SIDE * dil) // 128) * 128
        total = tq + 2 * pad
        c0 = 0
        while c0 < total:
            nk = min(DIL_CHUNK, total - c0)
            chunks.append((dil, c0 - pad, nk, off))
            off += nk
            c0 += nk
    return chunks, off


def _dil_bias(tq):
    chunks, total = _dil_chunks(tq)
    bias = np.full((total, tq), NEG, np.float32)
    col = np.arange(tq)[None, :]
    for dil, rel, nk, off in chunks:
        delta = rel + np.arange(nk)[:, None] - col
        ok = (np.abs(delta) <= N_SIDE * dil) & (delta % dil == 0)
        bias[off:off + nk][ok] = 0.0
    return jnp.asarray(bias)


def _dil_attn_kernel(qT_ref, k_ref, vT_ref, bias_ref, og_ref, gmat_ref, o_ref, acc, kmax_scr, *, seq, chunks):
    tq = acc.shape[2]
    n_sub = acc.shape[0]
    base = pl.program_id(1) * (tq * n_sub)

    @pl.when(pl.program_id(1) == 0)
    def _():
        ck = 1024

        def kchunk(c, mx):
            kf = k_ref[pl.ds(pl.multiple_of(c * ck, ck), ck), :].astype(F32)
            n2 = jnp.dot((kf * kf).astype(BF16), gmat_ref[...], preferred_element_type=F32)
            return jnp.maximum(mx, jnp.max(n2, axis=0, keepdims=True))

        kn2 = lax.fori_loop(0, k_ref.shape[0] // ck, kchunk, jnp.zeros((1, HEAD_W), F32))
        kmax_scr[...] = jnp.sqrt(kn2)

    def windows(i0):
        for dil, rel, nk, off in chunks:
            start = pl.multiple_of(i0 + (DIL_PAD + rel), 128)
            yield rel, nk, off, k_ref[pl.ds(start, nk), :], vT_ref[:, pl.ds(start, nk)]

    lmin = None
    for t in range(n_sub):
        qT = qT_ref[:, t * tq:(t + 1) * tq]
        qf = qT.astype(F32)
        shift = jnp.sqrt(jnp.sum(qf * qf, axis=0, keepdims=True)) * kmax_scr[:, 0:1]
        num = jnp.zeros((V_ROWS, tq), F32)
        for rel, nk, off, k_t, vT_t in windows(base + t * tq):
            s = jnp.dot(k_t, qT, preferred_element_type=F32) - shift + bias_ref[off:off + nk, :]
            num = num + jnp.dot(vT_t, jnp.exp2(s).astype(BF16), preferred_element_type=F32)
        acc[t] = num
        l_t = jnp.min(num[HEAD_W:HEAD_W + 1, :])
        lmin = l_t if lmin is None else jnp.minimum(lmin, l_t)

    @pl.when(jnp.logical_not(lmin >= ROW_SUM_FLOOR))
    def _():
        for t in range(n_sub):
            i0 = base + t * tq
            qT = qT_ref[:, t * tq:(t + 1) * tq]
            acc[t] = jnp.zeros((V_ROWS, tq), F32)
            m = jnp.full((1, tq), NEG, F32)
            for rel, nk, off, k_t, vT_t in windows(i0):
                s = jnp.dot(k_t, qT, preferred_element_type=F32)
                kpos = lax.broadcasted_iota(jnp.int32, (nk, tq), 0) + (i0 + rel)
                valid = jnp.logical_and(kpos >= 0, kpos < seq)
                s = jnp.where(valid, s + bias_ref[off:off + nk, :], NEG)
                m = _flash_step(s, m, acc.at[t], vT_t)

    for t in range(n_sub):
        oT = acc[t, 0:HEAD_W, :] * (1.0 / acc[t, HEAD_W:HEAD_W + 1, :])
        o_ref[t * tq:(t + 1) * tq, :] = _head_out(oT, og_ref[...], 1.0).astype(o_ref.dtype)


def _dil_attn(qbT, kb, vbT, og, gmat_ones):
    seq = qbT.shape[2]
    seq_p = kb.shape[0]
    tq = min(DIL_TQ, seq)
    n_sub = DIL_SUBTILES
    chunks, _ = _dil_chunks(tq)
    bias = _dil_bias(tq)
    kern = functools.partial(_dil_attn_kernel, seq=seq, chunks=chunks)
    return pl.pallas_call(
        kern,
        out_shape=jax.ShapeDtypeStruct((seq, SEG_W), BF16),
        grid=(N_HEADS, seq // (tq * n_sub)),
        in_specs=[pl.BlockSpec((None, HEAD_W, tq * n_sub), lambda h, qi: (h, 0, qi)),
                  pl.BlockSpec((seq_p, HEAD_W), lambda h, qi: (0, h)),
                  pl.BlockSpec((None, V_ROWS, seq_p), lambda h, qi: (h, 0, 0)),
                  pl.BlockSpec(bias.shape, lambda h, qi: (0, 0)),
                  pl.BlockSpec((1, HEAD_W), lambda h, qi: (0, 0)),
                  pl.BlockSpec((HEAD_W, HEAD_W), lambda h, qi: (0, 0))],
        out_specs=pl.BlockSpec((tq * n_sub, HEAD_W), lambda h, qi: (qi, h)),
        scratch_shapes=[pltpu.VMEM((n_sub, V_ROWS, tq), F32), pltpu.VMEM((1, HEAD_W), F32)],
        compiler_params=_cparams(("arbitrary", "arbitrary")),
        name="dil_attn",
    )(qbT, kb, vbT, bias, og.reshape(1, HEAD_W).astype(F32), gmat_ones)


def _outproj_kernel(*refs, with_router):
    if with_router:
        a_ref, b_ref, w_ref, x_ref, g_ref, r_hi_ref, r_lo_ref, x1_ref, h_ref, route_ref = refs
    else:
        a_ref, b_ref, w_ref, x_ref, g_ref, x1_ref, h_ref = refs
    acc = (jnp.dot(a_ref[...], w_ref[0:SEG_W, :], preferred_element_type=F32)
           + jnp.dot(b_ref[...], w_ref[SEG_W:2 * SEG_W, :], preferred_element_type=F32))
    x1 = x_ref[...] + acc
    x1_ref[...] = x1
    ms = jnp.mean(x1 * x1, axis=-1, keepdims=True)
    hn = x1 * lax.rsqrt(ms + EPS) * g_ref[...]
    if not with_router:
        h_ref[...] = hn.astype(BF16)
    else:
        tm = hn.shape[0]
        for s in range(ROW_CHUNKS):
            h_ref[pl.ds(s, tm, stride=ROW_CHUNKS), :] = hn[:, s * HEAD_W:(s + 1) * HEAD_W]
        h_hi = hn.astype(BF16)
        h_lo = (hn - h_hi.astype(F32)).astype(BF16)
        logits = (jnp.dot(h_hi, r_hi_ref[...], preferred_element_type=F32)
                  + jnp.dot(h_hi, r_lo_ref[...], preferred_element_type=F32)
                  + jnp.dot(h_lo, r_hi_ref[...], preferred_element_type=F32))
        lane = lax.broadcasted_iota(jnp.int32, logits.shape, 1)
        ninf = jnp.float32(-jnp.inf)
        lg = jnp.where(lane < N_EXPERTS, logits, ninf)
        v1 = jnp.max(lg, axis=-1, keepdims=True)
        i1 = jnp.min(jnp.where(lg == v1, lane, HEAD_W), axis=-1, keepdims=True)
        lg2 = jnp.where(lane == i1, ninf, lg)
        v2 = jnp.max(lg2, axis=-1, keepdims=True)
        i2 = jnp.min(jnp.where(lg2 == v2, lane, HEAD_W), axis=-1, keepdims=True)
        g1 = 1.0 / (1.0 + jnp.exp(v2 - v1))
        g2 = 1.0 - g1
        route_ref[...] = jnp.where(lane == 0, i1.astype(F32),
                                   jnp.where(lane == 1, i2.astype(F32),
                                             jnp.where(lane == 2, g1, jnp.where(lane == 3, g2, 0.0))))


def _outproj(out_a, out_b, w_out, x, f_norm, router=None):
    seq = x.shape[0]
    tm = ROW_TILE
    with_router = router is not None
    in_specs = [pl.BlockSpec((tm, SEG_W), lambda i: (i, 0)),
                pl.BlockSpec((tm, SEG_W), lambda i: (i, 0)),
                pl.BlockSpec((D_MODEL, D_MODEL), lambda i: (0, 0)),
                pl.BlockSpec((tm, D_MODEL), lambda i: (i, 0)),
                pl.BlockSpec((1, D_MODEL), lambda i: (0, 0))]
    args = [out_a, out_b, w_out.astype(BF16), x, f_norm.reshape(1, D_MODEL).astype(F32)]
    out_shape = [jax.ShapeDtypeStruct((seq, D_MODEL), F32)]
    out_specs = [pl.BlockSpec((tm, D_MODEL), lambda i: (i, 0))]
    if not with_router:
        out_shape.append(jax.ShapeDtypeStruct((seq, D_MODEL), BF16))
        out_specs.append(pl.BlockSpec((tm, D_MODEL), lambda i: (i, 0)))
    else:
        out_shape.append(jax.ShapeDtypeStruct((seq * ROW_CHUNKS, HEAD_W), F32))
        out_specs.append(pl.BlockSpec((tm * ROW_CHUNKS, HEAD_W), lambda i: (i, 0)))
        r = jnp.zeros((D_MODEL, HEAD_W), F32).at[:, :N_EXPERTS].set(router.astype(F32))
        r_hi = r.astype(BF16)
        r_lo = (r - r_hi.astype(F32)).astype(BF16)
        in_specs += [pl.BlockSpec((D_MODEL, HEAD_W), lambda i: (0, 0))] * 2
        args += [r_hi, r_lo]
        out_shape.append(jax.ShapeDtypeStruct((seq, HEAD_W), F32))
        out_specs.append(pl.BlockSpec((tm, HEAD_W), lambda i: (i, 0)))
    return pl.pallas_call(
        functools.partial(_outproj_kernel, with_router=with_router),
        out_shape=tuple(out_shape),
        grid=(seq // tm,),
        in_specs=in_specs,
        out_specs=tuple(out_specs),
        compiler_params=_cparams(("arbitrary",)),
        name="outproj",
    )(*args)


def _swiglu_part(h, wg, wu, wd):
    g = jnp.dot(h, wg, preferred_element_type=F32)
    u = jnp.dot(h, wu, preferred_element_type=F32)
    act = (g * (1.0 / (1.0 + jnp.exp(-g))) * u).astype(BF16)
    return jnp.dot(act, wd, preferred_element_type=F32)


def _dense_ffn_kernel(h_ref, x_ref, wg_ref, wu_ref, wd_ref, o_ref):
    @pl.when(pl.program_id(1) == 0)
    def _():
        o_ref[...] = x_ref[...]

    o_ref[...] += _swiglu_part(h_ref[...], wg_ref[...], wu_ref[...], wd_ref[...])


def _dense_ffn(h, x1, wg, wu, wd):
    seq = x1.shape[0]
    tm, tf = ROW_TILE, FF_TILE
    return pl.pallas_call(
        _dense_ffn_kernel,
        out_shape=jax.ShapeDtypeStruct((seq, D_MODEL), F32),
        grid=(seq // tm, D_FF // tf),
        in_specs=[pl.BlockSpec((tm, D_MODEL), lambda i, j: (i, 0)),
                  pl.BlockSpec((tm, D_MODEL), lambda i, j: (i, 0)),
                  pl.BlockSpec((D_MODEL, tf), lambda i, j: (0, j)),
                  pl.BlockSpec((D_MODEL, tf), lambda i, j: (0, j)),
                  pl.BlockSpec((tf, D_MODEL), lambda i, j: (j, 0))],
        out_specs=pl.BlockSpec((tm, D_MODEL), lambda i, j: (i, 0)),
        compiler_params=_cparams(("arbitrary", "arbitrary")),
        name="dense_ffn",
    )(h, x1, wg.astype(BF16), wu.astype(BF16), wd.astype(BF16))


def _moe_ffn_kernel(be_ref, nused_ref, src_ref, src_next_ref, dst_ref, dst_prev_ref,
                    h3_ref, wg_ref, wu_ref, wd_ref, y_ref, x3, xb, acc, o3, gsem, ssem):
    b = pl.program_id(0)
    j = pl.program_id(1)
    nb = pl.num_programs(0)
    nj = pl.num_programs(1)
    tm = xb.shape[0]
    n_used = nused_ref[0]
    used = b < n_used

    def slab(ref, row0):
        return ref.at[pl.ds(pl.multiple_of(row0, 8), ROW_CHUNKS), :]

    def start_gather(idx_ref, slot):
        def start(r, c):
            pltpu.make_async_copy(slab(h3_ref, idx_ref[0, r]), slab(x3.at[slot], r * SLAB_PITCH),
                                  gsem.at[slot]).start()
            return c

        lax.fori_loop(0, tm, start, 0)

    def scatter_copy(idx_ref, r):
        return pltpu.make_async_copy(slab(o3, r * SLAB_PITCH), slab(y_ref, idx_ref[0, r]), ssem)

    def wait_gather(slot):
        n_rows = tm * ROW_CHUNKS
        pltpu.make_async_copy(h3_ref.at[pl.ds(0, n_rows), :], x3.at[slot, pl.ds(0, n_rows), :],
                              gsem.at[slot]).wait()

    def wait_scatter():
        n_rows = tm * ROW_CHUNKS
        pltpu.make_async_copy(o3.at[pl.ds(0, n_rows), :], y_ref.at[pl.ds(0, n_rows), :], ssem).wait()

    def flush_scatter(idx_ref):
        def start(r, c):
            scatter_copy(idx_ref, r).start()
            return c

        lax.fori_loop(0, tm, start, 0)
        wait_scatter()

    @pl.when(jnp.logical_and(j == 0, jnp.logical_and(b == 0, used)))
    def _():
        start_gather(src_ref, 0)
        o3[...] = jnp.zeros(o3.shape, F32)

    @pl.when(jnp.logical_and(j == 0, jnp.logical_and(b >= 1, b == n_used)))
    def _():
        flush_scatter(dst_prev_ref)

    @pl.when(jnp.logical_and(j == 0, jnp.where(b == 0, used, b - 1 < n_used)))
    def _():
        wait_gather(b % 2)

    @pl.when(jnp.logical_and(j == 0, used))
    def _():
        slot = b % 2
        for s in range(ROW_CHUNKS):
            xb[:, s * HEAD_W:(s + 1) * HEAD_W] = x3[slot, pl.ds(s, tm, stride=SLAB_PITCH), :].astype(BF16)
        acc[...] = jnp.zeros(acc.shape, F32)

    @pl.when(used)
    def _():
        per_step = tm // D_EXPERT_STEPS
        nslot = (b + 1) % 2

        def move(r):
            pltpu.make_async_copy(slab(h3_ref, src_next_ref[0, r]), slab(x3.at[nslot], r * SLAB_PITCH),
                                  gsem.at[nslot]).start()
            scatter_copy(dst_prev_ref, r).start()

        for i in range(per_step):
            move(j * per_step + i)

        @pl.when(j == 0)
        def _():
            for r in range(per_step * D_EXPERT_STEPS, tm):
                move(r)

        acc[...] += _swiglu_part(xb[...], wg_ref[...], wu_ref[...], wd_ref[...])

    @pl.when(jnp.logical_and(j == nj - 1, used))
    def _():
        wait_scatter()
        for s in range(ROW_CHUNKS):
            o3[pl.ds(s, tm, stride=SLAB_PITCH), :] = acc[:, s * HEAD_W:(s + 1) * HEAD_W]

        @pl.when(b == nb - 1)
        def _():
            flush_scatter(dst_ref)
            wait_gather((b + 1) % 2)


def _moe_ffn(h3, src_idx, dst_idx, dst_prev_idx, blk_expert, n_used, wg, wu, wd):
    seq = h3.shape[0] // ROW_CHUNKS
    n_blocks = src_idx.shape[0]
    tm, tf = MOE_TILE, MOE_FF_TILE
    nj = D_EXPERT // tf

    def jj(b, j, be, nu):
        return jnp.where(b < nu[0], j, nj - 1)

    def idx_spec(shift):
        return pl.BlockSpec((None, 1, tm), lambda b, j, be, nu: (jnp.clip(b + shift, 0, n_blocks - 1), 0, 0),
                            memory_space=pltpu.SMEM)

    return pl.pallas_call(
        _moe_ffn_kernel,
        out_shape=jax.ShapeDtypeStruct(((2 * seq + tm) * ROW_CHUNKS, HEAD_W), F32),
        grid_spec=pltpu.PrefetchScalarGridSpec(
            num_scalar_prefetch=2,
            grid=(n_blocks, nj),
            in_specs=[idx_spec(0), idx_spec(1), idx_spec(0), idx_spec(0),
                      pl.BlockSpec(memory_space=pl.ANY),
                      pl.BlockSpec((None, D_MODEL, tf), lambda b, j, be, nu: (be[b], 0, jj(b, j, be, nu))),
                      pl.BlockSpec((None, D_MODEL, tf), lambda b, j, be, nu: (be[b], 0, jj(b, j, be, nu))),
                      pl.BlockSpec((None, tf, D_MODEL), lambda b, j, be, nu: (be[b], jj(b, j, be, nu), 0))],
            out_specs=pl.BlockSpec(memory_space=pl.ANY),
            scratch_shapes=[pltpu.VMEM((2, tm * SLAB_PITCH, HEAD_W), F32),
                            pltpu.VMEM((tm, D_MODEL), BF16),
                            pltpu.VMEM((tm, D_MODEL), F32),
                            pltpu.VMEM((tm * SLAB_PITCH, HEAD_W), F32),
                            pltpu.SemaphoreType.DMA((2,)),
                            pltpu.SemaphoreType.DMA(())]),
        compiler_params=_cparams(("arbitrary", "arbitrary")),
        name="moe_ffn",
    )(blk_expert, n_used, src_idx, src_idx, dst_idx, dst_prev_idx, h3, wg, wu, wd)


def _combine_kernel(x_ref, ya_ref, yb_ref, route_ref, o_ref):
    r = route_ref[...]
    g1 = jnp.broadcast_to(r[:, 2:3], (r.shape[0], HEAD_W))
    g2 = jnp.broadcast_to(r[:, 3:4], (r.shape[0], HEAD_W))
    tm = r.shape[0]
    for s in range(ROW_CHUNKS):
        cols = slice(s * HEAD_W, (s + 1) * HEAD_W)
        rows = pl.ds(s, tm, stride=ROW_CHUNKS)
        o_ref[:, cols] = x_ref[:, cols] + g1 * ya_ref[rows, :] + g2 * yb_ref[rows, :]


def _moe_combine(x1, y, route):
    seq = x1.shape[0]
    tm = ROW_TILE
    nb = seq // tm
    big = pl.BlockSpec((tm, D_MODEL), lambda i: (i, 0))
    return pl.pallas_call(
        _combine_kernel,
        out_shape=jax.ShapeDtypeStruct((seq, D_MODEL), F32),
        grid=(nb,),
        in_specs=[big,
                  pl.BlockSpec((tm * ROW_CHUNKS, HEAD_W), lambda i: (i, 0)),
                  pl.BlockSpec((tm * ROW_CHUNKS, HEAD_W), lambda i: (i + nb, 0)),
                  pl.BlockSpec((tm, HEAD_W), lambda i: (i, 0))],
        out_specs=big,
        compiler_params=_cparams(("arbitrary",)),
        name="moe_combine",
    )(x1, y, y, route)


def _moe_plan(route, seq):
    tm = MOE_TILE
    n_assign = 2 * seq
    n_blocks = n_assign // tm + N_EXPERTS
    e = route[:, 0:2].astype(jnp.int32).reshape(-1)
    onehot = (e[:, None] == jnp.arange(N_EXPERTS, dtype=jnp.int32)[None, :]).astype(jnp.int32)
    csum = jnp.cumsum(onehot, axis=0)
    counts = csum[-1]
    rank = jnp.sum((csum - onehot) * onehot, axis=1)
    nblk = (counts + tm - 1) // tm
    cum_blk = jnp.cumsum(nblk)
    pad_start = (cum_blk - nblk) * tm
    pos = pad_start[e] + rank
    assign = jnp.arange(n_assign, dtype=jnp.int32)
    slot_assign = jnp.full((n_blocks * tm,), -1, jnp.int32).at[pos].set(assign)
    real = slot_assign >= 0
    tok = jnp.maximum(slot_assign, 0) // 2
    in_block = jnp.arange(n_blocks * tm, dtype=jnp.int32) % tm
    src_idx = tok * ROW_CHUNKS
    dst_idx = jnp.where(real, (slot_assign % 2) * seq + tok, n_assign + in_block) * ROW_CHUNKS
    src_idx = src_idx.reshape(n_blocks, 1, tm)
    dst_idx = dst_idx.reshape(n_blocks, 1, tm)
    scratch_blk = ((n_assign + jnp.arange(tm, dtype=jnp.int32)) * ROW_CHUNKS).reshape(1, 1, tm)
    dst_prev_idx = jnp.concatenate([scratch_blk, dst_idx[:-1]], axis=0)
    n_used = cum_blk[-1].astype(jnp.int32)
    blk = jnp.minimum(jnp.arange(n_blocks, dtype=jnp.int32), n_used - 1)
    blk_expert = jnp.sum((cum_blk[None, :] <= blk[:, None]).astype(jnp.int32), axis=1)
    blk_expert = jnp.minimum(blk_expert, N_EXPERTS - 1)
    return src_idx, dst_idx, dst_prev_idx, blk_expert, n_used.reshape(1)


def _moe(h3, x1, route, wg, wu, wd):
    seq = x1.shape[0]
    src_idx, dst_idx, dst_prev_idx, blk_expert, n_used = _moe_plan(route, seq)
    y = _moe_ffn(h3, src_idx, dst_idx, dst_prev_idx, blk_expert, n_used, wg, wu, wd)
    return _moe_combine(x1, y, route)


def _layer(x, i, a_norm, w_in, dqn, dkn, lq1, lk1, lq2, lk2, don, bqn, bkn, bon, w_out, f_norm, ffn, riders=()):
    seq = x.shape[0]
    lam_init = 0.8 - 0.6 * math.exp(-0.3 * i)
    gains, gmat, cos_t, sin_t = _segment_tables(seq, dqn, dkn, bqn, bkn)
    qaT, ka, vaT, qbT, kb, vbT = _inproj(x, a_norm, _permute_w_in(w_in), gains, gmat, cos_t, sin_t)
    out_a, casted = _diff_attn(qaT, ka, vaT, lq1, lk1, lq2, lk2, don, gmat[0, :HEAD_W, :HEAD_W], lam_init, riders)
    out_b = _dil_attn(qbT, kb, vbT, bon, gmat[1, :HEAD_W, :HEAD_W])
    if len(ffn) == 3:
        x1, h = _outproj(out_a, out_b, w_out, x, f_norm)
        return _dense_ffn(h, x1, *ffn), casted
    router, wg, wu, wd = ffn
    x1, h3, route = _outproj(out_a, out_b, w_out, x, f_norm, router)
    return _moe(h3, x1, route, wg, wu, wd), casted


def kernel(x, attn_norm_0, w_in_0, diff_q_norm_0, diff_k_norm_0, diff_lam_q1_0, diff_lam_k1_0, diff_lam_q2_0, diff_lam_k2_0, diff_out_norm_0, dil_q_norm_0, dil_k_norm_0, dil_out_norm_0, w_out_0, ffn_norm_0, ffn_w_gate_0, ffn_w_up_0, ffn_w_down_0, attn_norm_1, w_in_1, diff_q_norm_1, diff_k_norm_1, diff_lam_q1_1, diff_lam_k1_1, diff_lam_q2_1, diff_lam_k2_1, diff_out_norm_1, dil_q_norm_1, dil_k_norm_1, dil_out_norm_1, w_out_1, ffn_norm_1, router_1, moe_w_gate_1, moe_w_up_1, moe_w_down_1):
    b, seq, d = x.shape
    assert b == 1 and d == D_MODEL and seq % (DIL_TQ * DIL_SUBTILES) == 0 and seq % ATT_TQ == 0
    assert MOE_TILE == ROW_TILE and DIL_PAD % ROW_TILE == 0
    xs = x.reshape(seq, d)
    xs, moe_w = _layer(xs, 0, attn_norm_0, w_in_0, diff_q_norm_0, diff_k_norm_0, diff_lam_q1_0, diff_lam_k1_0,
                       diff_lam_q2_0, diff_lam_k2_0, diff_out_norm_0, dil_q_norm_0, dil_k_norm_0, dil_out_norm_0,
                       w_out_0, ffn_norm_0, (ffn_w_gate_0, ffn_w_up_0, ffn_w_down_0),
                       riders=(moe_w_gate_1, moe_w_up_1, moe_w_down_1))
    xs, _ = _layer(xs, 1, attn_norm_1, w_in_1, diff_q_norm_1, diff_k_norm_1, diff_lam_q1_1, diff_lam_k1_1,
                   diff_lam_q2_1, diff_lam_k2_1, diff_out_norm_1, dil_q_norm_1, dil_k_norm_1, dil_out_norm_1,
                   w_out_1, ffn_norm_1, (router_1, *moe_w))
    return xs.reshape(b, seq, d)
```

```python
import functools
import math

import numpy as np
import jax
import jax.numpy as jnp
from jax import lax
from jax.experimental import pallas as pl
from jax.experimental.pallas import tpu as pltpu

F32 = jnp.float32
BF16 = jnp.bfloat16

D_MODEL = 2048
N_HEADS = 8
HEAD_W = 128
DIFF_DIM = 64
SEG_W = N_HEADS * HEAD_W
N_SEG = 6
ROW_CHUNKS = D_MODEL // HEAD_W
SLAB_PITCH = 24
DIL_BRANCHES = ((128, 1), (512, 4), (2048, 16))
N_SIDE = 64
D_FF = 5632
N_EXPERTS = 8
D_EXPERT = 7168
EPS = 1e-6
NEG = -1e30
LOG2E = 1.4426950408889634

MAP0_LANE = 0
MAP1_LANE = 32
ROW_SUM_FLOOR = 2.0 ** -80

V_ROWS = 144
DIL_PAD = 1024
V7X_VMEM_BYTES = 64 * 1024 * 1024
VMEM_LIMIT = V7X_VMEM_BYTES * 7 // 8

ROW_TILE = 512
ATT_TQ = 512
ATT_TK = 8192
DIL_TQ = 256
DIL_SUBTILES = 4
DIL_CHUNK = 768
FF_TILE = 512
D_EXPERT_STEPS = 7
MOE_FF_TILE = D_EXPERT // D_EXPERT_STEPS
MOE_TILE = 512


def _cparams(sem):
    return pltpu.CompilerParams(dimension_semantics=sem, vmem_limit_bytes=VMEM_LIMIT)


def _diff_lane_perm():
    perm = np.zeros(HEAD_W, np.int32)
    for m in range(2):
        for t in range(DIFF_DIM):
            p = (t // 32) * 64 + m * 32 + (t % 32)
            perm[p] = m * DIFF_DIM + t
    return perm


def _segment_tables(seq, dqn, dkn, bqn, bkn):
    perm = _diff_lane_perm()
    t_of_lane = perm % DIFF_DIM
    ones = jnp.ones((SEG_W,), F32)
    g_qa = jnp.tile(dqn.astype(F32)[t_of_lane], N_HEADS)
    g_ka = jnp.tile(dkn.astype(F32)[t_of_lane], N_HEADS)
    g_qb = jnp.tile(bqn.astype(F32), N_HEADS)
    g_kb = jnp.tile(bkn.astype(F32), N_HEADS)
    gains = jnp.stack([g_qa, g_ka, ones, g_qb, g_kb, ones]).reshape(N_SEG, 1, SEG_W)

    lane = np.arange(HEAD_W)
    map_of_lane = (lane // 32) % 2
    g_diff = (map_of_lane[:, None] == map_of_lane[None, :]).astype(np.float32)
    g_dil = np.ones((HEAD_W, HEAD_W), np.float32)
    pair = np.eye(2, dtype=np.float32)
    gmat = jnp.asarray(np.stack([np.kron(pair, g_diff), np.kron(pair, g_dil)]), BF16)

    pos = jnp.arange(seq, dtype=F32)

    def ang(dim):
        inv = 10000.0 ** (-jnp.arange(0, dim, 2, dtype=F32) / dim)
        return pos[:, None] * inv[None, :]

    a32 = ang(DIFF_DIM)
    a64 = ang(HEAD_W)
    cos_a = jnp.tile(jnp.cos(a32), (1, 4))
    sin_a = jnp.tile(jnp.sin(a32), (1, 4))
    cos_b = jnp.tile(jnp.cos(a64), (1, 2))
    sin_b = jnp.tile(jnp.sin(a64), (1, 2))
    sign = jnp.where(jnp.arange(HEAD_W) < 64, -1.0, 1.0).astype(F32)[None, :]
    cos_t = jnp.stack([cos_a, cos_b])
    sin_t = jnp.stack([sin_a * sign, sin_b * sign])
    return gains, gmat, cos_t, sin_t


def _permute_w_in(w_in):
    w = w_in.astype(BF16)
    qk = w[:, :2 * SEG_W].reshape(D_MODEL, 2 * N_HEADS, 2, 2, 32)
    qk = qk.transpose(0, 1, 3, 2, 4).reshape(D_MODEL, 2 * SEG_W)
    return jnp.concatenate([qk, w[:, 2 * SEG_W:]], axis=1)


def _inproj_kernel(x_ref, g_ref, w_ref, gain_ref, cos_ref, sin_ref, gmat_ref,
                   qaT_ref, ka_ref, vaT_ref, qbT_ref, kb_ref, vbT_ref, h_scr,
                   *, n_row_blocks, pad_blocks):
    ip = pl.program_id(0)
    j = pl.program_id(1)
    real = jnp.logical_and(ip >= pad_blocks, ip < pad_blocks + n_row_blocks)
    tm = x_ref.shape[0]

    @pl.when(jnp.logical_and(real, j == 0))
    def _():
        x = x_ref[...]
        ms = jnp.mean(x * x, axis=-1, keepdims=True)
        h_scr[...] = (x * lax.rsqrt(ms + EPS) * g_ref[...]).astype(BF16)

    def norm_rope(y, ss, c, n_group, scale):
        yn = y * lax.rsqrt(ss * (1.0 / n_group) + EPS) * gain_ref[:, c * HEAD_W:(c + 1) * HEAD_W]
        out = yn * cos_ref[...] + pltpu.roll(yn, 64, 1) * sin_ref[...]
        if scale != 1.0:
            out = out * scale
        return out

    def aug_rows():
        row = lax.broadcasted_iota(jnp.int32, (V_ROWS - HEAD_W, tm), 0)
        return jnp.where(row == 0, 1.0, 0.0).astype(BF16)

    def segment(seg):
        acc = jnp.dot(h_scr[...], w_ref[...], preferred_element_type=F32)
        for c in range(N_HEADS):
            y = acc[:, c * HEAD_W:(c + 1) * HEAD_W]
            if seg in (0, 1, 3, 4) and c % 2 == 0:
                pair = acc[:, c * HEAD_W:(c + 2) * HEAD_W]
                ss2 = jnp.dot((pair * pair).astype(BF16), gmat_ref[...], preferred_element_type=F32)
            if seg in (0, 1, 3, 4):
                ss = ss2[:, (c % 2) * HEAD_W:(c % 2 + 1) * HEAD_W]
            if seg == 0:
                qaT_ref[c] = norm_rope(y, ss, c, DIFF_DIM, DIFF_DIM ** -0.5 * LOG2E).T.astype(BF16)
            elif seg == 1:
                ka_ref[:, c * HEAD_W:(c + 1) * HEAD_W] = norm_rope(y, ss, c, DIFF_DIM, 1.0).astype(BF16)
            elif seg == 3:
                qbT_ref[c] = norm_rope(y, ss, c, HEAD_W, HEAD_W ** -0.5 * LOG2E).T.astype(BF16)
            elif seg == 4:
                kb_ref[:, c * HEAD_W:(c + 1) * HEAD_W] = norm_rope(y, ss, c, HEAD_W, 1.0).astype(BF16)
            elif seg == 2:
                vaT_ref[c] = y.T.astype(BF16)
            else:
                vbT_ref[c, 0:HEAD_W, :] = y.T.astype(BF16)
                vbT_ref[c, HEAD_W:V_ROWS, :] = aug_rows()

    for seg in range(N_SEG):
        pl.when(jnp.logical_and(real, j == seg))(functools.partial(segment, seg))

    @pl.when(jnp.logical_and(jnp.logical_not(real), j == 4))
    def _():
        kb_ref[...] = jnp.zeros(kb_ref.shape, BF16)

    @pl.when(jnp.logical_and(jnp.logical_not(real), j == 5))
    def _():
        vbT_ref[...] = jnp.zeros(vbT_ref.shape, BF16)


def _inproj(x, a_norm, w_in_p, gains, gmat, cos_t, sin_t):
    seq = x.shape[0]
    tm = ROW_TILE
    nrb = seq // tm
    pb = DIL_PAD // tm
    seq_p = seq + 2 * DIL_PAD

    def row(ip):
        return jnp.clip(ip - pb, 0, nrb - 1)

    kern = functools.partial(_inproj_kernel, n_row_blocks=nrb, pad_blocks=pb)
    out_shape = (
        jax.ShapeDtypeStruct((N_HEADS, HEAD_W, seq), BF16),
        jax.ShapeDtypeStruct((seq, SEG_W), BF16),
        jax.ShapeDtypeStruct((N_HEADS, HEAD_W, seq), BF16),
        jax.ShapeDtypeStruct((N_HEADS, HEAD_W, seq), BF16),
        jax.ShapeDtypeStruct((seq_p, SEG_W), BF16),
        jax.ShapeDtypeStruct((N_HEADS, V_ROWS, seq_p), BF16),
    )
    in_specs = [
        pl.BlockSpec((tm, D_MODEL), lambda ip, j: (row(ip), 0)),
        pl.BlockSpec((1, D_MODEL), lambda ip, j: (0, 0)),
        pl.BlockSpec((D_MODEL, SEG_W), lambda ip, j: (0, j)),
        pl.BlockSpec((None, 1, SEG_W), lambda ip, j: (j, 0, 0)),
        pl.BlockSpec((None, tm, HEAD_W), lambda ip, j: (j // 3, row(ip), 0)),
        pl.BlockSpec((None, tm, HEAD_W), lambda ip, j: (j // 3, row(ip), 0)),
        pl.BlockSpec((None, 2 * HEAD_W, 2 * HEAD_W), lambda ip, j: (j // 3, 0, 0)),
    ]
    out_specs = (
        pl.BlockSpec((N_HEADS, HEAD_W, tm), lambda ip, j: (0, 0, row(ip))),
        pl.BlockSpec((tm, SEG_W), lambda ip, j: (row(ip), 0)),
        pl.BlockSpec((N_HEADS, HEAD_W, tm), lambda ip, j: (0, 0, row(ip))),
        pl.BlockSpec((N_HEADS, HEAD_W, tm), lambda ip, j: (0, 0, row(ip))),
        pl.BlockSpec((tm, SEG_W), lambda ip, j: (ip, 0)),
        pl.BlockSpec((N_HEADS, V_ROWS, tm), lambda ip, j: (0, 0, ip)),
    )
    return pl.pallas_call(
        kern,
        out_shape=out_shape,
        grid=(nrb + 2 * pb, N_SEG),
        in_specs=in_specs,
        out_specs=out_specs,
        scratch_shapes=[pltpu.VMEM((tm, D_MODEL), BF16)],
        compiler_params=_cparams(("arbitrary", "arbitrary")),
        name="inproj",
    )(x, a_norm.reshape(1, D_MODEL).astype(F32), w_in_p, gains, cos_t, sin_t, gmat)


def _flash_step(s, m_old, acc_ref, vT_t):
    m_new = jnp.maximum(m_old, jnp.max(s, axis=0, keepdims=True))
    alpha = jnp.exp2(m_old - m_new)
    p = jnp.exp2((s - m_new).astype(BF16))
    acc_ref[...] = acc_ref[...] * alpha + jnp.dot(vT_t, p, preferred_element_type=F32)
    return m_new


def _flash_step_sum(s, m_old, l_old, acc_ref, vT_t):
    m_new = jnp.maximum(m_old, jnp.max(s, axis=0, keepdims=True))
    alpha = jnp.exp2(m_old - m_new)
    e = jnp.exp2(s - m_new)
    acc_ref[...] = acc_ref[...] * alpha + jnp.dot(vT_t, e.astype(BF16), preferred_element_type=F32)
    return m_new, l_old * alpha + jnp.sum(e, axis=0, keepdims=True)


def _head_out(aT, gain_row, out_scale):
    a = aT.T
    ms = jnp.mean(a * a, axis=-1, keepdims=True)
    y = a * lax.rsqrt(ms + EPS) * gain_row
    if out_scale != 1.0:
        y = y * out_scale
    return y


def _diff_attn_kernel(*refs, lam_init, tk, n_riders):
    lq1_ref, lk1_ref, lq2_ref, lk2_ref, qT_ref, k_ref, vT_ref, og_ref, gmat_ref = refs[:9]
    rider_in = refs[9:9 + n_riders]
    o_ref = refs[9 + n_riders]
    rider_out = refs[10 + n_riders:10 + 2 * n_riders]
    acc0, acc1, lsum, k0_scr, k1_scr, kmax_scr = refs[10 + 2 * n_riders:]
    tq = qT_ref.shape[1]
    seq = k_ref.shape[0]
    lane0, lane1 = MAP1_LANE, MAP0_LANE

    @pl.when(pl.program_id(1) == 0)
    def _():
        ck = min(1024, seq)

        def kchunk(c, mx):
            rows = pl.ds(pl.multiple_of(c * ck, ck), ck)
            kc = k_ref[rows, :]
            kf = kc.astype(F32)
            n2 = jnp.dot((kf * kf).astype(BF16), gmat_ref[...], preferred_element_type=F32)
            lane = lax.broadcasted_iota(jnp.int32, kc.shape, 1)
            one = jnp.ones_like(kc)
            k0_scr[rows, :] = jnp.where(lane == lane0, one, kc)
            k1_scr[rows, :] = jnp.where(lane == lane1, one, kc)
            return jnp.maximum(mx, jnp.max(n2, axis=0, keepdims=True))

        kn2 = lax.fori_loop(0, seq // ck, kchunk, jnp.zeros((1, HEAD_W), F32))
        kmax_scr[...] = jnp.sqrt(kn2)

    qT = qT_ref[...]
    row = lax.broadcasted_iota(jnp.int32, qT.shape, 0)
    in_map1 = ((row // 32) % 2) == 1
    zero = jnp.zeros_like(qT)
    qf = qT.astype(F32)
    q2 = qf * qf
    nq0 = jnp.sum(jnp.where(in_map1, 0.0, q2), axis=0, keepdims=True)
    nq1 = jnp.sum(jnp.where(in_map1, q2, 0.0), axis=0, keepdims=True)
    kmax = kmax_scr[...]
    b0 = jnp.sqrt(nq0) * kmax[:, MAP0_LANE:MAP0_LANE + 1]
    b1 = jnp.sqrt(nq1) * kmax[:, MAP1_LANE:MAP1_LANE + 1]
    q0 = jnp.where(in_map1, zero, qT)
    q1 = jnp.where(in_map1, qT, zero)
    q0s = jnp.where(row == lane0, jnp.broadcast_to(-b0, qf.shape).astype(BF16), q0)
    q1s = jnp.where(row == lane1, jnp.broadcast_to(-b1, qf.shape).astype(BF16), q1)
    acc0[...] = jnp.zeros(acc0.shape, F32)
    acc1[...] = jnp.zeros(acc1.shape, F32)

    def body(kt, carry):
        l0, l1 = carry
        rows = pl.ds(pl.multiple_of(kt * tk, tk), tk)
        vT_t = vT_ref[:, rows]
        e0 = jnp.exp2(jnp.dot(k0_scr[rows, :], q0s, preferred_element_type=F32))
        acc0[...] += jnp.dot(vT_t, e0.astype(BF16), preferred_element_type=F32)
        e1 = jnp.exp2(jnp.dot(k1_scr[rows, :], q1s, preferred_element_type=F32))
        acc1[...] += jnp.dot(vT_t, e1.astype(BF16), preferred_element_type=F32)
        for w_ref, wb_ref in zip(rider_in, rider_out):
            r = w_ref.shape[0] // (seq // tk)
            part = pl.ds(pl.multiple_of(kt * r, 16), r)
            wb_ref[part, :] = w_ref[part, :].astype(BF16)
        return l0 + jnp.sum(e0, axis=0, keepdims=True), l1 + jnp.sum(e1, axis=0, keepdims=True)

    l_init = jnp.zeros((1, tq), F32)
    l0, l1 = lax.fori_loop(0, seq // tk, body, (l_init, l_init))
    lsum[0:1, :] = l0
    lsum[1:2, :] = l1

    @pl.when(jnp.logical_not(jnp.min(jnp.minimum(l0, l1)) >= ROW_SUM_FLOOR))
    def _():
        acc0[...] = jnp.zeros(acc0.shape, F32)
        acc1[...] = jnp.zeros(acc1.shape, F32)

        def robust(kt, carry):
            m0, l0, m1, l1 = carry
            rows = pl.ds(pl.multiple_of(kt * tk, tk), tk)
            k_t = k_ref[rows, :]
            vT_t = vT_ref[:, rows]
            m0, l0 = _flash_step_sum(jnp.dot(k_t, q0, preferred_element_type=F32), m0, l0, acc0, vT_t)
            m1, l1 = _flash_step_sum(jnp.dot(k_t, q1, preferred_element_type=F32), m1, l1, acc1, vT_t)
            return m0, l0, m1, l1

        m_init = jnp.full((1, tq), NEG, F32)
        _, l0, _, l1 = lax.fori_loop(0, seq // tk, robust, (m_init, l_init, m_init, l_init))
        lsum[0:1, :] = l0
        lsum[1:2, :] = l1

    lam = (jnp.exp(jnp.sum(lq1_ref[...] * lk1_ref[...], axis=-1, keepdims=True))
           - jnp.exp(jnp.sum(lq2_ref[...] * lk2_ref[...], axis=-1, keepdims=True)) + lam_init)
    o0 = acc0[...] * (1.0 / lsum[0:1, :])
    o1 = acc1[...] * (1.0 / lsum[1:2, :])
    aT = o0 - lam * o1
    o_ref[...] = _head_out(aT, og_ref[...], 1.0 - lam_init).astype(o_ref.dtype)


def _diff_attn(qaT, ka, vaT, lq1, lk1, lq2, lk2, og, gmat_diff, lam_init, riders=()):
    seq = ka.shape[0]
    tq = min(ATT_TQ, seq)
    tk = min(ATT_TK, seq)
    nq = seq // tq
    n_steps = N_HEADS * nq
    kern = functools.partial(_diff_attn_kernel, lam_init=lam_init, tk=tk, n_riders=len(riders))
    vec = lambda v: v.reshape(1, -1).astype(F32)
    small = pl.BlockSpec((1, DIFF_DIM), lambda h, qi: (0, 0))
    once = pl.Buffered(1)
    rider_specs = []
    for w in riders:
        n_e, n_r, n_c = w.shape
        per_e = n_steps // n_e
        rider_specs.append(pl.BlockSpec(
            (None, n_r // per_e, n_c),
            lambda h, qi, per_e=per_e: ((h * nq + qi) // per_e, (h * nq + qi) % per_e, 0)))
    outs = pl.pallas_call(
        kern,
        out_shape=(jax.ShapeDtypeStruct((seq, SEG_W), BF16),
                   *[jax.ShapeDtypeStruct(w.shape, BF16) for w in riders]),
        grid=(N_HEADS, nq),
        in_specs=[small, small, small, small,
                  pl.BlockSpec((None, HEAD_W, tq), lambda h, qi: (h, 0, qi)),
                  pl.BlockSpec((seq, HEAD_W), lambda h, qi: (0, h), pipeline_mode=once),
                  pl.BlockSpec((None, HEAD_W, seq), lambda h, qi: (h, 0, 0), pipeline_mode=once),
                  pl.BlockSpec((1, HEAD_W), lambda h, qi: (0, 0)),
                  pl.BlockSpec((HEAD_W, HEAD_W), lambda h, qi: (0, 0)),
                  *rider_specs],
        out_specs=(pl.BlockSpec((tq, HEAD_W), lambda h, qi: (qi, h)), *rider_specs),
        scratch_shapes=[pltpu.VMEM((HEAD_W, tq), F32), pltpu.VMEM((HEAD_W, tq), F32),
                        pltpu.VMEM((8, tq), F32),
                        pltpu.VMEM((seq, HEAD_W), BF16), pltpu.VMEM((seq, HEAD_W), BF16),
                        pltpu.VMEM((1, HEAD_W), F32)],
        compiler_params=_cparams(("arbitrary", "arbitrary")),
        name="diff_attn",
    )(vec(lq1), vec(lk1), vec(lq2), vec(lk2), qaT, ka, vaT, vec(og), gmat_diff, *riders)
    return outs[0], tuple(outs[1:])


def _dil_chunks(tq):
    chunks, off = [], 0
    for _, dil in DIL_BRANCHES:
        pad = -(-(N_SIDE * dil) // 128) * 128
        total = tq + 2 * pad
        c0 = 0
        while c0 < total:
            nk = min(DIL_CHUNK, total - c0)
            chunks.append((dil, c0 - pad, nk, off))
            off += nk
            c0 += nk
    return chunks, off


def _dil_bias(tq):
    chunks, total = _dil_chunks(tq)
    bias = np.full((total, tq), NEG, np.float32)
    col = np.arange(tq)[None, :]
    for dil, rel, nk, off in chunks:
        delta = rel + np.arange(nk)[:, None] - col
        ok = (np.abs(delta) <= N_SIDE * dil) & (delta % dil == 0)
        bias[off:off + nk][ok] = 0.0
    return jnp.asarray(bias)


def _dil_attn_kernel(qT_ref, k_ref, vT_ref, bias_ref, og_ref, gmat_ref, o_ref, acc, kmax_scr, *, seq, chunks):
    tq = acc.shape[1]
    n_sub = qT_ref.shape[1] // tq
    base = pl.program_id(1) * (tq * n_sub)

    @pl.when(pl.program_id(1) == 0)
    def _():
        ck = 1024

        def kchunk(c, mx):
            kf = k_ref[pl.ds(pl.multiple_of(c * ck, ck), ck), :].astype(F32)
            n2 = jnp.dot((kf * kf).astype(BF16), gmat_ref[...], preferred_element_type=F32)
            return jnp.maximum(mx, jnp.max(n2, axis=0, keepdims=True))

        kn2 = lax.fori_loop(0, k_ref.shape[0] // ck, kchunk, jnp.zeros((1, HEAD_W), F32))
        kmax_scr[...] = jnp.sqrt(kn2)

    def windows(i0):
        for dil, rel, nk, off in chunks:
            start = pl.multiple_of(i0 + (DIL_PAD + rel), 128)
            yield rel, nk, off, k_ref[pl.ds(start, nk), :], vT_ref[:, pl.ds(start, nk)]

    def finish(t, num):
        oT = num[0:HEAD_W, :] * (1.0 / num[HEAD_W:HEAD_W + 1, :])
        o_ref[t * tq:(t + 1) * tq, :] = _head_out(oT, og_ref[...], 1.0).astype(o_ref.dtype)

    lmin = None
    for t in range(n_sub):
        qT = qT_ref[:, t * tq:(t + 1) * tq]
        qf = qT.astype(F32)
        shift = jnp.sqrt(jnp.sum(qf * qf, axis=0, keepdims=True)) * kmax_scr[:, 0:1]
        num = jnp.zeros((V_ROWS, tq), F32)
        for rel, nk, off, k_t, vT_t in windows(base + t * tq):
            s = jnp.dot(k_t, qT, preferred_element_type=F32) - shift + bias_ref[off:off + nk, :]
            num = num + jnp.dot(vT_t, jnp.exp2(s).astype(BF16), preferred_element_type=F32)
        finish(t, num)
        l_t = jnp.min(num[HEAD_W:HEAD_W + 1, :])
        lmin = l_t if lmin is None else jnp.minimum(lmin, l_t)

    @pl.when(jnp.logical_not(lmin >= ROW_SUM_FLOOR))
    def _():
        for t in range(n_sub):
            i0 = base + t * tq
            qT = qT_ref[:, t * tq:(t + 1) * tq]
            acc[...] = jnp.zeros(acc.shape, F32)
            m = jnp.full((1, tq), NEG, F32)
            for rel, nk, off, k_t, vT_t in windows(i0):
                s = jnp.dot(k_t, qT, preferred_element_type=F32)
                kpos = lax.broadcasted_iota(jnp.int32, (nk, tq), 0) + (i0 + rel)
                valid = jnp.logical_and(kpos >= 0, kpos < seq)
                s = jnp.where(valid, s + bias_ref[off:off + nk, :], NEG)
                m = _flash_step(s, m, acc, vT_t)
            finish(t, acc[...])


def _dil_attn(qbT, kb, vbT, og, gmat_ones):
    seq = qbT.shape[2]
    seq_p = kb.shape[0]
    tq = min(DIL_TQ, seq)
    n_sub = DIL_SUBTILES
    chunks, _ = _dil_chunks(tq)
    bias = _dil_bias(tq)
    kern = functools.partial(_dil_attn_kernel, seq=seq, chunks=chunks)
    return pl.pallas_call(
        kern,
        out_shape=jax.ShapeDtypeStruct((seq, SEG_W), BF16),
        grid=(N_HEADS, seq // (tq * n_sub)),
        in_specs=[pl.BlockSpec((None, HEAD_W, tq * n_sub), lambda h, qi: (h, 0, qi)),
                  pl.BlockSpec((seq_p, HEAD_W), lambda h, qi: (0, h)),
                  pl.BlockSpec((None, V_ROWS, seq_p), lambda h, qi: (h, 0, 0)),
                  pl.BlockSpec(bias.shape, lambda h, qi: (0, 0)),
                  pl.BlockSpec((1, HEAD_W), lambda h, qi: (0, 0)),
                  pl.BlockSpec((HEAD_W, HEAD_W), lambda h, qi: (0, 0))],
        out_specs=pl.BlockSpec((tq * n_sub, HEAD_W), lambda h, qi: (qi, h)),
        scratch_shapes=[pltpu.VMEM((V_ROWS, tq), F32), pltpu.VMEM((1, HEAD_W), F32)],
        compiler_params=_cparams(("arbitrary", "arbitrary")),
        name="dil_attn",
    )(qbT, kb, vbT, bias, og.reshape(1, HEAD_W).astype(F32), gmat_ones)


def _outproj_kernel(*refs, with_router):
    if with_router:
        a_ref, b_ref, w_ref, x_ref, g_ref, r_hi_ref, r_lo_ref, x1_ref, h_ref, route_ref = refs
    else:
        a_ref, b_ref, w_ref, x_ref, g_ref, x1_ref, h_ref = refs
    acc = (jnp.dot(a_ref[...], w_ref[0:SEG_W, :], preferred_element_type=F32)
           + jnp.dot(b_ref[...], w_ref[SEG_W:2 * SEG_W, :], preferred_element_type=F32))
    x1 = x_ref[...] + acc
    x1_ref[...] = x1
    ms = jnp.mean(x1 * x1, axis=-1, keepdims=True)
    hn = x1 * lax.rsqrt(ms + EPS) * g_ref[...]
    if not with_router:
        h_ref[...] = hn.astype(BF16)
    else:
        tm = hn.shape[0]
        for s in range(ROW_CHUNKS):
            h_ref[pl.ds(s, tm, stride=ROW_CHUNKS), :] = hn[:, s * HEAD_W:(s + 1) * HEAD_W]
        h_hi = hn.astype(BF16)
        h_lo = (hn - h_hi.astype(F32)).astype(BF16)
        logits = (jnp.dot(h_hi, r_hi_ref[...], preferred_element_type=F32)
                  + jnp.dot(h_hi, r_lo_ref[...], preferred_element_type=F32)
                  + jnp.dot(h_lo, r_hi_ref[...], preferred_element_type=F32))
        lane = lax.broadcasted_iota(jnp.int32, logits.shape, 1)
        ninf = jnp.float32(-jnp.inf)
        lg = jnp.where(lane < N_EXPERTS, logits, ninf)
        v1 = jnp.max(lg, axis=-1, keepdims=True)
        i1 = jnp.min(jnp.where(lg == v1, lane, HEAD_W), axis=-1, keepdims=True)
        lg2 = jnp.where(lane == i1, ninf, lg)
        v2 = jnp.max(lg2, axis=-1, keepdims=True)
        i2 = jnp.min(jnp.where(lg2 == v2, lane, HEAD_W), axis=-1, keepdims=True)
        g1 = 1.0 / (1.0 + jnp.exp(v2 - v1))
        g2 = 1.0 - g1
        route_ref[...] = jnp.where(lane == 0, i1.astype(F32),
                                   jnp.where(lane == 1, i2.astype(F32),
                                             jnp.where(lane == 2, g1, jnp.where(lane == 3, g2, 0.0))))


def _outproj(out_a, out_b, w_out, x, f_norm, router=None):
    seq = x.shape[0]
    tm = ROW_TILE
    with_router = router is not None
    in_specs = [pl.BlockSpec((tm, SEG_W), lambda i: (i, 0)),
                pl.BlockSpec((tm, SEG_W), lambda i: (i, 0)),
                pl.BlockSpec((D_MODEL, D_MODEL), lambda i: (0, 0)),
                pl.BlockSpec((tm, D_MODEL), lambda i: (i, 0)),
                pl.BlockSpec((1, D_MODEL), lambda i: (0, 0))]
    args = [out_a, out_b, w_out.astype(BF16), x, f_norm.reshape(1, D_MODEL).astype(F32)]
    out_shape = [jax.ShapeDtypeStruct((seq, D_MODEL), F32)]
    out_specs = [pl.BlockSpec((tm, D_MODEL), lambda i: (i, 0))]
    if not with_router:
        out_shape.append(jax.ShapeDtypeStruct((seq, D_MODEL), BF16))
        out_specs.append(pl.BlockSpec((tm, D_MODEL), lambda i: (i, 0)))
    else:
        out_shape.append(jax.ShapeDtypeStruct((seq * ROW_CHUNKS, HEAD_W), F32))
        out_specs.append(pl.BlockSpec((tm * ROW_CHUNKS, HEAD_W), lambda i: (i, 0)))
        r = jnp.zeros((D_MODEL, HEAD_W), F32).at[:, :N_EXPERTS].set(router.astype(F32))
        r_hi = r.astype(BF16)
        r_lo = (r - r_hi.astype(F32)).astype(BF16)
        in_specs += [pl.BlockSpec((D_MODEL, HEAD_W), lambda i: (0, 0))] * 2
        args += [r_hi, r_lo]
        out_shape.append(jax.ShapeDtypeStruct((seq, HEAD_W), F32))
        out_specs.append(pl.BlockSpec((tm, HEAD_W), lambda i: (i, 0)))
    return pl.pallas_call(
        functools.partial(_outproj_kernel, with_router=with_router),
        out_shape=tuple(out_shape),
        grid=(seq // tm,),
        in_specs=in_specs,
        out_specs=tuple(out_specs),
        compiler_params=_cparams(("arbitrary",)),
        name="outproj",
    )(*args)


def _swiglu_part(h, wg, wu, wd):
    g = jnp.dot(h, wg, preferred_element_type=F32)
    u = jnp.dot(h, wu, preferred_element_type=F32)
    act = (g * (1.0 / (1.0 + jnp.exp(-g))) * u).astype(BF16)
    return jnp.dot(act, wd, preferred_element_type=F32)


def _dense_ffn_kernel(h_ref, x_ref, wg_ref, wu_ref, wd_ref, o_ref):
    @pl.when(pl.program_id(1) == 0)
    def _():
        o_ref[...] = x_ref[...]

    o_ref[...] += _swiglu_part(h_ref[...], wg_ref[...], wu_ref[...], wd_ref[...])


def _dense_ffn(h, x1, wg, wu, wd):
    seq = x1.shape[0]
    tm, tf = ROW_TILE, FF_TILE
    return pl.pallas_call(
        _dense_ffn_kernel,
        out_shape=jax.ShapeDtypeStruct((seq, D_MODEL), F32),
        grid=(seq // tm, D_FF // tf),
        in_specs=[pl.BlockSpec((tm, D_MODEL), lambda i, j: (i, 0)),
                  pl.BlockSpec((tm, D_MODEL), lambda i, j: (i, 0)),
                  pl.BlockSpec((D_MODEL, tf), lambda i, j: (0, j)),
                  pl.BlockSpec((D_MODEL, tf), lambda i, j: (0, j)),
                  pl.BlockSpec((tf, D_MODEL), lambda i, j: (j, 0))],
        out_specs=pl.BlockSpec((tm, D_MODEL), lambda i, j: (i, 0)),
        compiler_params=_cparams(("arbitrary", "arbitrary")),
        name="dense_ffn",
    )(h, x1, wg.astype(BF16), wu.astype(BF16), wd.astype(BF16))


def _moe_ffn_kernel(be_ref, nused_ref, src_ref, src_next_ref, dst_ref, dst_prev_ref,
                    h3_ref, wg_ref, wu_ref, wd_ref, y_ref, x3, xb, acc, o3, gsem, ssem):
    b = pl.program_id(0)
    j = pl.program_id(1)
    nb = pl.num_programs(0)
    nj = pl.num_programs(1)
    tm = xb.shape[0]
    n_used = nused_ref[0]
    used = b < n_used

    def slab(ref, row0):
        return ref.at[pl.ds(pl.multiple_of(row0, 8), ROW_CHUNKS), :]

    def start_gather(idx_ref, slot):
        def start(r, c):
            pltpu.make_async_copy(slab(h3_ref, idx_ref[0, r]), slab(x3.at[slot], r * SLAB_PITCH),
                                  gsem.at[slot]).start()
            return c

        lax.fori_loop(0, tm, start, 0)

    def scatter_copy(idx_ref, r):
        return pltpu.make_async_copy(slab(o3, r * SLAB_PITCH), slab(y_ref, idx_ref[0, r]), ssem)

    def wait_gather(slot):
        n_rows = tm * ROW_CHUNKS
        pltpu.make_async_copy(h3_ref.at[pl.ds(0, n_rows), :], x3.at[slot, pl.ds(0, n_rows), :],
                              gsem.at[slot]).wait()

    def wait_scatter():
        n_rows = tm * ROW_CHUNKS
        pltpu.make_async_copy(o3.at[pl.ds(0, n_rows), :], y_ref.at[pl.ds(0, n_rows), :], ssem).wait()

    def flush_scatter(idx_ref):
        def start(r, c):
            scatter_copy(idx_ref, r).start()
            return c

        lax.fori_loop(0, tm, start, 0)
        wait_scatter()

    @pl.when(jnp.logical_and(j == 0, jnp.logical_and(b == 0, used)))
    def _():
        start_gather(src_ref, 0)
        o3[...] = jnp.zeros(o3.shape, F32)

    @pl.when(jnp.logical_and(j == 0, jnp.logical_and(b >= 1, b == n_used)))
    def _():
        flush_scatter(dst_prev_ref)

    @pl.when(jnp.logical_and(j == 0, jnp.where(b == 0, used, b - 1 < n_used)))
    def _():
        wait_gather(b % 2)

    @pl.when(jnp.logical_and(j == 0, used))
    def _():
        slot = b % 2
        for s in range(ROW_CHUNKS):
            xb[:, s * HEAD_W:(s + 1) * HEAD_W] = x3[slot, pl.ds(s, tm, stride=SLAB_PITCH), :].astype(BF16)
        acc[...] = jnp.zeros(acc.shape, F32)

    @pl.when(used)
    def _():
        per_step = tm // D_EXPERT_STEPS
        nslot = (b + 1) % 2

        def move(r):
            pltpu.make_async_copy(slab(h3_ref, src_next_ref[0, r]), slab(x3.at[nslot], r * SLAB_PITCH),
                                  gsem.at[nslot]).start()
            scatter_copy(dst_prev_ref, r).start()

        for i in range(per_step):
            move(j * per_step + i)

        @pl.when(j == 0)
        def _():
            for r in range(per_step * D_EXPERT_STEPS, tm):
                move(r)

        acc[...] += _swiglu_part(xb[...], wg_ref[...], wu_ref[...], wd_ref[...])

    @pl.when(jnp.logical_and(j == nj - 1, used))
    def _():
        wait_scatter()
        for s in range(ROW_CHUNKS):
            o3[pl.ds(s, tm, stride=SLAB_PITCH), :] = acc[:, s * HEAD_W:(s + 1) * HEAD_W]

        @pl.when(b == nb - 1)
        def _():
            flush_scatter(dst_ref)
            wait_gather((b + 1) % 2)


def _moe_ffn(h3, src_idx, dst_idx, dst_prev_idx, blk_expert, n_used, wg, wu, wd):
    seq = h3.shape[0] // ROW_CHUNKS
    n_blocks = src_idx.shape[0]
    tm, tf = MOE_TILE, MOE_FF_TILE
    nj = D_EXPERT // tf

    def jj(b, j, be, nu):
        return jnp.where(b < nu[0], j, nj - 1)

    def idx_spec(shift):
        return pl.BlockSpec((None, 1, tm), lambda b, j, be, nu: (jnp.clip(b + shift, 0, n_blocks - 1), 0, 0),
                            memory_space=pltpu.SMEM)

    return pl.pallas_call(
        _moe_ffn_kernel,
        out_shape=jax.ShapeDtypeStruct(((2 * seq + tm) * ROW_CHUNKS, HEAD_W), F32),
        grid_spec=pltpu.PrefetchScalarGridSpec(
            num_scalar_prefetch=2,
            grid=(n_blocks, nj),
            in_specs=[idx_spec(0), idx_spec(1), idx_spec(0), idx_spec(0),
                      pl.BlockSpec(memory_space=pl.ANY),
                      pl.BlockSpec((None, D_MODEL, tf), lambda b, j, be, nu: (be[b], 0, jj(b, j, be, nu))),
                      pl.BlockSpec((None, D_MODEL, tf), lambda b, j, be, nu: (be[b], 0, jj(b, j, be, nu))),
                      pl.BlockSpec((None, tf, D_MODEL), lambda b, j, be, nu: (be[b], jj(b, j, be, nu), 0))],
            out_specs=pl.BlockSpec(memory_space=pl.ANY),
            scratch_shapes=[pltpu.VMEM((2, tm * SLAB_PITCH, HEAD_W), F32),
                            pltpu.VMEM((tm, D_MODEL), BF16),
                            pltpu.VMEM((tm, D_MODEL), F32),
                            pltpu.VMEM((tm * SLAB_PITCH, HEAD_W), F32),
                            pltpu.SemaphoreType.DMA((2,)),
                            pltpu.SemaphoreType.DMA(())]),
        compiler_params=_cparams(("arbitrary", "arbitrary")),
        name="moe_ffn",
    )(blk_expert, n_used, src_idx, src_idx, dst_idx, dst_prev_idx, h3, wg, wu, wd)


def _combine_kernel(x_ref, ya_ref, yb_ref, route_ref, o_ref):
    r = route_ref[...]
    g1 = jnp.broadcast_to(r[:, 2:3], (r.shape[0], HEAD_W))
    g2 = jnp.broadcast_to(r[:, 3:4], (r.shape[0], HEAD_W))
    tm = r.shape[0]
    for s in range(ROW_CHUNKS):
        cols = slice(s * HEAD_W, (s + 1) * HEAD_W)
        rows = pl.ds(s, tm, stride=ROW_CHUNKS)
        o_ref[:, cols] = x_ref[:, cols] + g1 * ya_ref[rows, :] + g2 * yb_ref[rows, :]


def _moe_combine(x1, y, route):
    seq = x1.shape[0]
    tm = ROW_TILE
    nb = seq // tm
    big = pl.BlockSpec((tm, D_MODEL), lambda i: (i, 0))
    return pl.pallas_call(
        _combine_kernel,
        out_shape=jax.ShapeDtypeStruct((seq, D_MODEL), F32),
        grid=(nb,),
        in_specs=[big,
                  pl.BlockSpec((tm * ROW_CHUNKS, HEAD_W), lambda i: (i, 0)),
                  pl.BlockSpec((tm * ROW_CHUNKS, HEAD_W), lambda i: (i + nb, 0)),
                  pl.BlockSpec((tm, HEAD_W), lambda i: (i, 0))],
        out_specs=big,
        compiler_params=_cparams(("arbitrary",)),
        name="moe_combine",
    )(x1, y, y, route)


def _moe_plan(route, seq):
    tm = MOE_TILE
    n_assign = 2 * seq
    n_blocks = n_assign // tm + N_EXPERTS
    e = route[:, 0:2].astype(jnp.int32).reshape(-1)
    onehot = (e[:, None] == jnp.arange(N_EXPERTS, dtype=jnp.int32)[None, :]).astype(jnp.int32)
    csum = jnp.cumsum(onehot, axis=0)
    counts = csum[-1]
    rank = jnp.sum((csum - onehot) * onehot, axis=1)
    nblk = (counts + tm - 1) // tm
    cum_blk = jnp.cumsum(nblk)
    pad_start = (cum_blk - nblk) * tm
    pos = pad_start[e] + rank
    assign = jnp.arange(n_assign, dtype=jnp.int32)
    slot_assign = jnp.full((n_blocks * tm,), -1, jnp.int32).at[pos].set(assign)
    real = slot_assign >= 0
    tok = jnp.maximum(slot_assign, 0) // 2
    in_block = jnp.arange(n_blocks * tm, dtype=jnp.int32) % tm
    src_idx = tok * ROW_CHUNKS
    dst_idx = jnp.where(real, (slot_assign % 2) * seq + tok, n_assign + in_block) * ROW_CHUNKS
    src_idx = src_idx.reshape(n_blocks, 1, tm)
    dst_idx = dst_idx.reshape(n_blocks, 1, tm)
    scratch_blk = ((n_assign + jnp.arange(tm, dtype=jnp.int32)) * ROW_CHUNKS).reshape(1, 1, tm)
    dst_prev_idx = jnp.concatenate([scratch_blk, dst_idx[:-1]], axis=0)
    n_used = cum_blk[-1].astype(jnp.int32)
    blk = jnp.minimum(jnp.arange(n_blocks, dtype=jnp.int32), n_used - 1)
    blk_expert = jnp.sum((cum_blk[None, :] <= blk[:, None]).astype(jnp.int32), axis=1)
    blk_expert = jnp.minimum(blk_expert, N_EXPERTS - 1)
    return src_idx, dst_idx, dst_prev_idx, blk_expert, n_used.reshape(1)


def _moe(h3, x1, route, wg, wu, wd):
    seq = x1.shape[0]
    src_idx, dst_idx, dst_prev_idx, blk_expert, n_used = _moe_plan(route, seq)
    y = _moe_ffn(h3, src_idx, dst_idx, dst_prev_idx, blk_expert, n_used, wg, wu, wd)
    return _moe_combine(x1, y, route)


def _layer(x, i, a_norm, w_in, dqn, dkn, lq1, lk1, lq2, lk2, don, bqn, bkn, bon, w_out, f_norm, ffn, riders=()):
    seq = x.shape[0]
    lam_init = 0.8 - 0.6 * math.exp(-0.3 * i)
    gains, gmat, cos_t, sin_t = _segment_tables(seq, dqn, dkn, bqn, bkn)
    qaT, ka, vaT, qbT, kb, vbT = _inproj(x, a_norm, _permute_w_in(w_in), gains, gmat, cos_t, sin_t)
    out_a, casted = _diff_attn(qaT, ka, vaT, lq1, lk1, lq2, lk2, don, gmat[0, :HEAD_W, :HEAD_W], lam_init, riders)
    out_b = _dil_attn(qbT, kb, vbT, bon, gmat[1, :HEAD_W, :HEAD_W])
    if len(ffn) == 3:
        x1, h = _outproj(out_a, out_b, w_out, x, f_norm)
        return _dense_ffn(h, x1, *ffn), casted
    router, wg, wu, wd = ffn
    x1, h3, route = _outproj(out_a, out_b, w_out, x, f_norm, router)
    return _moe(h3, x1, route, wg, wu, wd), casted


def kernel(x, attn_norm_0, w_in_0, diff_q_norm_0, diff_k_norm_0, diff_lam_q1_0, diff_lam_k1_0, diff_lam_q2_0, diff_lam_k2_0, diff_out_norm_0, dil_q_norm_0, dil_k_norm_0, dil_out_norm_0, w_out_0, ffn_norm_0, ffn_w_gate_0, ffn_w_up_0, ffn_w_down_0, attn_norm_1, w_in_1, diff_q_norm_1, diff_k_norm_1, diff_lam_q1_1, diff_lam_k1_1, diff_lam_q2_1, diff_lam_k2_1, diff_out_norm_1, dil_q_norm_1, dil_k_norm_1, dil_out_norm_1, w_out_1, ffn_norm_1, router_1, moe_w_gate_1, moe_w_up_1, moe_w_down_1):
    b, seq, d = x.shape
    assert b == 1 and d == D_MODEL and seq % (DIL_TQ * DIL_SUBTILES) == 0 and seq % ATT_TQ == 0
    assert MOE_TILE == ROW_TILE and DIL_PAD % ROW_TILE == 0
    xs = x.reshape(seq, d)
    xs, moe_w = _layer(xs, 0, attn_norm_0, w_in_0, diff_q_norm_0, diff_k_norm_0, diff_lam_q1_0, diff_lam_k1_0,
                       diff_lam_q2_0, diff_lam_k2_0, diff_out_norm_0, dil_q_norm_0, dil_k_norm_0, dil_out_norm_0,
                       w_out_0, ffn_norm_0, (ffn_w_gate_0, ffn_w_up_0, ffn_w_down_0),
                       riders=(moe_w_gate_1, moe_w_up_1, moe_w_down_1))
    xs, _ = _layer(xs, 1, attn_norm_1, w_in_1, diff_q_norm_1, diff_k_norm_1, diff_lam_q1_1, diff_lam_k1_1,
                   diff_lam_q2_1, diff_lam_k2_1, diff_out_norm_1, dil_q_norm_1, dil_k_norm_1, dil_out_norm_1,
                   w_out_1, ffn_norm_1, (router_1, *moe_w))
    return xs.reshape(b, seq, d)
```

```python
import functools
import math

import numpy as np
import jax
import jax.numpy as jnp
from jax import lax
from jax.experimental import pallas as pl
from jax.experimental.pallas import tpu as pltpu

F32 = jnp.float32
BF16 = jnp.bfloat16

D_MODEL = 2048
N_HEADS = 8
HEAD_W = 128
DIFF_DIM = 64
SEG_W = N_HEADS * HEAD_W
N_SEG = 6
ROW_CHUNKS = D_MODEL // HEAD_W
SLAB_PITCH = 24
DIL_BRANCHES = ((128, 1), (512, 4), (2048, 16))
N_SIDE = 64
D_FF = 5632
N_EXPERTS = 8
D_EXPERT = 7168
EPS = 1e-6
NEG = -1e30
LOG2E = 1.4426950408889634

MAP0_LANE = 0
MAP1_LANE = 32
ROW_SUM_FLOOR = 2.0 ** -80

V_ROWS = 144
DIL_PAD = 1024
V7X_VMEM_BYTES = 64 * 1024 * 1024
VMEM_LIMIT = V7X_VMEM_BYTES * 7 // 8

ROW_TILE = 512
ATT_TQ = 512
ATT_TK = 8192
DIL_TQ = 256
DIL_SUBTILES = 8
DIL_CHUNK = 768
FF_TILE = 512
D_EXPERT_STEPS = 7
MOE_FF_TILE = D_EXPERT // D_EXPERT_STEPS
MOE_TILE = 512


def _cparams(sem):
    return pltpu.CompilerParams(dimension_semantics=sem, vmem_limit_bytes=VMEM_LIMIT)


def _diff_lane_perm():
    perm = np.zeros(HEAD_W, np.int32)
    for m in range(2):
        for t in range(DIFF_DIM):
            p = (t // 32) * 64 + m * 32 + (t % 32)
            perm[p] = m * DIFF_DIM + t
    return perm


def _segment_tables(seq, dqn, dkn, bqn, bkn):
    perm = _diff_lane_perm()
    t_of_lane = perm % DIFF_DIM
    ones = jnp.ones((SEG_W,), F32)
    g_qa = jnp.tile(dqn.astype(F32)[t_of_lane], N_HEADS)
    g_ka = jnp.tile(dkn.astype(F32)[t_of_lane], N_HEADS)
    g_qb = jnp.tile(bqn.astype(F32), N_HEADS)
    g_kb = jnp.tile(bkn.astype(F32), N_HEADS)
    gains = jnp.stack([g_qa, g_ka, ones, g_qb, g_kb, ones]).reshape(N_SEG, 1, SEG_W)

    lane = np.arange(HEAD_W)
    map_of_lane = (lane // 32) % 2
    g_diff = (map_of_lane[:, None] == map_of_lane[None, :]).astype(np.float32)
    g_dil = np.ones((HEAD_W, HEAD_W), np.float32)
    pair = np.eye(2, dtype=np.float32)
    gmat = jnp.asarray(np.stack([np.kron(pair, g_diff), np.kron(pair, g_dil)]), BF16)

    pos = jnp.arange(seq, dtype=F32)

    def ang(dim):
        inv = 10000.0 ** (-jnp.arange(0, dim, 2, dtype=F32) / dim)
        return pos[:, None] * inv[None, :]

    a32 = ang(DIFF_DIM)
    a64 = ang(HEAD_W)
    cos_a = jnp.tile(jnp.cos(a32), (1, 4))
    sin_a = jnp.tile(jnp.sin(a32), (1, 4))
    cos_b = jnp.tile(jnp.cos(a64), (1, 2))
    sin_b = jnp.tile(jnp.sin(a64), (1, 2))
    sign = jnp.where(jnp.arange(HEAD_W) < 64, -1.0, 1.0).astype(F32)[None, :]
    cos_t = jnp.stack([cos_a, cos_b])
    sin_t = jnp.stack([sin_a * sign, sin_b * sign])
    return gains, gmat, cos_t, sin_t


def _permute_w_in(w_in):
    w = w_in.astype(BF16)
    qk = w[:, :2 * SEG_W].reshape(D_MODEL, 2 * N_HEADS, 2, 2, 32)
    qk = qk.transpose(0, 1, 3, 2, 4).reshape(D_MODEL, 2 * SEG_W)
    return jnp.concatenate([qk, w[:, 2 * SEG_W:]], axis=1)


def _inproj_kernel(x_ref, g_ref, w_ref, gain_ref, cos_ref, sin_ref, gmat_ref,
                   qaT_ref, ka_ref, vaT_ref, qbT_ref, kb_ref, vbT_ref, h_scr,
                   *, n_row_blocks, pad_blocks):
    ip = pl.program_id(0)
    j = pl.program_id(1)
    real = jnp.logical_and(ip >= pad_blocks, ip < pad_blocks + n_row_blocks)
    tm = x_ref.shape[0]

    @pl.when(jnp.logical_and(real, j == 0))
    def _():
        x = x_ref[...]
        ms = jnp.mean(x * x, axis=-1, keepdims=True)
        h_scr[...] = (x * lax.rsqrt(ms + EPS) * g_ref[...]).astype(BF16)

    def norm_rope(y, ss, c, n_group, scale):
        yn = y * lax.rsqrt(ss * (1.0 / n_group) + EPS) * gain_ref[:, c * HEAD_W:(c + 1) * HEAD_W]
        out = yn * cos_ref[...] + pltpu.roll(yn, 64, 1) * sin_ref[...]
        if scale != 1.0:
            out = out * scale
        return out

    def aug_rows():
        row = lax.broadcasted_iota(jnp.int32, (V_ROWS - HEAD_W, tm), 0)
        return jnp.where(row == 0, 1.0, 0.0).astype(BF16)

    def segment(seg):
        acc = jnp.dot(h_scr[...], w_ref[...], preferred_element_type=F32)
        for c in range(N_HEADS):
            y = acc[:, c * HEAD_W:(c + 1) * HEAD_W]
            if seg in (0, 1, 3, 4) and c % 2 == 0:
                pair = acc[:, c * HEAD_W:(c + 2) * HEAD_W]
                ss2 = jnp.dot((pair * pair).astype(BF16), gmat_ref[...], preferred_element_type=F32)
            if seg in (0, 1, 3, 4):
                ss = ss2[:, (c % 2) * HEAD_W:(c % 2 + 1) * HEAD_W]
            if seg == 0:
                qaT_ref[c] = norm_rope(y, ss, c, DIFF_DIM, DIFF_DIM ** -0.5 * LOG2E).T.astype(BF16)
            elif seg == 1:
                ka_ref[:, c * HEAD_W:(c + 1) * HEAD_W] = norm_rope(y, ss, c, DIFF_DIM, 1.0).astype(BF16)
            elif seg == 3:
                qbT_ref[c] = norm_rope(y, ss, c, HEAD_W, HEAD_W ** -0.5 * LOG2E).T.astype(BF16)
            elif seg == 4:
                kb_ref[:, c * HEAD_W:(c + 1) * HEAD_W] = norm_rope(y, ss, c, HEAD_W, 1.0).astype(BF16)
            elif seg == 2:
                vaT_ref[c] = y.T.astype(BF16)
            else:
                vbT_ref[c, 0:HEAD_W, :] = y.T.astype(BF16)
                vbT_ref[c, HEAD_W:V_ROWS, :] = aug_rows()

    for seg in range(N_SEG):
        pl.when(jnp.logical_and(real, j == seg))(functools.partial(segment, seg))

    @pl.when(jnp.logical_and(jnp.logical_not(real), j == 4))
    def _():
        kb_ref[...] = jnp.zeros(kb_ref.shape, BF16)

    @pl.when(jnp.logical_and(jnp.logical_not(real), j == 5))
    def _():
        vbT_ref[...] = jnp.zeros(vbT_ref.shape, BF16)


def _inproj(x, a_norm, w_in_p, gains, gmat, cos_t, sin_t):
    seq = x.shape[0]
    tm = ROW_TILE
    nrb = seq // tm
    pb = DIL_PAD // tm
    seq_p = seq + 2 * DIL_PAD

    def row(ip):
        return jnp.clip(ip - pb, 0, nrb - 1)

    kern = functools.partial(_inproj_kernel, n_row_blocks=nrb, pad_blocks=pb)
    out_shape = (
        jax.ShapeDtypeStruct((N_HEADS, HEAD_W, seq), BF16),
        jax.ShapeDtypeStruct((seq, SEG_W), BF16),
        jax.ShapeDtypeStruct((N_HEADS, HEAD_W, seq), BF16),
        jax.ShapeDtypeStruct((N_HEADS, HEAD_W, seq), BF16),
        jax.ShapeDtypeStruct((seq_p, SEG_W), BF16),
        jax.ShapeDtypeStruct((N_HEADS, V_ROWS, seq_p), BF16),
    )
    in_specs = [
        pl.BlockSpec((tm, D_MODEL), lambda ip, j: (row(ip), 0)),
        pl.BlockSpec((1, D_MODEL), lambda ip, j: (0, 0)),
        pl.BlockSpec((D_MODEL, SEG_W), lambda ip, j: (0, j)),
        pl.BlockSpec((None, 1, SEG_W), lambda ip, j: (j, 0, 0)),
        pl.BlockSpec((None, tm, HEAD_W), lambda ip, j: (j // 3, row(ip), 0)),
        pl.BlockSpec((None, tm, HEAD_W), lambda ip, j: (j // 3, row(ip), 0)),
        pl.BlockSpec((None, 2 * HEAD_W, 2 * HEAD_W), lambda ip, j: (j // 3, 0, 0)),
    ]
    out_specs = (
        pl.BlockSpec((N_HEADS, HEAD_W, tm), lambda ip, j: (0, 0, row(ip))),
        pl.BlockSpec((tm, SEG_W), lambda ip, j: (row(ip), 0)),
        pl.BlockSpec((N_HEADS, HEAD_W, tm), lambda ip, j: (0, 0, row(ip))),
        pl.BlockSpec((N_HEADS, HEAD_W, tm), lambda ip, j: (0, 0, row(ip))),
        pl.BlockSpec((tm, SEG_W), lambda ip, j: (ip, 0)),
        pl.BlockSpec((N_HEADS, V_ROWS, tm), lambda ip, j: (0, 0, ip)),
    )
    return pl.pallas_call(
        kern,
        out_shape=out_shape,
        grid=(nrb + 2 * pb, N_SEG),
        in_specs=in_specs,
        out_specs=out_specs,
        scratch_shapes=[pltpu.VMEM((tm, D_MODEL), BF16)],
        compiler_params=_cparams(("arbitrary", "arbitrary")),
        name="inproj",
    )(x, a_norm.reshape(1, D_MODEL).astype(F32), w_in_p, gains, cos_t, sin_t, gmat)


def _flash_step(s, m_old, acc_ref, vT_t):
    m_new = jnp.maximum(m_old, jnp.max(s, axis=0, keepdims=True))
    alpha = jnp.exp2(m_old - m_new)
    p = jnp.exp2((s - m_new).astype(BF16))
    acc_ref[...] = acc_ref[...] * alpha + jnp.dot(vT_t, p, preferred_element_type=F32)
    return m_new


def _flash_step_sum(s, m_old, l_old, acc_ref, vT_t):
    m_new = jnp.maximum(m_old, jnp.max(s, axis=0, keepdims=True))
    alpha = jnp.exp2(m_old - m_new)
    e = jnp.exp2(s - m_new)
    acc_ref[...] = acc_ref[...] * alpha + jnp.dot(vT_t, e.astype(BF16), preferred_element_type=F32)
    return m_new, l_old * alpha + jnp.sum(e, axis=0, keepdims=True)


def _head_out(aT, gain_row, out_scale):
    a = aT.T
    ms = jnp.mean(a * a, axis=-1, keepdims=True)
    y = a * lax.rsqrt(ms + EPS) * gain_row
    if out_scale != 1.0:
        y = y * out_scale
    return y


def _diff_attn_kernel(*refs, lam_init, tk, n_riders):
    lq1_ref, lk1_ref, lq2_ref, lk2_ref, qT_ref, k_ref, vT_ref, og_ref, gmat_ref = refs[:9]
    rider_in = refs[9:9 + n_riders]
    o_ref = refs[9 + n_riders]
    rider_out = refs[10 + n_riders:10 + 2 * n_riders]
    acc0, acc1, lsum, k0_scr, k1_scr, kmax_scr = refs[10 + 2 * n_riders:]
    tq = qT_ref.shape[1]
    seq = k_ref.shape[0]
    lane0, lane1 = MAP1_LANE, MAP0_LANE

    @pl.when(pl.program_id(1) == 0)
    def _():
        ck = min(1024, seq)

        def kchunk(c, mx):
            rows = pl.ds(pl.multiple_of(c * ck, ck), ck)
            kc = k_ref[rows, :]
            kf = kc.astype(F32)
            n2 = jnp.dot((kf * kf).astype(BF16), gmat_ref[...], preferred_element_type=F32)
            lane = lax.broadcasted_iota(jnp.int32, kc.shape, 1)
            one = jnp.ones_like(kc)
            k0_scr[rows, :] = jnp.where(lane == lane0, one, kc)
            k1_scr[rows, :] = jnp.where(lane == lane1, one, kc)
            return jnp.maximum(mx, jnp.max(n2, axis=0, keepdims=True))

        kn2 = lax.fori_loop(0, seq // ck, kchunk, jnp.zeros((1, HEAD_W), F32))
        kmax_scr[...] = jnp.sqrt(kn2)

    qT = qT_ref[...]
    row = lax.broadcasted_iota(jnp.int32, qT.shape, 0)
    in_map1 = ((row // 32) % 2) == 1
    zero = jnp.zeros_like(qT)
    qf = qT.astype(F32)
    q2 = qf * qf
    nq0 = jnp.sum(jnp.where(in_map1, 0.0, q2), axis=0, keepdims=True)
    nq1 = jnp.sum(jnp.where(in_map1, q2, 0.0), axis=0, keepdims=True)
    kmax = kmax_scr[...]
    b0 = jnp.sqrt(nq0) * kmax[:, MAP0_LANE:MAP0_LANE + 1]
    b1 = jnp.sqrt(nq1) * kmax[:, MAP1_LANE:MAP1_LANE + 1]
    q0 = jnp.where(in_map1, zero, qT)
    q1 = jnp.where(in_map1, qT, zero)
    q0s = jnp.where(row == lane0, jnp.broadcast_to(-b0, qf.shape).astype(BF16), q0)
    q1s = jnp.where(row == lane1, jnp.broadcast_to(-b1, qf.shape).astype(BF16), q1)
    acc0[...] = jnp.zeros(acc0.shape, F32)
    acc1[...] = jnp.zeros(acc1.shape, F32)

    def body(kt, carry):
        l0, l1 = carry
        rows = pl.ds(pl.multiple_of(kt * tk, tk), tk)
        vT_t = vT_ref[:, rows]
        e0 = jnp.exp2(jnp.dot(k0_scr[rows, :], q0s, preferred_element_type=F32))
        acc0[...] += jnp.dot(vT_t, e0.astype(BF16), preferred_element_type=F32)
        e1 = jnp.exp2(jnp.dot(k1_scr[rows, :], q1s, preferred_element_type=F32))
        acc1[...] += jnp.dot(vT_t, e1.astype(BF16), preferred_element_type=F32)
        for w_ref, wb_ref in zip(rider_in, rider_out):
            r = w_ref.shape[0] // (seq // tk)
            part = pl.ds(pl.multiple_of(kt * r, 16), r)
            wb_ref[part, :] = w_ref[part, :].astype(BF16)
        return l0 + jnp.sum(e0, axis=0, keepdims=True), l1 + jnp.sum(e1, axis=0, keepdims=True)

    l_init = jnp.zeros((1, tq), F32)
    l0, l1 = lax.fori_loop(0, seq // tk, body, (l_init, l_init))
    lsum[0:1, :] = l0
    lsum[1:2, :] = l1

    @pl.when(jnp.logical_not(jnp.min(jnp.minimum(l0, l1)) >= ROW_SUM_FLOOR))
    def _():
        acc0[...] = jnp.zeros(acc0.shape, F32)
        acc1[...] = jnp.zeros(acc1.shape, F32)

        def robust(kt, carry):
            m0, l0, m1, l1 = carry
            rows = pl.ds(pl.multiple_of(kt * tk, tk), tk)
            k_t = k_ref[rows, :]
            vT_t = vT_ref[:, rows]
            m0, l0 = _flash_step_sum(jnp.dot(k_t, q0, preferred_element_type=F32), m0, l0, acc0, vT_t)
            m1, l1 = _flash_step_sum(jnp.dot(k_t, q1, preferred_element_type=F32), m1, l1, acc1, vT_t)
            return m0, l0, m1, l1

        m_init = jnp.full((1, tq), NEG, F32)
        _, l0, _, l1 = lax.fori_loop(0, seq // tk, robust, (m_init, l_init, m_init, l_init))
        lsum[0:1, :] = l0
        lsum[1:2, :] = l1

    lam = (jnp.exp(jnp.sum(lq1_ref[...] * lk1_ref[...], axis=-1, keepdims=True))
           - jnp.exp(jnp.sum(lq2_ref[...] * lk2_ref[...], axis=-1, keepdims=True)) + lam_init)
    o0 = acc0[...] * (1.0 / lsum[0:1, :])
    o1 = acc1[...] * (1.0 / lsum[1:2, :])
    aT = o0 - lam * o1
    o_ref[...] = _head_out(aT, og_ref[...], 1.0 - lam_init).astype(o_ref.dtype)


def _diff_attn(qaT, ka, vaT, lq1, lk1, lq2, lk2, og, gmat_diff, lam_init, riders=()):
    seq = ka.shape[0]
    tq = min(ATT_TQ, seq)
    tk = min(ATT_TK, seq)
    nq = seq // tq
    n_steps = N_HEADS * nq
    kern = functools.partial(_diff_attn_kernel, lam_init=lam_init, tk=tk, n_riders=len(riders))
    vec = lambda v: v.reshape(1, -1).astype(F32)
    small = pl.BlockSpec((1, DIFF_DIM), lambda h, qi: (0, 0))
    once = pl.Buffered(1)
    rider_specs = []
    for w in riders:
        n_e, n_r, n_c = w.shape
        per_e = n_steps // n_e
        rider_specs.append(pl.BlockSpec(
            (None, n_r // per_e, n_c),
            lambda h, qi, per_e=per_e: ((h * nq + qi) // per_e, (h * nq + qi) % per_e, 0)))
    outs = pl.pallas_call(
        kern,
        out_shape=(jax.ShapeDtypeStruct((seq, SEG_W), BF16),
                   *[jax.ShapeDtypeStruct(w.shape, BF16) for w in riders]),
        grid=(N_HEADS, nq),
        in_specs=[small, small, small, small,
                  pl.BlockSpec((None, HEAD_W, tq), lambda h, qi: (h, 0, qi)),
                  pl.BlockSpec((seq, HEAD_W), lambda h, qi: (0, h), pipeline_mode=once),
                  pl.BlockSpec((None, HEAD_W, seq), lambda h, qi: (h, 0, 0), pipeline_mode=once),
                  pl.BlockSpec((1, HEAD_W), lambda h, qi: (0, 0)),
                  pl.BlockSpec((HEAD_W, HEAD_W), lambda h, qi: (0, 0)),
                  *rider_specs],
        out_specs=(pl.BlockSpec((tq, HEAD_W), lambda h, qi: (qi, h)), *rider_specs),
        scratch_shapes=[pltpu.VMEM((HEAD_W, tq), F32), pltpu.VMEM((HEAD_W, tq), F32),
                        pltpu.VMEM((8, tq), F32),
                        pltpu.VMEM((seq, HEAD_W), BF16), pltpu.VMEM((seq, HEAD_W), BF16),
                        pltpu.VMEM((1, HEAD_W), F32)],
        compiler_params=_cparams(("arbitrary", "arbitrary")),
        name="diff_attn",
    )(vec(lq1), vec(lk1), vec(lq2), vec(lk2), qaT, ka, vaT, vec(og), gmat_diff, *riders)
    return outs[0], tuple(outs[1:])


def _dil_chunks(tq):
    chunks, off = [], 0
    for _, dil in DIL_BRANCHES:
        pad = -(-(N_SIDE * dil) // 128) * 128
        total = tq + 2 * pad
        c0 = 0
        while c0 < total:
            nk = min(DIL_CHUNK, total - c0)
            chunks.append((dil, c0 - pad, nk, off))
            off += nk
            c0 += nk
    return chunks, off


def _dil_bias(tq):
    chunks, total = _dil_chunks(tq)
    bias = np.full((total, tq), NEG, np.float32)
    col = np.arange(tq)[None, :]
    for dil, rel, nk, off in chunks:
        delta = rel + np.arange(nk)[:, None] - col
        ok = (np.abs(delta) <= N_SIDE * dil) & (delta % dil == 0)
        bias[off:off + nk][ok] = 0.0
    return jnp.asarray(bias)


def _dil_attn_kernel(qT_ref, k_ref, vT_ref, bias_ref, og_ref, gmat_ref, o_ref, acc, kmax_scr, *, seq, chunks):
    tq = acc.shape[1]
    n_sub = qT_ref.shape[1] // tq
    base = pl.program_id(1) * (tq * n_sub)

    @pl.when(pl.program_id(1) == 0)
    def _():
        ck = 1024

        def kchunk(c, mx):
            kf = k_ref[pl.ds(pl.multiple_of(c * ck, ck), ck), :].astype(F32)
            n2 = jnp.dot((kf * kf).astype(BF16), gmat_ref[...], preferred_element_type=F32)
            return jnp.maximum(mx, jnp.max(n2, axis=0, keepdims=True))

        kn2 = lax.fori_loop(0, k_ref.shape[0] // ck, kchunk, jnp.zeros((1, HEAD_W), F32))
        kmax_scr[...] = jnp.sqrt(kn2)

    def windows(i0):
        for dil, rel, nk, off in chunks:
            start = pl.multiple_of(i0 + (DIL_PAD + rel), 128)
            yield rel, nk, off, k_ref[pl.ds(start, nk), :], vT_ref[:, pl.ds(start, nk)]

    def finish(t, num):
        oT = num[0:HEAD_W, :] * (1.0 / num[HEAD_W:HEAD_W + 1, :])
        o_ref[t * tq:(t + 1) * tq, :] = _head_out(oT, og_ref[...], 1.0).astype(o_ref.dtype)

    lmin = None
    for t in range(n_sub):
        qT = qT_ref[:, t * tq:(t + 1) * tq]
        qf = qT.astype(F32)
        shift = jnp.sqrt(jnp.sum(qf * qf, axis=0, keepdims=True)) * kmax_scr[:, 0:1]
        num = jnp.zeros((V_ROWS, tq), F32)
        for rel, nk, off, k_t, vT_t in windows(base + t * tq):
            s = jnp.dot(k_t, qT, preferred_element_type=F32) - shift + bias_ref[off:off + nk, :]
            num = num + jnp.dot(vT_t, jnp.exp2(s).astype(BF16), preferred_element_type=F32)
        finish(t, num)
        l_t = jnp.min(num[HEAD_W:HEAD_W + 1, :])
        lmin = l_t if lmin is None else jnp.minimum(lmin, l_t)

    @pl.when(jnp.logical_not(lmin >= ROW_SUM_FLOOR))
    def _():
        for t in range(n_sub):
            i0 = base + t * tq
            qT = qT_ref[:, t * tq:(t + 1) * tq]
            acc[...] = jnp.zeros(acc.shape, F32)
            m = jnp.full((1, tq), NEG, F32)
            for rel, nk, off, k_t, vT_t in windows(i0):
                s = jnp.dot(k_t, qT, preferred_element_type=F32)
                kpos = lax.broadcasted_iota(jnp.int32, (nk, tq), 0) + (i0 + rel)
                valid = jnp.logical_and(kpos >= 0, kpos < seq)
                s = jnp.where(valid, s + bias_ref[off:off + nk, :], NEG)
                m = _flash_step(s, m, acc, vT_t)
            finish(t, acc[...])


def _dil_attn(qbT, kb, vbT, og, gmat_ones):
    seq = qbT.shape[2]
    seq_p = kb.shape[0]
    tq = min(DIL_TQ, seq)
    n_sub = DIL_SUBTILES
    chunks, _ = _dil_chunks(tq)
    bias = _dil_bias(tq)
    kern = functools.partial(_dil_attn_kernel, seq=seq, chunks=chunks)
    return pl.pallas_call(
        kern,
        out_shape=jax.ShapeDtypeStruct((seq, SEG_W), BF16),
        grid=(N_HEADS, seq // (tq * n_sub)),
        in_specs=[pl.BlockSpec((None, HEAD_W, tq * n_sub), lambda h, qi: (h, 0, qi)),
                  pl.BlockSpec((seq_p, HEAD_W), lambda h, qi: (0, h)),
                  pl.BlockSpec((None, V_ROWS, seq_p), lambda h, qi: (h, 0, 0)),
                  pl.BlockSpec(bias.shape, lambda h, qi: (0, 0)),
                  pl.BlockSpec((1, HEAD_W), lambda h, qi: (0, 0)),
                  pl.BlockSpec((HEAD_W, HEAD_W), lambda h, qi: (0, 0))],
        out_specs=pl.BlockSpec((tq * n_sub, HEAD_W), lambda h, qi: (qi, h)),
        scratch_shapes=[pltpu.VMEM((V_ROWS, tq), F32), pltpu.VMEM((1, HEAD_W), F32)],
        compiler_params=_cparams(("arbitrary", "arbitrary")),
        name="dil_attn",
    )(qbT, kb, vbT, bias, og.reshape(1, HEAD_W).astype(F32), gmat_ones)


def _outproj_kernel(*refs, with_router):
    if with_router:
        a_ref, b_ref, w_ref, x_ref, g_ref, r_hi_ref, r_lo_ref, x1_ref, h_ref, route_ref = refs
    else:
        a_ref, b_ref, w_ref, x_ref, g_ref, x1_ref, h_ref = refs
    acc = (jnp.dot(a_ref[...], w_ref[0:SEG_W, :], preferred_element_type=F32)
           + jnp.dot(b_ref[...], w_ref[SEG_W:2 * SEG_W, :], preferred_element_type=F32))
    x1 = x_ref[...] + acc
    x1_ref[...] = x1
    ms = jnp.mean(x1 * x1, axis=-1, keepdims=True)
    hn = x1 * lax.rsqrt(ms + EPS) * g_ref[...]
    if not with_router:
        h_ref[...] = hn.astype(BF16)
    else:
        tm = hn.shape[0]
        for s in range(ROW_CHUNKS):
            h_ref[pl.ds(s, tm, stride=ROW_CHUNKS), :] = hn[:, s * HEAD_W:(s + 1) * HEAD_W]
        h_hi = hn.astype(BF16)
        h_lo = (hn - h_hi.astype(F32)).astype(BF16)
        logits = (jnp.dot(h_hi, r_hi_ref[...], preferred_element_type=F32)
                  + jnp.dot(h_hi, r_lo_ref[...], preferred_element_type=F32)
                  + jnp.dot(h_lo, r_hi_ref[...], preferred_element_type=F32))
        lane = lax.broadcasted_iota(jnp.int32, logits.shape, 1)
        ninf = jnp.float32(-jnp.inf)
        lg = jnp.where(lane < N_EXPERTS, logits, ninf)
        v1 = jnp.max(lg, axis=-1, keepdims=True)
        i1 = jnp.min(jnp.where(lg == v1, lane, HEAD_W), axis=-1, keepdims=True)
        lg2 = jnp.where(lane == i1, ninf, lg)
        v2 = jnp.max(lg2, axis=-1, keepdims=True)
        i2 = jnp.min(jnp.where(lg2 == v2, lane, HEAD_W), axis=-1, keepdims=True)
        g1 = 1.0 / (1.0 + jnp.exp(v2 - v1))
        g2 = 1.0 - g1
        route_ref[...] = jnp.where(lane == 0, i1.astype(F32),
                                   jnp.where(lane == 1, i2.astype(F32),
                                             jnp.where(lane == 2, g1, jnp.where(lane == 3, g2, 0.0))))


def _outproj(out_a, out_b, w_out, x, f_norm, router=None):
    seq = x.shape[0]
    tm = ROW_TILE
    with_router = router is not None
    in_specs = [pl.BlockSpec((tm, SEG_W), lambda i: (i, 0)),
                pl.BlockSpec((tm, SEG_W), lambda i: (i, 0)),
                pl.BlockSpec((D_MODEL, D_MODEL), lambda i: (0, 0)),
                pl.BlockSpec((tm, D_MODEL), lambda i: (i, 0)),
                pl.BlockSpec((1, D_MODEL), lambda i: (0, 0))]
    args = [out_a, out_b, w_out.astype(BF16), x, f_norm.reshape(1, D_MODEL).astype(F32)]
    out_shape = [jax.ShapeDtypeStruct((seq, D_MODEL), F32)]
    out_specs = [pl.BlockSpec((tm, D_MODEL), lambda i: (i, 0))]
    if not with_router:
        out_shape.append(jax.ShapeDtypeStruct((seq, D_MODEL), BF16))
        out_specs.append(pl.BlockSpec((tm, D_MODEL), lambda i: (i, 0)))
    else:
        out_shape.append(jax.ShapeDtypeStruct((seq * ROW_CHUNKS, HEAD_W), F32))
        out_specs.append(pl.BlockSpec((tm * ROW_CHUNKS, HEAD_W), lambda i: (i, 0)))
        r = jnp.zeros((D_MODEL, HEAD_W), F32).at[:, :N_EXPERTS].set(router.astype(F32))
        r_hi = r.astype(BF16)
        r_lo = (r - r_hi.astype(F32)).astype(BF16)
        in_specs += [pl.BlockSpec((D_MODEL, HEAD_W), lambda i: (0, 0))] * 2
        args += [r_hi, r_lo]
        out_shape.append(jax.ShapeDtypeStruct((seq, HEAD_W), F32))
        out_specs.append(pl.BlockSpec((tm, HEAD_W), lambda i: (i, 0)))
    return pl.pallas_call(
        functools.partial(_outproj_kernel, with_router=with_router),
        out_shape=tuple(out_shape),
        grid=(seq // tm,),
        in_specs=in_specs,
        out_specs=tuple(out_specs),
        compiler_params=_cparams(("arbitrary",)),
        name="outproj",
    )(*args)


def _swiglu_part(h, wg, wu, wd):
    g = jnp.dot(h, wg, preferred_element_type=F32)
    u = jnp.dot(h, wu, preferred_element_type=F32)
    act = (g * (1.0 / (1.0 + jnp.exp(-g))) * u).astype(BF16)
    return jnp.dot(act, wd, preferred_element_type=F32)


def _dense_ffn_kernel(h_ref, x_ref, wg_ref, wu_ref, wd_ref, o_ref):
    @pl.when(pl.program_id(1) == 0)
    def _():
        o_ref[...] = x_ref[...]

    o_ref[...] += _swiglu_part(h_ref[...], wg_ref[...], wu_ref[...], wd_ref[...])


def _dense_ffn(h, x1, wg, wu, wd):
    seq = x1.shape[0]
    tm, tf = ROW_TILE, FF_TILE
    return pl.pallas_call(
        _dense_ffn_kernel,
        out_shape=jax.ShapeDtypeStruct((seq, D_MODEL), F32),
        grid=(seq // tm, D_FF // tf),
        in_specs=[pl.BlockSpec((tm, D_MODEL), lambda i, j: (i, 0)),
                  pl.BlockSpec((tm, D_MODEL), lambda i, j: (i, 0)),
                  pl.BlockSpec((D_MODEL, tf), lambda i, j: (0, j)),
                  pl.BlockSpec((D_MODEL, tf), lambda i, j: (0, j)),
                  pl.BlockSpec((tf, D_MODEL), lambda i, j: (j, 0))],
        out_specs=pl.BlockSpec((tm, D_MODEL), lambda i, j: (i, 0)),
        compiler_params=_cparams(("arbitrary", "arbitrary")),
        name="dense_ffn",
    )(h, x1, wg.astype(BF16), wu.astype(BF16), wd.astype(BF16))


def _moe_ffn_kernel(be_ref, nused_ref, src_ref, src_next_ref, dst_ref, dst_prev_ref,
                    h3_ref, wg_ref, wu_ref, wd_ref, y_ref, x3, xb, acc, o3, gsem, ssem):
    b = pl.program_id(0)
    j = pl.program_id(1)
    nb = pl.num_programs(0)
    nj = pl.num_programs(1)
    tm = xb.shape[0]
    n_used = nused_ref[0]
    used = b < n_used

    def slab(ref, row0):
        return ref.at[pl.ds(pl.multiple_of(row0, 8), ROW_CHUNKS), :]

    def start_gather(idx_ref, slot):
        def start(r, c):
            pltpu.make_async_copy(slab(h3_ref, idx_ref[0, r]), slab(x3.at[slot], r * SLAB_PITCH),
                                  gsem.at[slot]).start()
            return c

        lax.fori_loop(0, tm, start, 0)

    def scatter_copy(idx_ref, r):
        return pltpu.make_async_copy(slab(o3, r * SLAB_PITCH), slab(y_ref, idx_ref[0, r]), ssem)

    def wait_gather(slot):
        n_rows = tm * ROW_CHUNKS
        pltpu.make_async_copy(h3_ref.at[pl.ds(0, n_rows), :], x3.at[slot, pl.ds(0, n_rows), :],
                              gsem.at[slot]).wait()

    def wait_scatter():
        n_rows = tm * ROW_CHUNKS
        pltpu.make_async_copy(o3.at[pl.ds(0, n_rows), :], y_ref.at[pl.ds(0, n_rows), :], ssem).wait()

    def flush_scatter(idx_ref):
        def start(r, c):
            scatter_copy(idx_ref, r).start()
            return c

        lax.fori_loop(0, tm, start, 0)
        wait_scatter()

    @pl.when(jnp.logical_and(j == 0, jnp.logical_and(b == 0, used)))
    def _():
        start_gather(src_ref, 0)
        o3[...] = jnp.zeros(o3.shape, F32)

    @pl.when(jnp.logical_and(j == 0, jnp.logical_and(b >= 1, b == n_used)))
    def _():
        flush_scatter(dst_prev_ref)

    @pl.when(jnp.logical_and(j == 0, jnp.where(b == 0, used, b - 1 < n_used)))
    def _():
        wait_gather(b % 2)

    @pl.when(jnp.logical_and(j == 0, used))
    def _():
        slot = b % 2
        for s in range(ROW_CHUNKS):
            xb[:, s * HEAD_W:(s + 1) * HEAD_W] = x3[slot, pl.ds(s, tm, stride=SLAB_PITCH), :].astype(BF16)
        acc[...] = jnp.zeros(acc.shape, F32)

    @pl.when(used)
    def _():
        per_step = tm // D_EXPERT_STEPS
        nslot = (b + 1) % 2

        def move(r):
            pltpu.make_async_copy(slab(h3_ref, src_next_ref[0, r]), slab(x3.at[nslot], r * SLAB_PITCH),
                                  gsem.at[nslot]).start()
            scatter_copy(dst_prev_ref, r).start()

        for i in range(per_step):
            move(j * per_step + i)

        @pl.when(j == 0)
        def _():
            for r in range(per_step * D_EXPERT_STEPS, tm):
                move(r)

        acc[...] += _swiglu_part(xb[...], wg_ref[...], wu_ref[...], wd_ref[...])

    @pl.when(jnp.logical_and(j == nj - 1, used))
    def _():
        wait_scatter()
        for s in range(ROW_CHUNKS):
            o3[pl.ds(s, tm, stride=SLAB_PITCH), :] = acc[:, s * HEAD_W:(s + 1) * HEAD_W]

        @pl.when(b == nb - 1)
        def _():
            flush_scatter(dst_ref)
            wait_gather((b + 1) % 2)


def _moe_ffn(h3, src_idx, dst_idx, dst_prev_idx, blk_expert, n_used, wg, wu, wd):
    seq = h3.shape[0] // ROW_CHUNKS
    n_blocks = src_idx.shape[0]
    tm, tf = MOE_TILE, MOE_FF_TILE
    nj = D_EXPERT // tf

    def jj(b, j, be, nu):
        return jnp.where(b < nu[0], j, nj - 1)

    def idx_spec(shift):
        return pl.BlockSpec((None, 1, tm), lambda b, j, be, nu: (jnp.clip(b + shift, 0, n_blocks - 1), 0, 0),
                            memory_space=pltpu.SMEM)

    return pl.pallas_call(
        _moe_ffn_kernel,
        out_shape=jax.ShapeDtypeStruct(((2 * seq + tm) * ROW_CHUNKS, HEAD_W), F32),
        grid_spec=pltpu.PrefetchScalarGridSpec(
            num_scalar_prefetch=2,
            grid=(n_blocks, nj),
            in_specs=[idx_spec(0), idx_spec(1), idx_spec(0), idx_spec(0),
                      pl.BlockSpec(memory_space=pl.ANY),
                      pl.BlockSpec((None, D_MODEL, tf), lambda b, j, be, nu: (be[b], 0, jj(b, j, be, nu))),
                      pl.BlockSpec((None, D_MODEL, tf), lambda b, j, be, nu: (be[b], 0, jj(b, j, be, nu))),
                      pl.BlockSpec((None, tf, D_MODEL), lambda b, j, be, nu: (be[b], jj(b, j, be, nu), 0))],
            out_specs=pl.BlockSpec(memory_space=pl.ANY),
            scratch_shapes=[pltpu.VMEM((2, tm * SLAB_PITCH, HEAD_W), F32),
                            pltpu.VMEM((tm, D_MODEL), BF16),
                            pltpu.VMEM((tm, D_MODEL), F32),
                            pltpu.VMEM((tm * SLAB_PITCH, HEAD_W), F32),
                            pltpu.SemaphoreType.DMA((2,)),
                            pltpu.SemaphoreType.DMA(())]),
        compiler_params=_cparams(("arbitrary", "arbitrary")),
        name="moe_ffn",
    )(blk_expert, n_used, src_idx, src_idx, dst_idx, dst_prev_idx, h3, wg, wu, wd)


def _combine_kernel(x_ref, ya_ref, yb_ref, route_ref, o_ref):
    r = route_ref[...]
    g1 = jnp.broadcast_to(r[:, 2:3], (r.shape[0], HEAD_W))
    g2 = jnp.broadcast_to(r[:, 3:4], (r.shape[0], HEAD_W))
    tm = r.shape[0]
    for s in range(ROW_CHUNKS):
        cols = slice(s * HEAD_W, (s + 1) * HEAD_W)
        rows = pl.ds(s, tm, stride=ROW_CHUNKS)
        o_ref[:, cols] = x_ref[:, cols] + g1 * ya_ref[rows, :] + g2 * yb_ref[rows, :]


def _moe_combine(x1, y, route):
    seq = x1.shape[0]
    tm = ROW_TILE
    nb = seq // tm
    big = pl.BlockSpec((tm, D_MODEL), lambda i: (i, 0))
    return pl.pallas_call(
        _combine_kernel,
        out_shape=jax.ShapeDtypeStruct((seq, D_MODEL), F32),
        grid=(nb,),
        in_specs=[big,
                  pl.BlockSpec((tm * ROW_CHUNKS, HEAD_W), lambda i: (i, 0)),
                  pl.BlockSpec((tm * ROW_CHUNKS, HEAD_W), lambda i: (i + nb, 0)),
                  pl.BlockSpec((tm, HEAD_W), lambda i: (i, 0))],
        out_specs=big,
        compiler_params=_cparams(("arbitrary",)),
        name="moe_combine",
    )(x1, y, y, route)


def _moe_plan(route, seq):
    tm = MOE_TILE
    n_assign = 2 * seq
    n_blocks = n_assign // tm + N_EXPERTS
    e = route[:, 0:2].astype(jnp.int32).reshape(-1)
    onehot = (e[:, None] == jnp.arange(N_EXPERTS, dtype=jnp.int32)[None, :]).astype(jnp.int32)
    csum = jnp.cumsum(onehot, axis=0)
    counts = csum[-1]
    rank = jnp.sum((csum - onehot) * onehot, axis=1)
    nblk = (counts + tm - 1) // tm
    cum_blk = jnp.cumsum(nblk)
    pad_start = (cum_blk - nblk) * tm
    pos = pad_start[e] + rank
    assign = jnp.arange(n_assign, dtype=jnp.int32)
    slot_assign = jnp.full((n_blocks * tm,), -1, jnp.int32).at[pos].set(
        assign, unique_indices=True, mode='promise_in_bounds')
    real = slot_assign >= 0
    tok = jnp.maximum(slot_assign, 0) // 2
    in_block = jnp.arange(n_blocks * tm, dtype=jnp.int32) % tm
    src_idx = tok * ROW_CHUNKS
    dst_idx = jnp.where(real, (slot_assign % 2) * seq + tok, n_assign + in_block) * ROW_CHUNKS
    src_idx = src_idx.reshape(n_blocks, 1, tm)
    dst_idx = dst_idx.reshape(n_blocks, 1, tm)
    scratch_blk = ((n_assign + jnp.arange(tm, dtype=jnp.int32)) * ROW_CHUNKS).reshape(1, 1, tm)
    dst_prev_idx = jnp.concatenate([scratch_blk, dst_idx[:-1]], axis=0)
    n_used = cum_blk[-1].astype(jnp.int32)
    blk = jnp.minimum(jnp.arange(n_blocks, dtype=jnp.int32), n_used - 1)
    blk_expert = jnp.sum((cum_blk[None, :] <= blk[:, None]).astype(jnp.int32), axis=1)
    blk_expert = jnp.minimum(blk_expert, N_EXPERTS - 1)
    return src_idx, dst_idx, dst_prev_idx, blk_expert, n_used.reshape(1)


def _moe(h3, x1, route, wg, wu, wd):
    seq = x1.shape[0]
    src_idx, dst_idx, dst_prev_idx, blk_expert, n_used = _moe_plan(route, seq)
    y = _moe_ffn(h3, src_idx, dst_idx, dst_prev_idx, blk_expert, n_used, wg, wu, wd)
    return _moe_combine(x1, y, route)


def _layer(x, i, a_norm, w_in, dqn, dkn, lq1, lk1, lq2, lk2, don, bqn, bkn, bon, w_out, f_norm, ffn, riders=()):
    seq = x.shape[0]
    lam_init = 0.8 - 0.6 * math.exp(-0.3 * i)
    gains, gmat, cos_t, sin_t = _segment_tables(seq, dqn, dkn, bqn, bkn)
    qaT, ka, vaT, qbT, kb, vbT = _inproj(x, a_norm, _permute_w_in(w_in), gains, gmat, cos_t, sin_t)
    out_a, casted = _diff_attn(qaT, ka, vaT, lq1, lk1, lq2, lk2, don, gmat[0, :HEAD_W, :HEAD_W], lam_init, riders)
    out_b = _dil_attn(qbT, kb, vbT, bon, gmat[1, :HEAD_W, :HEAD_W])
    if len(ffn) == 3:
        x1, h = _outproj(out_a, out_b, w_out, x, f_norm)
        return _dense_ffn(h, x1, *ffn), casted
    router, wg, wu, wd = ffn
    x1, h3, route = _outproj(out_a, out_b, w_out, x, f_norm, router)
    return _moe(h3, x1, route, wg, wu, wd), casted


def kernel(x, attn_norm_0, w_in_0, diff_q_norm_0, diff_k_norm_0, diff_lam_q1_0, diff_lam_k1_0, diff_lam_q2_0, diff_lam_k2_0, diff_out_norm_0, dil_q_norm_0, dil_k_norm_0, dil_out_norm_0, w_out_0, ffn_norm_0, ffn_w_gate_0, ffn_w_up_0, ffn_w_down_0, attn_norm_1, w_in_1, diff_q_norm_1, diff_k_norm_1, diff_lam_q1_1, diff_lam_k1_1, diff_lam_q2_1, diff_lam_k2_1, diff_out_norm_1, dil_q_norm_1, dil_k_norm_1, dil_out_norm_1, w_out_1, ffn_norm_1, router_1, moe_w_gate_1, moe_w_up_1, moe_w_down_1):
    b, seq, d = x.shape
    assert b == 1 and d == D_MODEL and seq % (DIL_TQ * DIL_SUBTILES) == 0 and seq % ATT_TQ == 0
    assert MOE_TILE == ROW_TILE and DIL_PAD % ROW_TILE == 0
    xs = x.reshape(seq, d)
    xs, moe_w = _layer(xs, 0, attn_norm_0, w_in_0, diff_q_norm_0, diff_k_norm_0, diff_lam_q1_0, diff_lam_k1_0,
                       diff_lam_q2_0, diff_lam_k2_0, diff_out_norm_0, dil_q_norm_0, dil_k_norm_0, dil_out_norm_0,
                       w_out_0, ffn_norm_0, (ffn_w_gate_0, ffn_w_up_0, ffn_w_down_0),
                       riders=(moe_w_gate_1, moe_w_up_1, moe_w_down_1))
    xs, _ = _layer(xs, 1, attn_norm_1, w_in_1, diff_q_norm_1, diff_k_norm_1, diff_lam_q1_1, diff_lam_k1_1,
                   diff_lam_q2_1, diff_lam_k2_1, diff_out_norm_1, dil_q_norm_1, dil_k_norm_1, dil_out_norm_1,
                   w_out_1, ffn_norm_1, (router_1, *moe_w))
    return xs.reshape(b, seq, d)
```

```python
import functools
import math

import numpy as np
import jax
import jax.numpy as jnp
from jax import lax
from jax.experimental import pallas as pl
from jax.experimental.pallas import tpu as pltpu

F32 = jnp.float32
BF16 = jnp.bfloat16

D_MODEL = 2048
N_HEADS = 8
HEAD_W = 128
DIFF_DIM = 64
SEG_W = N_HEADS * HEAD_W
N_SEG = 6
ROW_CHUNKS = D_MODEL // HEAD_W
SLAB_PITCH = 24
DIL_BRANCHES = ((128, 1), (512, 4), (2048, 16))
N_SIDE = 64
D_FF = 5632
N_EXPERTS = 8
D_EXPERT = 7168
EPS = 1e-6
NEG = -1e30
LOG2E = 1.4426950408889634

MAP0_LANE = 0
MAP1_LANE = 32
ROW_SUM_FLOOR = 2.0 ** -80

V_ROWS = 144
DIL_PAD = 1024
V7X_VMEM_BYTES = 64 * 1024 * 1024
VMEM_LIMIT = V7X_VMEM_BYTES * 7 // 8

ROW_TILE = 512
ATT_TQ = 512
ATT_TK = 8192
DIL_TQ = 256
DIL_SUBTILES = 8
DIL_CHUNK = 768
FF_TILE = 512
D_EXPERT_STEPS = 7
MOE_FF_TILE = D_EXPERT // D_EXPERT_STEPS
MOE_TILE = 512


def _cparams(sem):
    return pltpu.CompilerParams(dimension_semantics=sem, vmem_limit_bytes=VMEM_LIMIT)


def _diff_lane_perm():
    perm = np.zeros(HEAD_W, np.int32)
    for m in range(2):
        for t in range(DIFF_DIM):
            p = (t // 32) * 64 + m * 32 + (t % 32)
            perm[p] = m * DIFF_DIM + t
    return perm


def _segment_tables(seq, dqn, dkn, bqn, bkn):
    perm = _diff_lane_perm()
    t_of_lane = perm % DIFF_DIM
    g_qa = dqn.astype(F32)[t_of_lane]
    g_ka = jnp.tile(dkn.astype(F32)[t_of_lane], N_HEADS)
    g_qb = bqn.astype(F32)
    g_kb = jnp.tile(bkn.astype(F32), N_HEADS)
    gains = jnp.stack([g_ka, g_kb]).reshape(2, 1, SEG_W)

    lane = np.arange(HEAD_W)
    map_of_lane = (lane // 32) % 2
    g_diff = (map_of_lane[:, None] == map_of_lane[None, :]).astype(np.float32)
    g_dil = np.ones((HEAD_W, HEAD_W), np.float32)
    pair = np.eye(2, dtype=np.float32)
    gmat = jnp.asarray(np.stack([np.kron(pair, g_diff), np.kron(pair, g_dil)]), BF16)

    pos = jnp.arange(seq, dtype=F32)

    def ang(dim):
        inv = 10000.0 ** (-jnp.arange(0, dim, 2, dtype=F32) / dim)
        return pos[:, None] * inv[None, :]

    a32 = ang(DIFF_DIM)
    a64 = ang(HEAD_W)
    cos_a = jnp.tile(jnp.cos(a32), (1, 4))
    sin_a = jnp.tile(jnp.sin(a32), (1, 4))
    cos_b = jnp.tile(jnp.cos(a64), (1, 2))
    sin_b = jnp.tile(jnp.sin(a64), (1, 2))
    sign = jnp.where(jnp.arange(HEAD_W) < 64, -1.0, 1.0).astype(F32)[None, :]
    cos_t = jnp.stack([cos_a, cos_b])
    sin_t = jnp.stack([sin_a * sign, sin_b * sign])

    def q_tables(g, cos, sin_signed, scale):
        return (g[:, None] * cos.T * scale), (jnp.roll(g, 64)[:, None] * sin_signed.T * scale)

    qc_a, qs_a = q_tables(g_qa, cos_a, sin_a * sign, DIFF_DIM ** -0.5 * LOG2E)
    qc_b, qs_b = q_tables(g_qb, cos_b, sin_b * sign, HEAD_W ** -0.5 * LOG2E)
    return gains, gmat, cos_t, sin_t, jnp.stack([qc_a, qc_b]), jnp.stack([qs_a, qs_b])


def _split_w_in(w_in):
    w = w_in.astype(BF16)
    seg = [w[:, s * SEG_W:(s + 1) * SEG_W] for s in range(N_SEG)]

    def perm(blk):
        return blk.reshape(D_MODEL, N_HEADS, 2, 2, 32).transpose(0, 1, 3, 2, 4).reshape(D_MODEL, SEG_W)

    w_k = jnp.concatenate([perm(seg[1]), seg[4]], axis=1)
    w_t = jnp.concatenate([perm(seg[0]), seg[2], seg[3], seg[5]], axis=1).T
    return w_k, w_t


INPROJ_STEPS = ("ka", "kb", "qa", "va", "qb", "vb")


def _inproj_kernel(x_ref, g_ref, wk_ref, wt_ref, gain_ref, cos_ref, sin_ref, gmat_ref, qcos_ref, qsin_ref,
                   qaT_ref, ka_ref, vaT_ref, qbT_ref, kb_ref, vbT_ref, h_scr,
                   *, n_row_blocks, pad_blocks):
    ip = pl.program_id(0)
    j = pl.program_id(1)
    real = jnp.logical_and(ip >= pad_blocks, ip < pad_blocks + n_row_blocks)
    tm = x_ref.shape[0]

    @pl.when(jnp.logical_and(real, j == 0))
    def _():
        x = x_ref[...]
        ms = jnp.mean(x * x, axis=-1, keepdims=True)
        h_scr[...] = (x * lax.rsqrt(ms + EPS) * g_ref[...]).astype(BF16)

    def k_segment(o_ref, n_group):
        acc = jnp.dot(h_scr[...], wk_ref[...], preferred_element_type=F32)
        for c in range(0, N_HEADS, 2):
            pair = acc[:, c * HEAD_W:(c + 2) * HEAD_W]
            ss2 = jnp.dot((pair * pair).astype(BF16), gmat_ref[...], preferred_element_type=F32)
            for h in (c, c + 1):
                cols = slice(h * HEAD_W, (h + 1) * HEAD_W)
                ss = ss2[:, (h - c) * HEAD_W:(h - c + 1) * HEAD_W]
                yn = acc[:, cols] * lax.rsqrt(ss * (1.0 / n_group) + EPS) * gain_ref[:, cols]
                o_ref[:, cols] = (yn * cos_ref[...] + pltpu.roll(yn, 64, 1) * sin_ref[...]).astype(BF16)

    def transposed_segment(o_ref, kind):
        accT = lax.dot_general(wt_ref[...], h_scr[...], (((1,), (1,)), ((), ())),
                               preferred_element_type=F32)
        for c in range(N_HEADS):
            yT = accT[c * HEAD_W:(c + 1) * HEAD_W, :]
            if kind == "v":
                o_ref[c, 0:HEAD_W, :] = yT.astype(BF16)
                continue
            y2 = yT * yT
            if kind == "qa":
                q = [jnp.sum(y2[a * 32:(a + 1) * 32, :], axis=0, keepdims=True) for a in range(4)]
                r0 = lax.rsqrt((q[0] + q[2]) * (1.0 / DIFF_DIM) + EPS)
                r1 = lax.rsqrt((q[1] + q[3]) * (1.0 / DIFF_DIM) + EPS)
                r = jnp.concatenate([jnp.broadcast_to(v, (32, tm)) for v in (r0, r1, r0, r1)], axis=0)
            else:
                r = lax.rsqrt(jnp.sum(y2, axis=0, keepdims=True) * (1.0 / HEAD_W) + EPS)
            swapped = jnp.concatenate([yT[64:HEAD_W, :], yT[0:64, :]], axis=0)
            o_ref[c] = ((yT * qcos_ref[...] + swapped * qsin_ref[...]) * r).astype(BF16)

    def step(name):
        if name == "ka":
            k_segment(ka_ref, DIFF_DIM)
        elif name == "kb":
            k_segment(kb_ref, HEAD_W)
        elif name == "qa":
            transposed_segment(qaT_ref, "qa")
        elif name == "qb":
            transposed_segment(qbT_ref, "qb")
        elif name == "va":
            transposed_segment(vaT_ref, "v")
        else:
            transposed_segment(vbT_ref, "v")
            row = lax.broadcasted_iota(jnp.int32, (V_ROWS - HEAD_W, tm), 0)
            ones_row = jnp.where(row == 0, 1.0, 0.0).astype(BF16)
            for c in range(N_HEADS):
                vbT_ref[c, HEAD_W:V_ROWS, :] = ones_row

    for s, name in enumerate(INPROJ_STEPS):
        pl.when(jnp.logical_and(real, j == s))(functools.partial(step, name))

    @pl.when(jnp.logical_and(jnp.logical_not(real), j == INPROJ_STEPS.index("kb")))
    def _():
        kb_ref[...] = jnp.zeros(kb_ref.shape, BF16)

    @pl.when(jnp.logical_and(jnp.logical_not(real), j == INPROJ_STEPS.index("vb")))
    def _():
        vbT_ref[...] = jnp.zeros(vbT_ref.shape, BF16)


def _inproj(x, a_norm, w_k, w_t, gains, gmat, cos_t, sin_t, qcos_t, qsin_t):
    seq = x.shape[0]
    tm = ROW_TILE
    nrb = seq // tm
    pb = DIL_PAD // tm
    seq_p = seq + 2 * DIL_PAD

    def row(ip):
        return jnp.clip(ip - pb, 0, nrb - 1)

    kern = functools.partial(_inproj_kernel, n_row_blocks=nrb, pad_blocks=pb)
    out_shape = (
        jax.ShapeDtypeStruct((N_HEADS, HEAD_W, seq), BF16),
        jax.ShapeDtypeStruct((seq, SEG_W), BF16),
        jax.ShapeDtypeStruct((N_HEADS, HEAD_W, seq), BF16),
        jax.ShapeDtypeStruct((N_HEADS, HEAD_W, seq), BF16),
        jax.ShapeDtypeStruct((seq_p, SEG_W), BF16),
        jax.ShapeDtypeStruct((N_HEADS, V_ROWS, seq_p), BF16),
    )
    def k_group(j):
        return jnp.minimum(j, 1)

    def t_block(j):
        return jnp.clip(j - 2, 0, 3)

    def q_group(j):
        return jnp.where(j >= INPROJ_STEPS.index("qb"), 1, 0)

    in_specs = [
        pl.BlockSpec((tm, D_MODEL), lambda ip, j: (row(ip), 0)),
        pl.BlockSpec((1, D_MODEL), lambda ip, j: (0, 0)),
        pl.BlockSpec((D_MODEL, SEG_W), lambda ip, j: (0, k_group(j))),
        pl.BlockSpec((SEG_W, D_MODEL), lambda ip, j: (t_block(j), 0)),
        pl.BlockSpec((None, 1, SEG_W), lambda ip, j: (k_group(j), 0, 0)),
        pl.BlockSpec((None, tm, HEAD_W), lambda ip, j: (k_group(j), row(ip), 0)),
        pl.BlockSpec((None, tm, HEAD_W), lambda ip, j: (k_group(j), row(ip), 0)),
        pl.BlockSpec((None, 2 * HEAD_W, 2 * HEAD_W), lambda ip, j: (k_group(j), 0, 0)),
        pl.BlockSpec((None, HEAD_W, tm), lambda ip, j: (q_group(j), 0, row(ip))),
        pl.BlockSpec((None, HEAD_W, tm), lambda ip, j: (q_group(j), 0, row(ip))),
    ]
    out_specs = (
        pl.BlockSpec((N_HEADS, HEAD_W, tm), lambda ip, j: (0, 0, row(ip))),
        pl.BlockSpec((tm, SEG_W), lambda ip, j: (row(ip), 0)),
        pl.BlockSpec((N_HEADS, HEAD_W, tm), lambda ip, j: (0, 0, row(ip))),
        pl.BlockSpec((N_HEADS, HEAD_W, tm), lambda ip, j: (0, 0, row(ip))),
        pl.BlockSpec((tm, SEG_W), lambda ip, j: (ip, 0)),
        pl.BlockSpec((N_HEADS, V_ROWS, tm), lambda ip, j: (0, 0, ip)),
    )
    return pl.pallas_call(
        kern,
        out_shape=out_shape,
        grid=(nrb + 2 * pb, N_SEG),
        in_specs=in_specs,
        out_specs=out_specs,
        scratch_shapes=[pltpu.VMEM((tm, D_MODEL), BF16)],
        compiler_params=_cparams(("arbitrary", "arbitrary")),
        name="inproj",
    )(x, a_norm.reshape(1, D_MODEL).astype(F32), w_k, w_t, gains, cos_t, sin_t, gmat, qcos_t, qsin_t)


def _flash_step(s, m_old, acc_ref, vT_t):
    m_new = jnp.maximum(m_old, jnp.max(s, axis=0, keepdims=True))
    alpha = jnp.exp2(m_old - m_new)
    p = jnp.exp2((s - m_new).astype(BF16))
    acc_ref[...] = acc_ref[...] * alpha + jnp.dot(vT_t, p, preferred_element_type=F32)
    return m_new


def _flash_step_sum(s, m_old, l_old, acc_ref, vT_t):
    m_new = jnp.maximum(m_old, jnp.max(s, axis=0, keepdims=True))
    alpha = jnp.exp2(m_old - m_new)
    e = jnp.exp2(s - m_new)
    acc_ref[...] = acc_ref[...] * alpha + jnp.dot(vT_t, e.astype(BF16), preferred_element_type=F32)
    return m_new, l_old * alpha + jnp.sum(e, axis=0, keepdims=True)


def _head_out(aT, gain_row, out_scale):
    a = aT.T
    ms = jnp.mean(a * a, axis=-1, keepdims=True)
    y = a * lax.rsqrt(ms + EPS) * gain_row
    if out_scale != 1.0:
        y = y * out_scale
    return y


def _diff_attn_kernel(*refs, lam_init, tk, n_riders):
    lq1_ref, lk1_ref, lq2_ref, lk2_ref, qT_ref, k_ref, vT_ref, og_ref, gmat_ref = refs[:9]
    rider_in = refs[9:9 + n_riders]
    o_ref = refs[9 + n_riders]
    rider_out = refs[10 + n_riders:10 + 2 * n_riders]
    acc0, acc1, lsum, k0_scr, k1_scr, kmax_scr = refs[10 + 2 * n_riders:]
    tq = qT_ref.shape[1]
    seq = k_ref.shape[0]
    lane0, lane1 = MAP1_LANE, MAP0_LANE

    @pl.when(pl.program_id(1) == 0)
    def _():
        ck = min(1024, seq)

        def kchunk(c, mx):
            rows = pl.ds(pl.multiple_of(c * ck, ck), ck)
            kc = k_ref[rows, :]
            kf = kc.astype(F32)
            n2 = jnp.dot((kf * kf).astype(BF16), gmat_ref[...], preferred_element_type=F32)
            lane = lax.broadcasted_iota(jnp.int32, kc.shape, 1)
            one = jnp.ones_like(kc)
            k0_scr[rows, :] = jnp.where(lane == lane0, one, kc)
            k1_scr[rows, :] = jnp.where(lane == lane1, one, kc)
            return jnp.maximum(mx, jnp.max(n2, axis=0, keepdims=True))

        kn2 = lax.fori_loop(0, seq // ck, kchunk, jnp.zeros((1, HEAD_W), F32))
        kmax_scr[...] = jnp.sqrt(kn2)

    qT = qT_ref[...]
    row = lax.broadcasted_iota(jnp.int32, qT.shape, 0)
    in_map1 = ((row // 32) % 2) == 1
    zero = jnp.zeros_like(qT)
    qf = qT.astype(F32)
    q2 = qf * qf
    nq0 = jnp.sum(jnp.where(in_map1, 0.0, q2), axis=0, keepdims=True)
    nq1 = jnp.sum(jnp.where(in_map1, q2, 0.0), axis=0, keepdims=True)
    kmax = kmax_scr[...]
    b0 = jnp.sqrt(nq0) * kmax[:, MAP0_LANE:MAP0_LANE + 1]
    b1 = jnp.sqrt(nq1) * kmax[:, MAP1_LANE:MAP1_LANE + 1]
    q0 = jnp.where(in_map1, zero, qT)
    q1 = jnp.where(in_map1, qT, zero)
    q0s = jnp.where(row == lane0, jnp.broadcast_to(-b0, qf.shape).astype(BF16), q0)
    q1s = jnp.where(row == lane1, jnp.broadcast_to(-b1, qf.shape).astype(BF16), q1)
    acc0[...] = jnp.zeros(acc0.shape, F32)
    acc1[...] = jnp.zeros(acc1.shape, F32)

    def body(kt, carry):
        l0, l1 = carry
        rows = pl.ds(pl.multiple_of(kt * tk, tk), tk)
        vT_t = vT_ref[:, rows]
        e0 = jnp.exp2(jnp.dot(k0_scr[rows, :], q0s, preferred_element_type=F32))
        acc0[...] += jnp.dot(vT_t, e0.astype(BF16), preferred_element_type=F32)
        e1 = jnp.exp2(jnp.dot(k1_scr[rows, :], q1s, preferred_element_type=F32))
        acc1[...] += jnp.dot(vT_t, e1.astype(BF16), preferred_element_type=F32)
        for w_ref, wb_ref in zip(rider_in, rider_out):
            r = w_ref.shape[0] // (seq // tk)
            part = pl.ds(pl.multiple_of(kt * r, 16), r)
            wb_ref[part, :] = w_ref[part, :].astype(BF16)
        return l0 + jnp.sum(e0, axis=0, keepdims=True), l1 + jnp.sum(e1, axis=0, keepdims=True)

    l_init = jnp.zeros((1, tq), F32)
    l0, l1 = lax.fori_loop(0, seq // tk, body, (l_init, l_init))
    lsum[0:1, :] = l0
    lsum[1:2, :] = l1

    @pl.when(jnp.logical_not(jnp.min(jnp.minimum(l0, l1)) >= ROW_SUM_FLOOR))
    def _():
        acc0[...] = jnp.zeros(acc0.shape, F32)
        acc1[...] = jnp.zeros(acc1.shape, F32)

        def robust(kt, carry):
            m0, l0, m1, l1 = carry
            rows = pl.ds(pl.multiple_of(kt * tk, tk), tk)
            k_t = k_ref[rows, :]
            vT_t = vT_ref[:, rows]
            m0, l0 = _flash_step_sum(jnp.dot(k_t, q0, preferred_element_type=F32), m0, l0, acc0, vT_t)
            m1, l1 = _flash_step_sum(jnp.dot(k_t, q1, preferred_element_type=F32), m1, l1, acc1, vT_t)
            return m0, l0, m1, l1

        m_init = jnp.full((1, tq), NEG, F32)
        _, l0, _, l1 = lax.fori_loop(0, seq // tk, robust, (m_init, l_init, m_init, l_init))
        lsum[0:1, :] = l0
        lsum[1:2, :] = l1

    lam = (jnp.exp(jnp.sum(lq1_ref[...] * lk1_ref[...], axis=-1, keepdims=True))
           - jnp.exp(jnp.sum(lq2_ref[...] * lk2_ref[...], axis=-1, keepdims=True)) + lam_init)
    o0 = acc0[...] * (1.0 / lsum[0:1, :])
    o1 = acc1[...] * (1.0 / lsum[1:2, :])
    aT = o0 - lam * o1
    o_ref[...] = _head_out(aT, og_ref[...], 1.0 - lam_init).astype(o_ref.dtype)


def _diff_attn(qaT, ka, vaT, lq1, lk1, lq2, lk2, og, gmat_diff, lam_init, riders=()):
    seq = ka.shape[0]
    tq = min(ATT_TQ, seq)
    tk = min(ATT_TK, seq)
    nq = seq // tq
    n_steps = N_HEADS * nq
    kern = functools.partial(_diff_attn_kernel, lam_init=lam_init, tk=tk, n_riders=len(riders))
    vec = lambda v: v.reshape(1, -1).astype(F32)
    small = pl.BlockSpec((1, DIFF_DIM), lambda h, qi: (0, 0))
    once = pl.Buffered(1)
    rider_specs = []
    for w in riders:
        n_e, n_r, n_c = w.shape
        per_e = n_steps // n_e
        rider_specs.append(pl.BlockSpec(
            (None, n_r // per_e, n_c),
            lambda h, qi, per_e=per_e: ((h * nq + qi) // per_e, (h * nq + qi) % per_e, 0)))
    outs = pl.pallas_call(
        kern,
        out_shape=(jax.ShapeDtypeStruct((seq, SEG_W), BF16),
                   *[jax.ShapeDtypeStruct(w.shape, BF16) for w in riders]),
        grid=(N_HEADS, nq),
        in_specs=[small, small, small, small,
                  pl.BlockSpec((None, HEAD_W, tq), lambda h, qi: (h, 0, qi)),
                  pl.BlockSpec((seq, HEAD_W), lambda h, qi: (0, h), pipeline_mode=once),
                  pl.BlockSpec((None, HEAD_W, seq), lambda h, qi: (h, 0, 0), pipeline_mode=once),
                  pl.BlockSpec((1, HEAD_W), lambda h, qi: (0, 0)),
                  pl.BlockSpec((HEAD_W, HEAD_W), lambda h, qi: (0, 0)),
                  *rider_specs],
        out_specs=(pl.BlockSpec((tq, HEAD_W), lambda h, qi: (qi, h)), *rider_specs),
        scratch_shapes=[pltpu.VMEM((HEAD_W, tq), F32), pltpu.VMEM((HEAD_W, tq), F32),
                        pltpu.VMEM((8, tq), F32),
                        pltpu.VMEM((seq, HEAD_W), BF16), pltpu.VMEM((seq, HEAD_W), BF16),
                        pltpu.VMEM((1, HEAD_W), F32)],
        compiler_params=_cparams(("arbitrary", "arbitrary")),
        name="diff_attn",
    )(vec(lq1), vec(lk1), vec(lq2), vec(lk2), qaT, ka, vaT, vec(og), gmat_diff, *riders)
    return outs[0], tuple(outs[1:])


def _dil_chunks(tq):
    chunks, off = [], 0
    for _, dil in DIL_BRANCHES:
        pad = -(-(N_SIDE * dil) // 128) * 128
        total = tq + 2 * pad
        c0 = 0
        while c0 < total:
            nk = min(DIL_CHUNK, total - c0)
            chunks.append((dil, c0 - pad, nk, off))
            off += nk
            c0 += nk
    return chunks, off


def _dil_bias(tq):
    chunks, total = _dil_chunks(tq)
    bias = np.full((total, tq), NEG, np.float32)
    col = np.arange(tq)[None, :]
    for dil, rel, nk, off in chunks:
        delta = rel + np.arange(nk)[:, None] - col
        ok = (np.abs(delta) <= N_SIDE * dil) & (delta % dil == 0)
        bias[off:off + nk][ok] = 0.0
    return jnp.asarray(bias)


def _dil_attn_kernel(qT_ref, k_ref, vT_ref, bias_ref, og_ref, gmat_ref, o_ref, acc, kmax_scr, *, seq, chunks):
    tq = acc.shape[1]
    n_sub = qT_ref.shape[1] // tq
    base = pl.program_id(1) * (tq * n_sub)

    @pl.when(pl.program_id(1) == 0)
    def _():
        ck = 1024

        def kchunk(c, mx):
            kf = k_ref[pl.ds(pl.multiple_of(c * ck, ck), ck), :].astype(F32)
            n2 = jnp.dot((kf * kf).astype(BF16), gmat_ref[...], preferred_element_type=F32)
            return jnp.maximum(mx, jnp.max(n2, axis=0, keepdims=True))

        kn2 = lax.fori_loop(0, k_ref.shape[0] // ck, kchunk, jnp.zeros((1, HEAD_W), F32))
        kmax_scr[...] = jnp.sqrt(kn2)

    def windows(i0):
        for dil, rel, nk, off in chunks:
            start = pl.multiple_of(i0 + (DIL_PAD + rel), 128)
            yield rel, nk, off, k_ref[pl.ds(start, nk), :], vT_ref[:, pl.ds(start, nk)]

    def finish(t, num):
        oT = num[0:HEAD_W, :] * (1.0 / num[HEAD_W:HEAD_W + 1, :])
        o_ref[t * tq:(t + 1) * tq, :] = _head_out(oT, og_ref[...], 1.0).astype(o_ref.dtype)

    lmin = None
    for t in range(n_sub):
        qT = qT_ref[:, t * tq:(t + 1) * tq]
        qf = qT.astype(F32)
        shift = jnp.sqrt(jnp.sum(qf * qf, axis=0, keepdims=True)) * kmax_scr[:, 0:1]
        num = jnp.zeros((V_ROWS, tq), F32)
        for rel, nk, off, k_t, vT_t in windows(base + t * tq):
            s = jnp.dot(k_t, qT, preferred_element_type=F32) - shift + bias_ref[off:off + nk, :]
            num = num + jnp.dot(vT_t, jnp.exp2(s).astype(BF16), preferred_element_type=F32)
        finish(t, num)
        l_t = jnp.min(num[HEAD_W:HEAD_W + 1, :])
        lmin = l_t if lmin is None else jnp.minimum(lmin, l_t)

    @pl.when(jnp.logical_not(lmin >= ROW_SUM_FLOOR))
    def _():
        for t in range(n_sub):
            i0 = base + t * tq
            qT = qT_ref[:, t * tq:(t + 1) * tq]
            acc[...] = jnp.zeros(acc.shape, F32)
            m = jnp.full((1, tq), NEG, F32)
            for rel, nk, off, k_t, vT_t in windows(i0):
                s = jnp.dot(k_t, qT, preferred_element_type=F32)
                kpos = lax.broadcasted_iota(jnp.int32, (nk, tq), 0) + (i0 + rel)
                valid = jnp.logical_and(kpos >= 0, kpos < seq)
                s = jnp.where(valid, s + bias_ref[off:off + nk, :], NEG)
                m = _flash_step(s, m, acc, vT_t)
            finish(t, acc[...])


def _dil_attn(qbT, kb, vbT, og, gmat_ones):
    seq = qbT.shape[2]
    seq_p = kb.shape[0]
    tq = min(DIL_TQ, seq)
    n_sub = DIL_SUBTILES
    chunks, _ = _dil_chunks(tq)
    bias = _dil_bias(tq)
    kern = functools.partial(_dil_attn_kernel, seq=seq, chunks=chunks)
    return pl.pallas_call(
        kern,
        out_shape=jax.ShapeDtypeStruct((seq, SEG_W), BF16),
        grid=(N_HEADS, seq // (tq * n_sub)),
        in_specs=[pl.BlockSpec((None, HEAD_W, tq * n_sub), lambda h, qi: (h, 0, qi)),
                  pl.BlockSpec((seq_p, HEAD_W), lambda h, qi: (0, h)),
                  pl.BlockSpec((None, V_ROWS, seq_p), lambda h, qi: (h, 0, 0)),
                  pl.BlockSpec(bias.shape, lambda h, qi: (0, 0)),
                  pl.BlockSpec((1, HEAD_W), lambda h, qi: (0, 0)),
                  pl.BlockSpec((HEAD_W, HEAD_W), lambda h, qi: (0, 0))],
        out_specs=pl.BlockSpec((tq * n_sub, HEAD_W), lambda h, qi: (qi, h)),
        scratch_shapes=[pltpu.VMEM((V_ROWS, tq), F32), pltpu.VMEM((1, HEAD_W), F32)],
        compiler_params=_cparams(("arbitrary", "arbitrary")),
        name="dil_attn",
    )(qbT, kb, vbT, bias, og.reshape(1, HEAD_W).astype(F32), gmat_ones)


def _outproj_kernel(*refs, with_router):
    if with_router:
        a_ref, b_ref, w_ref, x_ref, g_ref, r_hi_ref, r_lo_ref, x1_ref, h_ref, route_ref = refs
    else:
        a_ref, b_ref, w_ref, x_ref, g_ref, x1_ref, h_ref = refs
    acc = (jnp.dot(a_ref[...], w_ref[0:SEG_W, :], preferred_element_type=F32)
           + jnp.dot(b_ref[...], w_ref[SEG_W:2 * SEG_W, :], preferred_element_type=F32))
    x1 = x_ref[...] + acc
    x1_ref[...] = x1
    ms = jnp.mean(x1 * x1, axis=-1, keepdims=True)
    hn = x1 * lax.rsqrt(ms + EPS) * g_ref[...]
    if not with_router:
        h_ref[...] = hn.astype(BF16)
    else:
        tm = hn.shape[0]
        for s in range(ROW_CHUNKS):
            h_ref[pl.ds(s, tm, stride=ROW_CHUNKS), :] = hn[:, s * HEAD_W:(s + 1) * HEAD_W]
        h_hi = hn.astype(BF16)
        h_lo = (hn - h_hi.astype(F32)).astype(BF16)
        logits = (jnp.dot(h_hi, r_hi_ref[...], preferred_element_type=F32)
                  + jnp.dot(h_hi, r_lo_ref[...], preferred_element_type=F32)
                  + jnp.dot(h_lo, r_hi_ref[...], preferred_element_type=F32))
        lane = lax.broadcasted_iota(jnp.int32, logits.shape, 1)
        ninf = jnp.float32(-jnp.inf)
        lg = jnp.where(lane < N_EXPERTS, logits, ninf)
        v1 = jnp.max(lg, axis=-1, keepdims=True)
        i1 = jnp.min(jnp.where(lg == v1, lane, HEAD_W), axis=-1, keepdims=True)
        lg2 = jnp.where(lane == i1, ninf, lg)
        v2 = jnp.max(lg2, axis=-1, keepdims=True)
        i2 = jnp.min(jnp.where(lg2 == v2, lane, HEAD_W), axis=-1, keepdims=True)
        g1 = 1.0 / (1.0 + jnp.exp(v2 - v1))
        g2 = 1.0 - g1
        route_ref[...] = jnp.where(lane == 0, i1.astype(F32),
                                   jnp.where(lane == 1, i2.astype(F32),
                                             jnp.where(lane == 2, g1, jnp.where(lane == 3, g2, 0.0))))


def _outproj(out_a, out_b, w_out, x, f_norm, router=None):
    seq = x.shape[0]
    tm = ROW_TILE
    with_router = router is not None
    in_specs = [pl.BlockSpec((tm, SEG_W), lambda i: (i, 0)),
                pl.BlockSpec((tm, SEG_W), lambda i: (i, 0)),
                pl.BlockSpec((D_MODEL, D_MODEL), lambda i: (0, 0)),
                pl.BlockSpec((tm, D_MODEL), lambda i: (i, 0)),
                pl.BlockSpec((1, D_MODEL), lambda i: (0, 0))]
    args = [out_a, out_b, w_out.astype(BF16), x, f_norm.reshape(1, D_MODEL).astype(F32)]
    out_shape = [jax.ShapeDtypeStruct((seq, D_MODEL), F32)]
    out_specs = [pl.BlockSpec((tm, D_MODEL), lambda i: (i, 0))]
    if not with_router:
        out_shape.append(jax.ShapeDtypeStruct((seq, D_MODEL), BF16))
        out_specs.append(pl.BlockSpec((tm, D_MODEL), lambda i: (i, 0)))
    else:
        out_shape.append(jax.ShapeDtypeStruct((seq * ROW_CHUNKS, HEAD_W), F32))
        out_specs.append(pl.BlockSpec((tm * ROW_CHUNKS, HEAD_W), lambda i: (i, 0)))
        r = jnp.zeros((D_MODEL, HEAD_W), F32).at[:, :N_EXPERTS].set(router.astype(F32))
        r_hi = r.astype(BF16)
        r_lo = (r - r_hi.astype(F32)).astype(BF16)
        in_specs += [pl.BlockSpec((D_MODEL, HEAD_W), lambda i: (0, 0))] * 2
        args += [r_hi, r_lo]
        out_shape.append(jax.ShapeDtypeStruct((seq, HEAD_W), F32))
        out_specs.append(pl.BlockSpec((tm, HEAD_W), lambda i: (i, 0)))
    return pl.pallas_call(
        functools.partial(_outproj_kernel, with_router=with_router),
        out_shape=tuple(out_shape),
        grid=(seq // tm,),
        in_specs=in_specs,
        out_specs=tuple(out_specs),
        compiler_params=_cparams(("arbitrary",)),
        name="outproj",
    )(*args)


def _swiglu_part(h, wg, wu, wd):
    g = jnp.dot(h, wg, preferred_element_type=F32)
    u = jnp.dot(h, wu, preferred_element_type=F32)
    act = (g * (1.0 / (1.0 + jnp.exp(-g))) * u).astype(BF16)
    return jnp.dot(act, wd, preferred_element_type=F32)


def _dense_ffn_kernel(h_ref, x_ref, wg_ref, wu_ref, wd_ref, o_ref):
    @pl.when(pl.program_id(1) == 0)
    def _():
        o_ref[...] = x_ref[...]

    o_ref[...] += _swiglu_part(h_ref[...], wg_ref[...], wu_ref[...], wd_ref[...])


def _dense_ffn(h, x1, wg, wu, wd):
    seq = x1.shape[0]
    tm, tf = ROW_TILE, FF_TILE
    return pl.pallas_call(
        _dense_ffn_kernel,
        out_shape=jax.ShapeDtypeStruct((seq, D_MODEL), F32),
        grid=(seq // tm, D_FF // tf),
        in_specs=[pl.BlockSpec((tm, D_MODEL), lambda i, j: (i, 0)),
                  pl.BlockSpec((tm, D_MODEL), lambda i, j: (i, 0)),
                  pl.BlockSpec((D_MODEL, tf), lambda i, j: (0, j)),
                  pl.BlockSpec((D_MODEL, tf), lambda i, j: (0, j)),
                  pl.BlockSpec((tf, D_MODEL), lambda i, j: (j, 0))],
        out_specs=pl.BlockSpec((tm, D_MODEL), lambda i, j: (i, 0)),
        compiler_params=_cparams(("arbitrary", "arbitrary")),
        name="dense_ffn",
    )(h, x1, wg.astype(BF16), wu.astype(BF16), wd.astype(BF16))


def _moe_ffn_kernel(be_ref, nused_ref, src_ref, src_next_ref, dst_ref, dst_prev_ref,
                    h3_ref, wg_ref, wu_ref, wd_ref, y_ref, x3, xb, acc, o3, gsem, ssem):
    b = pl.program_id(0)
    j = pl.program_id(1)
    nb = pl.num_programs(0)
    nj = pl.num_programs(1)
    tm = xb.shape[0]
    n_used = nused_ref[0]
    used = b < n_used

    def slab(ref, row0):
        return ref.at[pl.ds(pl.multiple_of(row0, 8), ROW_CHUNKS), :]

    def start_gather(idx_ref, slot):
        def start(r, c):
            pltpu.make_async_copy(slab(h3_ref, idx_ref[0, r]), slab(x3.at[slot], r * SLAB_PITCH),
                                  gsem.at[slot]).start()
            return c

        lax.fori_loop(0, tm, start, 0)

    def scatter_copy(idx_ref, r):
        return pltpu.make_async_copy(slab(o3, r * SLAB_PITCH), slab(y_ref, idx_ref[0, r]), ssem)

    def wait_gather(slot):
        n_rows = tm * ROW_CHUNKS
        pltpu.make_async_copy(h3_ref.at[pl.ds(0, n_rows), :], x3.at[slot, pl.ds(0, n_rows), :],
                              gsem.at[slot]).wait()

    def wait_scatter():
        n_rows = tm * ROW_CHUNKS
        pltpu.make_async_copy(o3.at[pl.ds(0, n_rows), :], y_ref.at[pl.ds(0, n_rows), :], ssem).wait()

    def flush_scatter(idx_ref):
        def start(r, c):
            scatter_copy(idx_ref, r).start()
            return c

        lax.fori_loop(0, tm, start, 0)
        wait_scatter()

    @pl.when(jnp.logical_and(j == 0, jnp.logical_and(b == 0, used)))
    def _():
        start_gather(src_ref, 0)
        o3[...] = jnp.zeros(o3.shape, F32)

    @pl.when(jnp.logical_and(j == 0, jnp.logical_and(b >= 1, b == n_used)))
    def _():
        flush_scatter(dst_prev_ref)

    @pl.when(jnp.logical_and(j == 0, jnp.where(b == 0, used, b - 1 < n_used)))
    def _():
        wait_gather(b % 2)

    @pl.when(jnp.logical_and(j == 0, used))
    def _():
        slot = b % 2
        for s in range(ROW_CHUNKS):
            xb[:, s * HEAD_W:(s + 1) * HEAD_W] = x3[slot, pl.ds(s, tm, stride=SLAB_PITCH), :].astype(BF16)
        acc[...] = jnp.zeros(acc.shape, F32)

    @pl.when(used)
    def _():
        per_step = tm // D_EXPERT_STEPS
        nslot = (b + 1) % 2

        def move(r):
            pltpu.make_async_copy(slab(h3_ref, src_next_ref[0, r]), slab(x3.at[nslot], r * SLAB_PITCH),
                                  gsem.at[nslot]).start()
            scatter_copy(dst_prev_ref, r).start()

        for i in range(per_step):
            move(j * per_step + i)

        @pl.when(j == 0)
        def _():
            for r in range(per_step * D_EXPERT_STEPS, tm):
                move(r)

        acc[...] += _swiglu_part(xb[...], wg_ref[...], wu_ref[...], wd_ref[...])

    @pl.when(jnp.logical_and(j == nj - 1, used))
    def _():
        wait_scatter()
        for s in range(ROW_CHUNKS):
            o3[pl.ds(s, tm, stride=SLAB_PITCH), :] = acc[:, s * HEAD_W:(s + 1) * HEAD_W]

        @pl.when(b == nb - 1)
        def _():
            flush_scatter(dst_ref)
            wait_gather((b + 1) % 2)


def _moe_ffn(h3, src_idx, dst_idx, dst_prev_idx, blk_expert, n_used, wg, wu, wd):
    seq = h3.shape[0] // ROW_CHUNKS
    n_blocks = src_idx.shape[0]
    tm, tf = MOE_TILE, MOE_FF_TILE
    nj = D_EXPERT // tf

    def jj(b, j, be, nu):
        return jnp.where(b < nu[0], j, nj - 1)

    def idx_spec(shift):
        return pl.BlockSpec((None, 1, tm), lambda b, j, be, nu: (jnp.clip(b + shift, 0, n_blocks - 1), 0, 0),
                            memory_space=pltpu.SMEM)

    return pl.pallas_call(
        _moe_ffn_kernel,
        out_shape=jax.ShapeDtypeStruct(((2 * seq + tm) * ROW_CHUNKS, HEAD_W), F32),
        grid_spec=pltpu.PrefetchScalarGridSpec(
            num_scalar_prefetch=2,
            grid=(n_blocks, nj),
            in_specs=[idx_spec(0), idx_spec(1), idx_spec(0), idx_spec(0),
                      pl.BlockSpec(memory_space=pl.ANY),
                      pl.BlockSpec((None, D_MODEL, tf), lambda b, j, be, nu: (be[b], 0, jj(b, j, be, nu))),
                      pl.BlockSpec((None, D_MODEL, tf), lambda b, j, be, nu: (be[b], 0, jj(b, j, be, nu))),
                      pl.BlockSpec((None, tf, D_MODEL), lambda b, j, be, nu: (be[b], jj(b, j, be, nu), 0))],
            out_specs=pl.BlockSpec(memory_space=pl.ANY),
            scratch_shapes=[pltpu.VMEM((2, tm * SLAB_PITCH, HEAD_W), F32),
                            pltpu.VMEM((tm, D_MODEL), BF16),
                            pltpu.VMEM((tm, D_MODEL), F32),
                            pltpu.VMEM((tm * SLAB_PITCH, HEAD_W), F32),
                            pltpu.SemaphoreType.DMA((2,)),
                            pltpu.SemaphoreType.DMA(())]),
        compiler_params=_cparams(("arbitrary", "arbitrary")),
        name="moe_ffn",
    )(blk_expert, n_used, src_idx, src_idx, dst_idx, dst_prev_idx, h3, wg, wu, wd)


def _combine_kernel(x_ref, ya_ref, yb_ref, route_ref, o_ref):
    r = route_ref[...]
    g1 = jnp.broadcast_to(r[:, 2:3], (r.shape[0], HEAD_W))
    g2 = jnp.broadcast_to(r[:, 3:4], (r.shape[0], HEAD_W))
    tm = r.shape[0]
    for s in range(ROW_CHUNKS):
        cols = slice(s * HEAD_W, (s + 1) * HEAD_W)
        rows = pl.ds(s, tm, stride=ROW_CHUNKS)
        o_ref[:, cols] = x_ref[:, cols] + g1 * ya_ref[rows, :] + g2 * yb_ref[rows, :]


def _moe_combine(x1, y, route):
    seq = x1.shape[0]
    tm = ROW_TILE
    nb = seq // tm
    big = pl.BlockSpec((tm, D_MODEL), lambda i: (i, 0))
    return pl.pallas_call(
        _combine_kernel,
        out_shape=jax.ShapeDtypeStruct((seq, D_MODEL), F32),
        grid=(nb,),
        in_specs=[big,
                  pl.BlockSpec((tm * ROW_CHUNKS, HEAD_W), lambda i: (i, 0)),
                  pl.BlockSpec((tm * ROW_CHUNKS, HEAD_W), lambda i: (i + nb, 0)),
                  pl.BlockSpec((tm, HEAD_W), lambda i: (i, 0))],
        out_specs=big,
        compiler_params=_cparams(("arbitrary",)),
        name="moe_combine",
    )(x1, y, y, route)


def _moe_plan(route, seq):
    tm = MOE_TILE
    n_assign = 2 * seq
    n_blocks = n_assign // tm + N_EXPERTS
    e = route[:, 0:2].astype(jnp.int32).reshape(-1)
    onehot = (e[:, None] == jnp.arange(N_EXPERTS, dtype=jnp.int32)[None, :]).astype(jnp.int32)
    csum = jnp.cumsum(onehot, axis=0)
    counts = csum[-1]
    rank = jnp.sum((csum - onehot) * onehot, axis=1)
    nblk = (counts + tm - 1) // tm
    cum_blk = jnp.cumsum(nblk)
    pad_start = (cum_blk - nblk) * tm
    pos = pad_start[e] + rank
    assign = jnp.arange(n_assign, dtype=jnp.int32)
    slot_assign = jnp.full((n_blocks * tm,), -1, jnp.int32).at[pos].set(
        assign, unique_indices=True, mode='promise_in_bounds')
    real = slot_assign >= 0
    tok = jnp.maximum(slot_assign, 0) // 2
    in_block = jnp.arange(n_blocks * tm, dtype=jnp.int32) % tm
    src_idx = tok * ROW_CHUNKS
    dst_idx = jnp.where(real, (slot_assign % 2) * seq + tok, n_assign + in_block) * ROW_CHUNKS
    src_idx = src_idx.reshape(n_blocks, 1, tm)
    dst_idx = dst_idx.reshape(n_blocks, 1, tm)
    scratch_blk = ((n_assign + jnp.arange(tm, dtype=jnp.int32)) * ROW_CHUNKS).reshape(1, 1, tm)
    dst_prev_idx = jnp.concatenate([scratch_blk, dst_idx[:-1]], axis=0)
    n_used = cum_blk[-1].astype(jnp.int32)
    blk = jnp.minimum(jnp.arange(n_blocks, dtype=jnp.int32), n_used - 1)
    blk_expert = jnp.sum((cum_blk[None, :] <= blk[:, None]).astype(jnp.int32), axis=1)
    blk_expert = jnp.minimum(blk_expert, N_EXPERTS - 1)
    return src_idx, dst_idx, dst_prev_idx, blk_expert, n_used.reshape(1)


def _moe(h3, x1, route, wg, wu, wd):
    seq = x1.shape[0]
    src_idx, dst_idx, dst_prev_idx, blk_expert, n_used = _moe_plan(route, seq)
    y = _moe_ffn(h3, src_idx, dst_idx, dst_prev_idx, blk_expert, n_used, wg, wu, wd)
    return _moe_combine(x1, y, route)


def _layer(x, i, a_norm, w_in, dqn, dkn, lq1, lk1, lq2, lk2, don, bqn, bkn, bon, w_out, f_norm, ffn, riders=()):
    seq = x.shape[0]
    lam_init = 0.8 - 0.6 * math.exp(-0.3 * i)
    gains, gmat, cos_t, sin_t, qcos_t, qsin_t = _segment_tables(seq, dqn, dkn, bqn, bkn)
    w_k, w_t = _split_w_in(w_in)
    qaT, ka, vaT, qbT, kb, vbT = _inproj(x, a_norm, w_k, w_t, gains, gmat, cos_t, sin_t, qcos_t, qsin_t)
    out_a, casted = _diff_attn(qaT, ka, vaT, lq1, lk1, lq2, lk2, don, gmat[0, :HEAD_W, :HEAD_W], lam_init, riders)
    out_b = _dil_attn(qbT, kb, vbT, bon, gmat[1, :HEAD_W, :HEAD_W])
    if len(ffn) == 3:
        x1, h = _outproj(out_a, out_b, w_out, x, f_norm)
        return _dense_ffn(h, x1, *ffn), casted
    router, wg, wu, wd = ffn
    x1, h3, route = _outproj(out_a, out_b, w_out, x, f_norm, router)
    return _moe(h3, x1, route, wg, wu, wd), casted


def kernel(x, attn_norm_0, w_in_0, diff_q_norm_0, diff_k_norm_0, diff_lam_q1_0, diff_lam_k1_0, diff_lam_q2_0, diff_lam_k2_0, diff_out_norm_0, dil_q_norm_0, dil_k_norm_0, dil_out_norm_0, w_out_0, ffn_norm_0, ffn_w_gate_0, ffn_w_up_0, ffn_w_down_0, attn_norm_1, w_in_1, diff_q_norm_1, diff_k_norm_1, diff_lam_q1_1, diff_lam_k1_1, diff_lam_q2_1, diff_lam_k2_1, diff_out_norm_1, dil_q_norm_1, dil_k_norm_1, dil_out_norm_1, w_out_1, ffn_norm_1, router_1, moe_w_gate_1, moe_w_up_1, moe_w_down_1):
    b, seq, d = x.shape
    assert b == 1 and d == D_MODEL and seq % (DIL_TQ * DIL_SUBTILES) == 0 and seq % ATT_TQ == 0
    assert MOE_TILE == ROW_TILE and DIL_PAD % ROW_TILE == 0
    xs = x.reshape(seq, d)
    xs, moe_w = _layer(xs, 0, attn_norm_0, w_in_0, diff_q_norm_0, diff_k_norm_0, diff_lam_q1_0, diff_lam_k1_0,
                       diff_lam_q2_0, diff_lam_k2_0, diff_out_norm_0, dil_q_norm_0, dil_k_norm_0, dil_out_norm_0,
                       w_out_0, ffn_norm_0, (ffn_w_gate_0, ffn_w_up_0, ffn_w_down_0),
                       riders=(moe_w_gate_1, moe_w_up_1, moe_w_down_1))
    xs, _ = _layer(xs, 1, attn_norm_1, w_in_1, diff_q_norm_1, diff_k_norm_1, diff_lam_q1_1, diff_lam_k1_1,
                   diff_lam_q2_1, diff_lam_k2_1, diff_out_norm_1, dil_q_norm_1, dil_k_norm_1, dil_out_norm_1,
                   w_out_1, ffn_norm_1, (router_1, *moe_w))
    return xs.reshape(b, seq, d)
```

```python
import functools
import math

import numpy as np
import jax
import jax.numpy as jnp
from jax import lax
from jax.experimental import pallas as pl
from jax.experimental.pallas import tpu as pltpu

F32 = jnp.float32
BF16 = jnp.bfloat16

D_MODEL = 2048
N_HEADS = 8
HEAD_W = 128
DIFF_DIM = 64
SEG_W = N_HEADS * HEAD_W
N_SEG = 6
ROW_CHUNKS = D_MODEL // HEAD_W
SLAB_PITCH = 24
DIL_BRANCHES = ((128, 1), (512, 4), (2048, 16))
N_SIDE = 64
D_FF = 5632
N_EXPERTS = 8
D_EXPERT = 7168
EPS = 1e-6
NEG = -1e30
LOG2E = 1.4426950408889634

MAP0_LANE = 0
MAP1_LANE = 32
ROW_SUM_FLOOR = 2.0 ** -80

V_ROWS = 144
DIL_PAD = 1024
V7X_VMEM_BYTES = 64 * 1024 * 1024
VMEM_LIMIT = V7X_VMEM_BYTES * 7 // 8

ROW_TILE = 512
ATT_TQ = 512
ATT_TK = 8192
ATT_SUB_K = 1024
DIL_TQ = 256
DIL_SUBTILES = 8
DIL_CHUNK = 768
FF_TILE = 512
D_EXPERT_STEPS = 7
MOE_FF_TILE = D_EXPERT // D_EXPERT_STEPS
MOE_TILE = 512


def _cparams(sem):
    return pltpu.CompilerParams(dimension_semantics=sem, vmem_limit_bytes=VMEM_LIMIT)


def _diff_lane_perm():
    perm = np.zeros(HEAD_W, np.int32)
    for m in range(2):
        for t in range(DIFF_DIM):
            p = (t // 32) * 64 + m * 32 + (t % 32)
            perm[p] = m * DIFF_DIM + t
    return perm


def _segment_tables(seq, dqn, dkn, bqn, bkn):
    perm = _diff_lane_perm()
    t_of_lane = perm % DIFF_DIM
    ones = jnp.ones((SEG_W,), F32)
    g_qa = jnp.tile(dqn.astype(F32)[t_of_lane], N_HEADS)
    g_ka = jnp.tile(dkn.astype(F32)[t_of_lane], N_HEADS)
    g_qb = jnp.tile(bqn.astype(F32), N_HEADS)
    g_kb = jnp.tile(bkn.astype(F32), N_HEADS)
    gains = jnp.stack([g_qa, g_ka, ones, g_qb, g_kb, ones]).reshape(N_SEG, 1, SEG_W)

    lane = np.arange(HEAD_W)
    map_of_lane = (lane // 32) % 2
    g_diff = (map_of_lane[:, None] == map_of_lane[None, :]).astype(np.float32)
    g_dil = np.ones((HEAD_W, HEAD_W), np.float32)
    pair = np.eye(2, dtype=np.float32)
    gmat = jnp.asarray(np.stack([np.kron(pair, g_diff), np.kron(pair, g_dil)]), BF16)

    pos = jnp.arange(seq, dtype=F32)

    def ang(dim):
        inv = 10000.0 ** (-jnp.arange(0, dim, 2, dtype=F32) / dim)
        return pos[:, None] * inv[None, :]

    a32 = ang(DIFF_DIM)
    a64 = ang(HEAD_W)
    cos_a = jnp.tile(jnp.cos(a32), (1, 4))
    sin_a = jnp.tile(jnp.sin(a32), (1, 4))
    cos_b = jnp.tile(jnp.cos(a64), (1, 2))
    sin_b = jnp.tile(jnp.sin(a64), (1, 2))
    sign = jnp.where(jnp.arange(HEAD_W) < 64, -1.0, 1.0).astype(F32)[None, :]
    cos_t = jnp.stack([cos_a, cos_b])
    sin_t = jnp.stack([sin_a * sign, sin_b * sign])
    return gains, gmat, cos_t, sin_t


def _permute_w_in(w_in):
    w = w_in.astype(BF16)
    qk = w[:, :2 * SEG_W].reshape(D_MODEL, 2 * N_HEADS, 2, 2, 32)
    qk = qk.transpose(0, 1, 3, 2, 4).reshape(D_MODEL, 2 * SEG_W)
    return jnp.concatenate([qk, w[:, 2 * SEG_W:]], axis=1)


def _inproj_kernel(x_ref, g_ref, w_ref, gain_ref, cos_ref, sin_ref, gmat_ref,
                   qaT_ref, ka_ref, vaT_ref, qbT_ref, kb_ref, vbT_ref, h_scr,
                   *, n_row_blocks, pad_blocks):
    ip = pl.program_id(0)
    j = pl.program_id(1)
    real = jnp.logical_and(ip >= pad_blocks, ip < pad_blocks + n_row_blocks)
    tm = x_ref.shape[0]

    @pl.when(jnp.logical_and(real, j == 0))
    def _():
        x = x_ref[...]
        ms = jnp.mean(x * x, axis=-1, keepdims=True)
        h_scr[...] = (x * lax.rsqrt(ms + EPS) * g_ref[...]).astype(BF16)

    def norm_rope(y, ss, c, n_group, scale):
        yn = y * lax.rsqrt(ss * (1.0 / n_group) + EPS) * gain_ref[:, c * HEAD_W:(c + 1) * HEAD_W]
        out = yn * cos_ref[...] + pltpu.roll(yn, 64, 1) * sin_ref[...]
        if scale != 1.0:
            out = out * scale
        return out

    def aug_rows():
        row = lax.broadcasted_iota(jnp.int32, (V_ROWS - HEAD_W, tm), 0)
        return jnp.where(row == 0, 1.0, 0.0).astype(BF16)

    def segment(seg):
        acc = jnp.dot(h_scr[...], w_ref[...], preferred_element_type=F32)
        for c in range(N_HEADS):
            y = acc[:, c * HEAD_W:(c + 1) * HEAD_W]
            if seg in (0, 1, 3, 4) and c % 2 == 0:
                pair = acc[:, c * HEAD_W:(c + 2) * HEAD_W]
                ss2 = jnp.dot((pair * pair).astype(BF16), gmat_ref[...], preferred_element_type=F32)
            if seg in (0, 1, 3, 4):
                ss = ss2[:, (c % 2) * HEAD_W:(c % 2 + 1) * HEAD_W]
            if seg == 0:
                qaT_ref[c] = norm_rope(y, ss, c, DIFF_DIM, DIFF_DIM ** -0.5 * LOG2E).T.astype(BF16)
            elif seg == 1:
                ka_ref[:, c * HEAD_W:(c + 1) * HEAD_W] = norm_rope(y, ss, c, DIFF_DIM, 1.0).astype(BF16)
            elif seg == 3:
                qbT_ref[c] = norm_rope(y, ss, c, HEAD_W, HEAD_W ** -0.5 * LOG2E).T.astype(BF16)
            elif seg == 4:
                kb_ref[:, c * HEAD_W:(c + 1) * HEAD_W] = norm_rope(y, ss, c, HEAD_W, 1.0).astype(BF16)
            elif seg == 2:
                vaT_ref[c] = y.T.astype(BF16)
            else:
                vbT_ref[c, 0:HEAD_W, :] = y.T.astype(BF16)
                vbT_ref[c, HEAD_W:V_ROWS, :] = aug_rows()

    for seg in range(N_SEG):
        pl.when(jnp.logical_and(real, j == seg))(functools.partial(segment, seg))

    @pl.when(jnp.logical_and(jnp.logical_not(real), j == 4))
    def _():
        kb_ref[...] = jnp.zeros(kb_ref.shape, BF16)

    @pl.when(jnp.logical_and(jnp.logical_not(real), j == 5))
    def _():
        vbT_ref[...] = jnp.zeros(vbT_ref.shape, BF16)


def _inproj(x, a_norm, w_in_p, gains, gmat, cos_t, sin_t):
    seq = x.shape[0]
    tm = ROW_TILE
    nrb = seq // tm
    pb = DIL_PAD // tm
    seq_p = seq + 2 * DIL_PAD

    def row(ip):
        return jnp.clip(ip - pb, 0, nrb - 1)

    kern = functools.partial(_inproj_kernel, n_row_blocks=nrb, pad_blocks=pb)
    out_shape = (
        jax.ShapeDtypeStruct((N_HEADS, HEAD_W, seq), BF16),
        jax.ShapeDtypeStruct((seq, SEG_W), BF16),
        jax.ShapeDtypeStruct((N_HEADS, HEAD_W, seq), BF16),
        jax.ShapeDtypeStruct((N_HEADS, HEAD_W, seq), BF16),
        jax.ShapeDtypeStruct((seq_p, SEG_W), BF16),
        jax.ShapeDtypeStruct((N_HEADS, V_ROWS, seq_p), BF16),
    )
    in_specs = [
        pl.BlockSpec((tm, D_MODEL), lambda ip, j: (row(ip), 0)),
        pl.BlockSpec((1, D_MODEL), lambda ip, j: (0, 0)),
        pl.BlockSpec((D_MODEL, SEG_W), lambda ip, j: (0, j)),
        pl.BlockSpec((None, 1, SEG_W), lambda ip, j: (j, 0, 0)),
        pl.BlockSpec((None, tm, HEAD_W), lambda ip, j: (j // 3, row(ip), 0)),
        pl.BlockSpec((None, tm, HEAD_W), lambda ip, j: (j // 3, row(ip), 0)),
        pl.BlockSpec((None, 2 * HEAD_W, 2 * HEAD_W), lambda ip, j: (j // 3, 0, 0)),
    ]
    out_specs = (
        pl.BlockSpec((N_HEADS, HEAD_W, tm), lambda ip, j: (0, 0, row(ip))),
        pl.BlockSpec((tm, SEG_W), lambda ip, j: (row(ip), 0)),
        pl.BlockSpec((N_HEADS, HEAD_W, tm), lambda ip, j: (0, 0, row(ip))),
        pl.BlockSpec((N_HEADS, HEAD_W, tm), lambda ip, j: (0, 0, row(ip))),
        pl.BlockSpec((tm, SEG_W), lambda ip, j: (ip, 0)),
        pl.BlockSpec((N_HEADS, V_ROWS, tm), lambda ip, j: (0, 0, ip)),
    )
    return pl.pallas_call(
        kern,
        out_shape=out_shape,
        grid=(nrb + 2 * pb, N_SEG),
        in_specs=in_specs,
        out_specs=out_specs,
        scratch_shapes=[pltpu.VMEM((tm, D_MODEL), BF16)],
        compiler_params=_cparams(("arbitrary", "arbitrary")),
        name="inproj",
    )(x, a_norm.reshape(1, D_MODEL).astype(F32), w_in_p, gains, cos_t, sin_t, gmat)


def _flash_step(s, m_old, acc_ref, vT_t):
    m_new = jnp.maximum(m_old, jnp.max(s, axis=0, keepdims=True))
    alpha = jnp.exp2(m_old - m_new)
    p = jnp.exp2((s - m_new).astype(BF16))
    acc_ref[...] = acc_ref[...] * alpha + jnp.dot(vT_t, p, preferred_element_type=F32)
    return m_new


def _flash_step_sum(s, m_old, l_old, acc_ref, vT_t):
    m_new = jnp.maximum(m_old, jnp.max(s, axis=0, keepdims=True))
    alpha = jnp.exp2(m_old - m_new)
    e = jnp.exp2(s - m_new)
    acc_ref[...] = acc_ref[...] * alpha + jnp.dot(vT_t, e.astype(BF16), preferred_element_type=F32)
    return m_new, l_old * alpha + jnp.sum(e, axis=0, keepdims=True)


def _head_out(aT, gain_row, out_scale):
    a = aT.T
    ms = jnp.mean(a * a, axis=-1, keepdims=True)
    y = a * lax.rsqrt(ms + EPS) * gain_row
    if out_scale != 1.0:
        y = y * out_scale
    return y


def _diff_attn_kernel(*refs, lam_init, tk, n_riders):
    lq1_ref, lk1_ref, lq2_ref, lk2_ref, qT_ref, k_ref, vT_ref, og_ref, gmat_ref = refs[:9]
    rider_in = refs[9:9 + n_riders]
    o_ref = refs[9 + n_riders]
    rider_out = refs[10 + n_riders:10 + 2 * n_riders]
    acc0, acc1, lsum, k0_scr, k1_scr, kmax_scr = refs[10 + 2 * n_riders:]
    tq = qT_ref.shape[1]
    seq = k_ref.shape[0]
    lane0, lane1 = MAP1_LANE, MAP0_LANE

    @pl.when(pl.program_id(1) == 0)
    def _():
        ck = min(1024, seq)

        def kchunk(c, mx):
            rows = pl.ds(pl.multiple_of(c * ck, ck), ck)
            kc = k_ref[rows, :]
            kf = kc.astype(F32)
            n2 = jnp.dot((kf * kf).astype(BF16), gmat_ref[...], preferred_element_type=F32)
            lane = lax.broadcasted_iota(jnp.int32, kc.shape, 1)
            one = jnp.ones_like(kc)
            k0_scr[rows, :] = jnp.where(lane == lane0, one, kc)
            k1_scr[rows, :] = jnp.where(lane == lane1, one, kc)
            return jnp.maximum(mx, jnp.max(n2, axis=0, keepdims=True))

        kn2 = lax.fori_loop(0, seq // ck, kchunk, jnp.zeros((1, HEAD_W), F32))
        kmax_scr[...] = jnp.sqrt(kn2)

    qT = qT_ref[...]
    row = lax.broadcasted_iota(jnp.int32, qT.shape, 0)
    in_map1 = ((row // 32) % 2) == 1
    zero = jnp.zeros_like(qT)
    qf = qT.astype(F32)
    q2 = qf * qf
    nq0 = jnp.sum(jnp.where(in_map1, 0.0, q2), axis=0, keepdims=True)
    nq1 = jnp.sum(jnp.where(in_map1, q2, 0.0), axis=0, keepdims=True)
    kmax = kmax_scr[...]
    b0 = jnp.sqrt(nq0) * kmax[:, MAP0_LANE:MAP0_LANE + 1]
    b1 = jnp.sqrt(nq1) * kmax[:, MAP1_LANE:MAP1_LANE + 1]
    q0 = jnp.where(in_map1, zero, qT)
    q1 = jnp.where(in_map1, qT, zero)
    q0s = jnp.where(row == lane0, jnp.broadcast_to(-b0, qf.shape).astype(BF16), q0)
    q1s = jnp.where(row == lane1, jnp.broadcast_to(-b1, qf.shape).astype(BF16), q1)
    acc0[...] = jnp.zeros(acc0.shape, F32)
    acc1[...] = jnp.zeros(acc1.shape, F32)

    def body(kt, carry):
        l0, l1 = carry
        half = tq // 2
        sub_k = min(ATT_SUB_K, tk)
        sums = []
        for q_s, k_scr, acc in ((q0s, k0_scr, acc0), (q1s, k1_scr, acc1)):
            parts = []
            for hh in range(2):
                lanes = slice(hh * half, (hh + 1) * half)
                num = acc[:, lanes]
                l_h = jnp.zeros((1, half), F32)
                for c in range(tk // sub_k):
                    rows = pl.ds(pl.multiple_of(kt * tk + c * sub_k, sub_k), sub_k)
                    e = jnp.exp2(jnp.dot(k_scr[rows, :], q_s[:, lanes], preferred_element_type=F32))
                    l_h = l_h + jnp.sum(e, axis=0, keepdims=True)
                    num = num + jnp.dot(vT_ref[:, rows], e.astype(BF16), preferred_element_type=F32)
                acc[:, lanes] = num
                parts.append(l_h)
            sums.append(jnp.concatenate(parts, axis=1))
        for w_ref, wb_ref in zip(rider_in, rider_out):
            r = w_ref.shape[0] // (seq // tk)
            part = pl.ds(pl.multiple_of(kt * r, 16), r)
            wb_ref[part, :] = w_ref[part, :].astype(BF16)
        return l0 + sums[0], l1 + sums[1]

    l_init = jnp.zeros((1, tq), F32)
    l0, l1 = lax.fori_loop(0, seq // tk, body, (l_init, l_init))
    lsum[0:1, :] = l0
    lsum[1:2, :] = l1

    @pl.when(jnp.logical_not(jnp.min(jnp.minimum(l0, l1)) >= ROW_SUM_FLOOR))
    def _():
        acc0[...] = jnp.zeros(acc0.shape, F32)
        acc1[...] = jnp.zeros(acc1.shape, F32)

        def robust(kt, carry):
            m0, l0, m1, l1 = carry
            rows = pl.ds(pl.multiple_of(kt * tk, tk), tk)
            k_t = k_ref[rows, :]
            vT_t = vT_ref[:, rows]
            m0, l0 = _flash_step_sum(jnp.dot(k_t, q0, preferred_element_type=F32), m0, l0, acc0, vT_t)
            m1, l1 = _flash_step_sum(jnp.dot(k_t, q1, preferred_element_type=F32), m1, l1, acc1, vT_t)
            return m0, l0, m1, l1

        m_init = jnp.full((1, tq), NEG, F32)
        _, l0, _, l1 = lax.fori_loop(0, seq // tk, robust, (m_init, l_init, m_init, l_init))
        lsum[0:1, :] = l0
        lsum[1:2, :] = l1

    lam = (jnp.exp(jnp.sum(lq1_ref[...] * lk1_ref[...], axis=-1, keepdims=True))
           - jnp.exp(jnp.sum(lq2_ref[...] * lk2_ref[...], axis=-1, keepdims=True)) + lam_init)
    o0 = acc0[...] * (1.0 / lsum[0:1, :])
    o1 = acc1[...] * (1.0 / lsum[1:2, :])
    aT = o0 - lam * o1
    o_ref[...] = _head_out(aT, og_ref[...], 1.0 - lam_init).astype(o_ref.dtype)


def _diff_attn(qaT, ka, vaT, lq1, lk1, lq2, lk2, og, gmat_diff, lam_init, riders=()):
    seq = ka.shape[0]
    tq = min(ATT_TQ, seq)
    tk = min(ATT_TK, seq)
    nq = seq // tq
    n_steps = N_HEADS * nq
    kern = functools.partial(_diff_attn_kernel, lam_init=lam_init, tk=tk, n_riders=len(riders))
    vec = lambda v: v.reshape(1, -1).astype(F32)
    small = pl.BlockSpec((1, DIFF_DIM), lambda h, qi: (0, 0))
    once = pl.Buffered(1)
    rider_specs = []
    for w in riders:
        n_e, n_r, n_c = w.shape
        per_e = n_steps // n_e
        rider_specs.append(pl.BlockSpec(
            (None, n_r // per_e, n_c),
            lambda h, qi, per_e=per_e: ((h * nq + qi) // per_e, (h * nq + qi) % per_e, 0)))
    outs = pl.pallas_call(
        kern,
        out_shape=(jax.ShapeDtypeStruct((seq, SEG_W), BF16),
                   *[jax.ShapeDtypeStruct(w.shape, BF16) for w in riders]),
        grid=(N_HEADS, nq),
        in_specs=[small, small, small, small,
                  pl.BlockSpec((None, HEAD_W, tq), lambda h, qi: (h, 0, qi)),
                  pl.BlockSpec((seq, HEAD_W), lambda h, qi: (0, h), pipeline_mode=once),
                  pl.BlockSpec((None, HEAD_W, seq), lambda h, qi: (h, 0, 0), pipeline_mode=once),
                  pl.BlockSpec((1, HEAD_W), lambda h, qi: (0, 0)),
                  pl.BlockSpec((HEAD_W, HEAD_W), lambda h, qi: (0, 0)),
                  *rider_specs],
        out_specs=(pl.BlockSpec((tq, HEAD_W), lambda h, qi: (qi, h)), *rider_specs),
        scratch_shapes=[pltpu.VMEM((HEAD_W, tq), F32), pltpu.VMEM((HEAD_W, tq), F32),
                        pltpu.VMEM((8, tq), F32),
                        pltpu.VMEM((seq, HEAD_W), BF16), pltpu.VMEM((seq, HEAD_W), BF16),
                        pltpu.VMEM((1, HEAD_W), F32)],
        compiler_params=_cparams(("arbitrary", "arbitrary")),
        name="diff_attn",
    )(vec(lq1), vec(lk1), vec(lq2), vec(lk2), qaT, ka, vaT, vec(og), gmat_diff, *riders)
    return outs[0], tuple(outs[1:])


def _dil_chunks(tq):
    chunks, off = [], 0
    for _, dil in DIL_BRANCHES:
        pad = -(-(N_SIDE * dil) // 128) * 128
        total = tq + 2 * pad
        c0 = 0
        while c0 < total:
            nk = min(DIL_CHUNK, total - c0)
            chunks.append((dil, c0 - pad, nk, off))
            off += nk
            c0 += nk
    return chunks, off


def _dil_bias(tq):
    chunks, total = _dil_chunks(tq)
    bias = np.full((total, tq), NEG, np.float32)
    col = np.arange(tq)[None, :]
    for dil, rel, nk, off in chunks:
        delta = rel + np.arange(nk)[:, None] - col
        ok = (np.abs(delta) <= N_SIDE * dil) & (delta % dil == 0)
        bias[off:off + nk][ok] = 0.0
    return jnp.asarray(bias)


def _dil_attn_kernel(qT_ref, k_ref, vT_ref, bias_ref, og_ref, gmat_ref, o_ref, acc, kmax_scr, *, seq, chunks):
    tq = acc.shape[1]
    n_sub = qT_ref.shape[1] // tq
    base = pl.program_id(1) * (tq * n_sub)

    @pl.when(pl.program_id(1) == 0)
    def _():
        ck = 1024

        def kchunk(c, mx):
            kf = k_ref[pl.ds(pl.multiple_of(c * ck, ck), ck), :].astype(F32)
            n2 = jnp.dot((kf * kf).astype(BF16), gmat_ref[...], preferred_element_type=F32)
            return jnp.maximum(mx, jnp.max(n2, axis=0, keepdims=True))

        kn2 = lax.fori_loop(0, k_ref.shape[0] // ck, kchunk, jnp.zeros((1, HEAD_W), F32))
        kmax_scr[...] = jnp.sqrt(kn2)

    def windows(i0):
        for dil, rel, nk, off in chunks:
            start = pl.multiple_of(i0 + (DIL_PAD + rel), 128)
            yield rel, nk, off, k_ref[pl.ds(start, nk), :], vT_ref[:, pl.ds(start, nk)]

    def finish(t, num):
        oT = num[0:HEAD_W, :] * (1.0 / num[HEAD_W:HEAD_W + 1, :])
        o_ref[t * tq:(t + 1) * tq, :] = _head_out(oT, og_ref[...], 1.0).astype(o_ref.dtype)

    lmin = None
    for t in range(n_sub):
        qT = qT_ref[:, t * tq:(t + 1) * tq]
        qf = qT.astype(F32)
        shift = jnp.sqrt(jnp.sum(qf * qf, axis=0, keepdims=True)) * kmax_scr[:, 0:1]
        num = jnp.zeros((V_ROWS, tq), F32)
        for rel, nk, off, k_t, vT_t in windows(base + t * tq):
            s = jnp.dot(k_t, qT, preferred_element_type=F32) - shift + bias_ref[off:off + nk, :]
            num = num + jnp.dot(vT_t, jnp.exp2(s).astype(BF16), preferred_element_type=F32)
        finish(t, num)
        l_t = jnp.min(num[HEAD_W:HEAD_W + 1, :])
        lmin = l_t if lmin is None else jnp.minimum(lmin, l_t)

    @pl.when(jnp.logical_not(lmin >= ROW_SUM_FLOOR))
    def _():
        for t in range(n_sub):
            i0 = base + t * tq
            qT = qT_ref[:, t * tq:(t + 1) * tq]
            acc[...] = jnp.zeros(acc.shape, F32)
            m = jnp.full((1, tq), NEG, F32)
            for rel, nk, off, k_t, vT_t in windows(i0):
                s = jnp.dot(k_t, qT, preferred_element_type=F32)
                kpos = lax.broadcasted_iota(jnp.int32, (nk, tq), 0) + (i0 + rel)
                valid = jnp.logical_and(kpos >= 0, kpos < seq)
                s = jnp.where(valid, s + bias_ref[off:off + nk, :], NEG)
                m = _flash_step(s, m, acc, vT_t)
            finish(t, acc[...])


def _dil_attn(qbT, kb, vbT, og, gmat_ones):
    seq = qbT.shape[2]
    seq_p = kb.shape[0]
    tq = min(DIL_TQ, seq)
    n_sub = DIL_SUBTILES
    chunks, _ = _dil_chunks(tq)
    bias = _dil_bias(tq)
    kern = functools.partial(_dil_attn_kernel, seq=seq, chunks=chunks)
    return pl.pallas_call(
        kern,
        out_shape=jax.ShapeDtypeStruct((seq, SEG_W), BF16),
        grid=(N_HEADS, seq // (tq * n_sub)),
        in_specs=[pl.BlockSpec((None, HEAD_W, tq * n_sub), lambda h, qi: (h, 0, qi)),
                  pl.BlockSpec((seq_p, HEAD_W), lambda h, qi: (0, h)),
                  pl.BlockSpec((None, V_ROWS, seq_p), lambda h, qi: (h, 0, 0)),
                  pl.BlockSpec(bias.shape, lambda h, qi: (0, 0)),
                  pl.BlockSpec((1, HEAD_W), lambda h, qi: (0, 0)),
                  pl.BlockSpec((HEAD_W, HEAD_W), lambda h, qi: (0, 0))],
        out_specs=pl.BlockSpec((tq * n_sub, HEAD_W), lambda h, qi: (qi, h)),
        scratch_shapes=[pltpu.VMEM((V_ROWS, tq), F32), pltpu.VMEM((1, HEAD_W), F32)],
        compiler_params=_cparams(("arbitrary", "arbitrary")),
        name="dil_attn",
    )(qbT, kb, vbT, bias, og.reshape(1, HEAD_W).astype(F32), gmat_ones)


def _outproj_kernel(*refs, with_router):
    if with_router:
        a_ref, b_ref, w_ref, x_ref, g_ref, r_hi_ref, r_lo_ref, x1_ref, h_ref, route_ref = refs
    else:
        a_ref, b_ref, w_ref, x_ref, g_ref, x1_ref, h_ref = refs
    acc = (jnp.dot(a_ref[...], w_ref[0:SEG_W, :], preferred_element_type=F32)
           + jnp.dot(b_ref[...], w_ref[SEG_W:2 * SEG_W, :], preferred_element_type=F32))
    x1 = x_ref[...] + acc
    x1_ref[...] = x1
    ms = jnp.mean(x1 * x1, axis=-1, keepdims=True)
    hn = x1 * lax.rsqrt(ms + EPS) * g_ref[...]
    if not with_router:
        h_ref[...] = hn.astype(BF16)
    else:
        tm = hn.shape[0]
        for s in range(ROW_CHUNKS):
            h_ref[pl.ds(s, tm, stride=ROW_CHUNKS), :] = hn[:, s * HEAD_W:(s + 1) * HEAD_W]
        h_hi = hn.astype(BF16)
        h_lo = (hn - h_hi.astype(F32)).astype(BF16)
        logits = (jnp.dot(h_hi, r_hi_ref[...], preferred_element_type=F32)
                  + jnp.dot(h_hi, r_lo_ref[...], preferred_element_type=F32)
                  + jnp.dot(h_lo, r_hi_ref[...], preferred_element_type=F32))
        lane = lax.broadcasted_iota(jnp.int32, logits.shape, 1)
        ninf = jnp.float32(-jnp.inf)
        lg = jnp.where(lane < N_EXPERTS, logits, ninf)
        v1 = jnp.max(lg, axis=-1, keepdims=True)
        i1 = jnp.min(jnp.where(lg == v1, lane, HEAD_W), axis=-1, keepdims=True)
        lg2 = jnp.where(lane == i1, ninf, lg)
        v2 = jnp.max(lg2, axis=-1, keepdims=True)
        i2 = jnp.min(jnp.where(lg2 == v2, lane, HEAD_W), axis=-1, keepdims=True)
        g1 = 1.0 / (1.0 + jnp.exp(v2 - v1))
        g2 = 1.0 - g1
        route_ref[...] = jnp.where(lane == 0, i1.astype(F32),
                                   jnp.where(lane == 1, i2.astype(F32),
                                             jnp.where(lane == 2, g1, jnp.where(lane == 3, g2, 0.0))))


def _outproj(out_a, out_b, w_out, x, f_norm, router=None):
    seq = x.shape[0]
    tm = ROW_TILE
    with_router = router is not None
    in_specs = [pl.BlockSpec((tm, SEG_W), lambda i: (i, 0)),
                pl.BlockSpec((tm, SEG_W), lambda i: (i, 0)),
                pl.BlockSpec((D_MODEL, D_MODEL), lambda i: (0, 0)),
                pl.BlockSpec((tm, D_MODEL), lambda i: (i, 0)),
                pl.BlockSpec((1, D_MODEL), lambda i: (0, 0))]
    args = [out_a, out_b, w_out.astype(BF16), x, f_norm.reshape(1, D_MODEL).astype(F32)]
    out_shape = [jax.ShapeDtypeStruct((seq, D_MODEL), F32)]
    out_specs = [pl.BlockSpec((tm, D_MODEL), lambda i: (i, 0))]
    if not with_router:
        out_shape.append(jax.ShapeDtypeStruct((seq, D_MODEL), BF16))
        out_specs.append(pl.BlockSpec((tm, D_MODEL), lambda i: (i, 0)))
    else:
        out_shape.append(jax.ShapeDtypeStruct((seq * ROW_CHUNKS, HEAD_W), F32))
        out_specs.append(pl.BlockSpec((tm * ROW_CHUNKS, HEAD_W), lambda i: (i, 0)))
        r = jnp.zeros((D_MODEL, HEAD_W), F32).at[:, :N_EXPERTS].set(router.astype(F32))
        r_hi = r.astype(BF16)
        r_lo = (r - r_hi.astype(F32)).astype(BF16)
        in_specs += [pl.BlockSpec((D_MODEL, HEAD_W), lambda i: (0, 0))] * 2
        args += [r_hi, r_lo]
        out_shape.append(jax.ShapeDtypeStruct((seq, HEAD_W), F32))
        out_specs.append(pl.BlockSpec((tm, HEAD_W), lambda i: (i, 0)))
    return pl.pallas_call(
        functools.partial(_outproj_kernel, with_router=with_router),
        out_shape=tuple(out_shape),
        grid=(seq // tm,),
        in_specs=in_specs,
        out_specs=tuple(out_specs),
        compiler_params=_cparams(("arbitrary",)),
        name="outproj",
    )(*args)


def _swiglu_part(h, wg, wu, wd):
    g = jnp.dot(h, wg, preferred_element_type=F32)
    u = jnp.dot(h, wu, preferred_element_type=F32)
    act = (g * (1.0 / (1.0 + jnp.exp(-g))) * u).astype(BF16)
    return jnp.dot(act, wd, preferred_element_type=F32)


def _dense_ffn_kernel(h_ref, x_ref, wg_ref, wu_ref, wd_ref, o_ref):
    @pl.when(pl.program_id(1) == 0)
    def _():
        o_ref[...] = x_ref[...]

    o_ref[...] += _swiglu_part(h_ref[...], wg_ref[...], wu_ref[...], wd_ref[...])


def _dense_ffn(h, x1, wg, wu, wd):
    seq = x1.shape[0]
    tm, tf = ROW_TILE, FF_TILE
    return pl.pallas_call(
        _dense_ffn_kernel,
        out_shape=jax.ShapeDtypeStruct((seq, D_MODEL), F32),
        grid=(seq // tm, D_FF // tf),
        in_specs=[pl.BlockSpec((tm, D_MODEL), lambda i, j: (i, 0)),
                  pl.BlockSpec((tm, D_MODEL), lambda i, j: (i, 0)),
                  pl.BlockSpec((D_MODEL, tf), lambda i, j: (0, j)),
                  pl.BlockSpec((D_MODEL, tf), lambda i, j: (0, j)),
                  pl.BlockSpec((tf, D_MODEL), lambda i, j: (j, 0))],
        out_specs=pl.BlockSpec((tm, D_MODEL), lambda i, j: (i, 0)),
        compiler_params=_cparams(("arbitrary", "arbitrary")),
        name="dense_ffn",
    )(h, x1, wg.astype(BF16), wu.astype(BF16), wd.astype(BF16))


def _moe_ffn_kernel(be_ref, nused_ref, src_ref, src_next_ref, dst_ref, dst_prev_ref,
                    h3_ref, wg_ref, wu_ref, wd_ref, y_ref, x3, xb, acc, o3, gsem, ssem):
    b = pl.program_id(0)
    j = pl.program_id(1)
    nb = pl.num_programs(0)
    nj = pl.num_programs(1)
    tm = xb.shape[0]
    n_used = nused_ref[0]
    used = b < n_used

    def slab(ref, row0):
        return ref.at[pl.ds(pl.multiple_of(row0, 8), ROW_CHUNKS), :]

    def start_gather(idx_ref, slot):
        def start(r, c):
            pltpu.make_async_copy(slab(h3_ref, idx_ref[0, r]), slab(x3.at[slot], r * SLAB_PITCH),
                                  gsem.at[slot]).start()
            return c

        lax.fori_loop(0, tm, start, 0)

    def scatter_copy(idx_ref, r):
        return pltpu.make_async_copy(slab(o3, r * SLAB_PITCH), slab(y_ref, idx_ref[0, r]), ssem)

    def wait_gather(slot):
        n_rows = tm * ROW_CHUNKS
        pltpu.make_async_copy(h3_ref.at[pl.ds(0, n_rows), :], x3.at[slot, pl.ds(0, n_rows), :],
                              gsem.at[slot]).wait()

    def wait_scatter():
        n_rows = tm * ROW_CHUNKS
        pltpu.make_async_copy(o3.at[pl.ds(0, n_rows), :], y_ref.at[pl.ds(0, n_rows), :], ssem).wait()

    def flush_scatter(idx_ref):
        def start(r, c):
            scatter_copy(idx_ref, r).start()
            return c

        lax.fori_loop(0, tm, start, 0)
        wait_scatter()

    @pl.when(jnp.logical_and(j == 0, jnp.logical_and(b == 0, used)))
    def _():
        start_gather(src_ref, 0)
        o3[...] = jnp.zeros(o3.shape, F32)

    @pl.when(jnp.logical_and(j == 0, jnp.logical_and(b >= 1, b == n_used)))
    def _():
        flush_scatter(dst_prev_ref)

    @pl.when(jnp.logical_and(j == 0, jnp.where(b == 0, used, b - 1 < n_used)))
    def _():
        wait_gather(b % 2)

    @pl.when(jnp.logical_and(j == 0, used))
    def _():
        slot = b % 2
        for s in range(ROW_CHUNKS):
            xb[:, s * HEAD_W:(s + 1) * HEAD_W] = x3[slot, pl.ds(s, tm, stride=SLAB_PITCH), :].astype(BF16)
        acc[...] = jnp.zeros(acc.shape, F32)

    @pl.when(used)
    def _():
        per_step = tm // D_EXPERT_STEPS
        nslot = (b + 1) % 2

        def move(r):
            pltpu.make_async_copy(slab(h3_ref, src_next_ref[0, r]), slab(x3.at[nslot], r * SLAB_PITCH),
                                  gsem.at[nslot]).start()
            scatter_copy(dst_prev_ref, r).start()

        for i in range(per_step):
            move(j * per_step + i)

        @pl.when(j == 0)
        def _():
            for r in range(per_step * D_EXPERT_STEPS, tm):
                move(r)

        acc[...] += _swiglu_part(xb[...], wg_ref[...], wu_ref[...], wd_ref[...])

    @pl.when(jnp.logical_and(j == nj - 1, used))
    def _():
        wait_scatter()
        for s in range(ROW_CHUNKS):
            o3[pl.ds(s, tm, stride=SLAB_PITCH), :] = acc[:, s * HEAD_W:(s + 1) * HEAD_W]

        @pl.when(b == nb - 1)
        def _():
            flush_scatter(dst_ref)
            wait_gather((b + 1) % 2)


def _moe_ffn(h3, src_idx, dst_idx, dst_prev_idx, blk_expert, n_used, wg, wu, wd):
    seq = h3.shape[0] // ROW_CHUNKS
    n_blocks = src_idx.shape[0]
    tm, tf = MOE_TILE, MOE_FF_TILE
    nj = D_EXPERT // tf

    def jj(b, j, be, nu):
        return jnp.where(b < nu[0], j, nj - 1)

    def idx_spec(shift):
        return pl.BlockSpec((None, 1, tm), lambda b, j, be, nu: (jnp.clip(b + shift, 0, n_blocks - 1), 0, 0),
                            memory_space=pltpu.SMEM)

    return pl.pallas_call(
        _moe_ffn_kernel,
        out_shape=jax.ShapeDtypeStruct(((2 * seq + tm) * ROW_CHUNKS, HEAD_W), F32),
        grid_spec=pltpu.PrefetchScalarGridSpec(
            num_scalar_prefetch=2,
            grid=(n_blocks, nj),
            in_specs=[idx_spec(0), idx_spec(1), idx_spec(0), idx_spec(0),
                      pl.BlockSpec(memory_space=pl.ANY),
                      pl.BlockSpec((None, D_MODEL, tf), lambda b, j, be, nu: (be[b], 0, jj(b, j, be, nu))),
                      pl.BlockSpec((None, D_MODEL, tf), lambda b, j, be, nu: (be[b], 0, jj(b, j, be, nu))),
                      pl.BlockSpec((None, tf, D_MODEL), lambda b, j, be, nu: (be[b], jj(b, j, be, nu), 0))],
            out_specs=pl.BlockSpec(memory_space=pl.ANY),
            scratch_shapes=[pltpu.VMEM((2, tm * SLAB_PITCH, HEAD_W), F32),
                            pltpu.VMEM((tm, D_MODEL), BF16),
                            pltpu.VMEM((tm, D_MODEL), F32),
                            pltpu.VMEM((tm * SLAB_PITCH, HEAD_W), F32),
                            pltpu.SemaphoreType.DMA((2,)),
                            pltpu.SemaphoreType.DMA(())]),
        compiler_params=_cparams(("arbitrary", "arbitrary")),
        name="moe_ffn",
    )(blk_expert, n_used, src_idx, src_idx, dst_idx, dst_prev_idx, h3, wg, wu, wd)


def _combine_kernel(x_ref, ya_ref, yb_ref, route_ref, o_ref):
    r = route_ref[...]
    g1 = jnp.broadcast_to(r[:, 2:3], (r.shape[0], HEAD_W))
    g2 = jnp.broadcast_to(r[:, 3:4], (r.shape[0], HEAD_W))
    tm = r.shape[0]
    for s in range(ROW_CHUNKS):
        cols = slice(s * HEAD_W, (s + 1) * HEAD_W)
        rows = pl.ds(s, tm, stride=ROW_CHUNKS)
        o_ref[:, cols] = x_ref[:, cols] + g1 * ya_ref[rows, :] + g2 * yb_ref[rows, :]


def _moe_combine(x1, y, route):
    seq = x1.shape[0]
    tm = ROW_TILE
    nb = seq // tm
    big = pl.BlockSpec((tm, D_MODEL), lambda i: (i, 0))
    return pl.pallas_call(
        _combine_kernel,
        out_shape=jax.ShapeDtypeStruct((seq, D_MODEL), F32),
        grid=(nb,),
        in_specs=[big,
                  pl.BlockSpec((tm * ROW_CHUNKS, HEAD_W), lambda i: (i, 0)),
                  pl.BlockSpec((tm * ROW_CHUNKS, HEAD_W), lambda i: (i + nb, 0)),
                  pl.BlockSpec((tm, HEAD_W), lambda i: (i, 0))],
        out_specs=big,
        compiler_params=_cparams(("arbitrary",)),
        name="moe_combine",
    )(x1, y, y, route)


def _moe_plan(route, seq):
    tm = MOE_TILE
    n_assign = 2 * seq
    n_blocks = n_assign // tm + N_EXPERTS
    e = route[:, 0:2].astype(jnp.int32).reshape(-1)
    onehot = (e[:, None] == jnp.arange(N_EXPERTS, dtype=jnp.int32)[None, :]).astype(jnp.int32)
    csum = jnp.cumsum(onehot, axis=0)
    counts = csum[-1]
    rank = jnp.sum((csum - onehot) * onehot, axis=1)
    nblk = (counts + tm - 1) // tm
    cum_blk = jnp.cumsum(nblk)
    pad_start = (cum_blk - nblk) * tm
    pos = pad_start[e] + rank
    assign = jnp.arange(n_assign, dtype=jnp.int32)
    slot_assign = jnp.full((n_blocks * tm,), -1, jnp.int32).at[pos].set(
        assign, unique_indices=True, mode='promise_in_bounds')
    real = slot_assign >= 0
    tok = jnp.maximum(slot_assign, 0) // 2
    in_block = jnp.arange(n_blocks * tm, dtype=jnp.int32) % tm
    src_idx = tok * ROW_CHUNKS
    dst_idx = jnp.where(real, (slot_assign % 2) * seq + tok, n_assign + in_block) * ROW_CHUNKS
    src_idx = src_idx.reshape(n_blocks, 1, tm)
    dst_idx = dst_idx.reshape(n_blocks, 1, tm)
    scratch_blk = ((n_assign + jnp.arange(tm, dtype=jnp.int32)) * ROW_CHUNKS).reshape(1, 1, tm)
    dst_prev_idx = jnp.concatenate([scratch_blk, dst_idx[:-1]], axis=0)
    n_used = cum_blk[-1].astype(jnp.int32)
    blk = jnp.minimum(jnp.arange(n_blocks, dtype=jnp.int32), n_used - 1)
    blk_expert = jnp.sum((cum_blk[None, :] <= blk[:, None]).astype(jnp.int32), axis=1)
    blk_expert = jnp.minimum(blk_expert, N_EXPERTS - 1)
    return src_idx, dst_idx, dst_prev_idx, blk_expert, n_used.reshape(1)


def _moe(h3, x1, route, wg, wu, wd):
    seq = x1.shape[0]
    src_idx, dst_idx, dst_prev_idx, blk_expert, n_used = _moe_plan(route, seq)
    y = _moe_ffn(h3, src_idx, dst_idx, dst_prev_idx, blk_expert, n_used, wg, wu, wd)
    return _moe_combine(x1, y, route)


def _layer(x, i, a_norm, w_in, dqn, dkn, lq1, lk1, lq2, lk2, don, bqn, bkn, bon, w_out, f_norm, ffn, riders=()):
    seq = x.shape[0]
    lam_init = 0.8 - 0.6 * math.exp(-0.3 * i)
    gains, gmat, cos_t, sin_t = _segment_tables(seq, dqn, dkn, bqn, bkn)
    qaT, ka, vaT, qbT, kb, vbT = _inproj(x, a_norm, _permute_w_in(w_in), gains, gmat, cos_t, sin_t)
    out_a, casted = _diff_attn(qaT, ka, vaT, lq1, lk1, lq2, lk2, don, gmat[0, :HEAD_W, :HEAD_W], lam_init, riders)
    out_b = _dil_attn(qbT, kb, vbT, bon, gmat[1, :HEAD_W, :HEAD_W])
    if len(ffn) == 3:
        x1, h = _outproj(out_a, out_b, w_out, x, f_norm)
        return _dense_ffn(h, x1, *ffn), casted
    router, wg, wu, wd = ffn
    x1, h3, route = _outproj(out_a, out_b, w_out, x, f_norm, router)
    return _moe(h3, x1, route, wg, wu, wd), casted


def kernel(x, attn_norm_0, w_in_0, diff_q_norm_0, diff_k_norm_0, diff_lam_q1_0, diff_lam_k1_0, diff_lam_q2_0, diff_lam_k2_0, diff_out_norm_0, dil_q_norm_0, dil_k_norm_0, dil_out_norm_0, w_out_0, ffn_norm_0, ffn_w_gate_0, ffn_w_up_0, ffn_w_down_0, attn_norm_1, w_in_1, diff_q_norm_1, diff_k_norm_1, diff_lam_q1_1, diff_lam_k1_1, diff_lam_q2_1, diff_lam_k2_1, diff_out_norm_1, dil_q_norm_1, dil_k_norm_1, dil_out_norm_1, w_out_1, ffn_norm_1, router_1, moe_w_gate_1, moe_w_up_1, moe_w_down_1):
    b, seq, d = x.shape
    assert b == 1 and d == D_MODEL and seq % (DIL_TQ * DIL_SUBTILES) == 0 and seq % ATT_TQ == 0
    assert MOE_TILE == ROW_TILE and DIL_PAD % ROW_TILE == 0
    xs = x.reshape(seq, d)
    xs, moe_w = _layer(xs, 0, attn_norm_0, w_in_0, diff_q_norm_0, diff_k_norm_0, diff_lam_q1_0, diff_lam_k1_0,
                       diff_lam_q2_0, diff_lam_k2_0, diff_out_norm_0, dil_q_norm_0, dil_k_norm_0, dil_out_norm_0,
                       w_out_0, ffn_norm_0, (ffn_w_gate_0, ffn_w_up_0, ffn_w_down_0),
                       riders=(moe_w_gate_1, moe_w_up_1, moe_w_down_1))
    xs, _ = _layer(xs, 1, attn_norm_1, w_in_1, diff_q_norm_1, diff_k_norm_1, diff_lam_q1_1, diff_lam_k1_1,
                   diff_lam_q2_1, diff_lam_k2_1, diff_out_norm_1, dil_q_norm_1, dil_k_norm_1, dil_out_norm_1,
                   w_out_1, ffn_norm_1, (router_1, *moe_w))
    return xs.reshape(b, seq, d)
```

```python
import functools
import math

import numpy as np
import jax
import jax.numpy as jnp
from jax import lax
from jax.experimental import pallas as pl
from jax.experimental.pallas import tpu as pltpu

F32 = jnp.float32
BF16 = jnp.bfloat16

D_MODEL = 2048
N_HEADS = 8
HEAD_W = 128
DIFF_DIM = 64
SEG_W = N_HEADS * HEAD_W
N_SEG = 6
ROW_CHUNKS = D_MODEL // HEAD_W
SLAB_PITCH = 24
DIL_BRANCHES = ((128, 1), (512, 4), (2048, 16))
N_SIDE = 64
D_FF = 5632
N_EXPERTS = 8
D_EXPERT = 7168
EPS = 1e-6
NEG = -1e30
LOG2E = 1.4426950408889634

MAP0_LANE = 0
MAP1_LANE = 32
ROW_SUM_FLOOR = 2.0 ** -80

V_ROWS = 144
DIL_PAD = 1024
V7X_VMEM_BYTES = 64 * 1024 * 1024
VMEM_LIMIT = V7X_VMEM_BYTES * 7 // 8

ROW_TILE = 512
ATT_TQ = 512
ATT_TK = 8192
DIL_TQ = 256
DIL_SUBTILES = 8
DIL_CHUNK = 768
FF_TILE = 512
D_EXPERT_STEPS = 7
MOE_FF_TILE = D_EXPERT // D_EXPERT_STEPS
MOE_TILE = 512


def _cparams(sem):
    return pltpu.CompilerParams(dimension_semantics=sem, vmem_limit_bytes=VMEM_LIMIT)


def _diff_lane_perm():
    perm = np.zeros(HEAD_W, np.int32)
    for m in range(2):
        for t in range(DIFF_DIM):
            p = (t // 32) * 64 + m * 32 + (t % 32)
            perm[p] = m * DIFF_DIM + t
    return perm


def _segment_tables(seq, dqn, dkn, bqn, bkn):
    perm = _diff_lane_perm()
    t_of_lane = perm % DIFF_DIM
    ones = jnp.ones((SEG_W,), F32)
    g_qa = jnp.tile(dqn.astype(F32)[t_of_lane], N_HEADS)
    g_ka = jnp.tile(dkn.astype(F32)[t_of_lane], N_HEADS)
    g_qb = jnp.tile(bqn.astype(F32), N_HEADS)
    g_kb = jnp.tile(bkn.astype(F32), N_HEADS)
    gains = jnp.stack([g_qa, g_ka, ones, g_qb, g_kb, ones]).reshape(N_SEG, 1, SEG_W)

    lane = np.arange(HEAD_W)
    map_of_lane = (lane // 32) % 2
    g_diff = (map_of_lane[:, None] == map_of_lane[None, :]).astype(np.float32)
    g_dil = np.ones((HEAD_W, HEAD_W), np.float32)
    pair = np.eye(2, dtype=np.float32)
    gmat = jnp.asarray(np.stack([np.kron(pair, g_diff), np.kron(pair, g_dil)]), BF16)

    pos = jnp.arange(seq, dtype=F32)

    def ang(dim):
        inv = 10000.0 ** (-jnp.arange(0, dim, 2, dtype=F32) / dim)
        return pos[:, None] * inv[None, :]

    a32 = ang(DIFF_DIM)
    a64 = ang(HEAD_W)
    cos_a = jnp.tile(jnp.cos(a32), (1, 4))
    sin_a = jnp.tile(jnp.sin(a32), (1, 4))
    cos_b = jnp.tile(jnp.cos(a64), (1, 2))
    sin_b = jnp.tile(jnp.sin(a64), (1, 2))
    sign = jnp.where(jnp.arange(HEAD_W) < 64, -1.0, 1.0).astype(F32)[None, :]
    cos_t = jnp.stack([cos_a, cos_b])
    sin_t = jnp.stack([sin_a * sign, sin_b * sign])
    return gains, gmat, cos_t, sin_t


def _permute_w_in(w_in):
    w = w_in.astype(BF16)
    qk = w[:, :2 * SEG_W].reshape(D_MODEL, 2 * N_HEADS, 2, 2, 32)
    qk = qk.transpose(0, 1, 3, 2, 4).reshape(D_MODEL, 2 * SEG_W)
    return jnp.concatenate([qk, w[:, 2 * SEG_W:]], axis=1)


def _inproj_kernel(x_ref, g_ref, w_ref, gain_ref, cos_ref, sin_ref, gmat_ref,
                   qaT_ref, ka_ref, vaT_ref, qbT_ref, kb_ref, vbT_ref, h_scr,
                   *, n_row_blocks, pad_blocks):
    ip = pl.program_id(0)
    j = pl.program_id(1)
    real = jnp.logical_and(ip >= pad_blocks, ip < pad_blocks + n_row_blocks)
    tm = x_ref.shape[0]

    @pl.when(jnp.logical_and(real, j == 0))
    def _():
        x = x_ref[...]
        ms = jnp.mean(x * x, axis=-1, keepdims=True)
        h_scr[...] = (x * lax.rsqrt(ms + EPS) * g_ref[...]).astype(BF16)

    def norm_rope(y, ss, c, n_group, scale):
        yn = y * lax.rsqrt(ss * (1.0 / n_group) + EPS) * gain_ref[:, c * HEAD_W:(c + 1) * HEAD_W]
        out = yn * cos_ref[...] + pltpu.roll(yn, 64, 1) * sin_ref[...]
        if scale != 1.0:
            out = out * scale
        return out

    def aug_rows():
        row = lax.broadcasted_iota(jnp.int32, (V_ROWS - HEAD_W, tm), 0)
        return jnp.where(row == 0, 1.0, 0.0).astype(BF16)

    def segment(seg):
        acc = jnp.dot(h_scr[...], w_ref[...], preferred_element_type=F32)
        for c in range(N_HEADS):
            y = acc[:, c * HEAD_W:(c + 1) * HEAD_W]
            if seg in (0, 1, 3, 4) and c % 2 == 0:
                pair = acc[:, c * HEAD_W:(c + 2) * HEAD_W]
                ss2 = jnp.dot((pair * pair).astype(BF16), gmat_ref[...], preferred_element_type=F32)
            if seg in (0, 1, 3, 4):
                ss = ss2[:, (c % 2) * HEAD_W:(c % 2 + 1) * HEAD_W]
            if seg == 0:
                qaT_ref[c] = norm_rope(y, ss, c, DIFF_DIM, DIFF_DIM ** -0.5 * LOG2E).T.astype(BF16)
            elif seg == 1:
                ka_ref[:, c * HEAD_W:(c + 1) * HEAD_W] = norm_rope(y, ss, c, DIFF_DIM, 1.0).astype(BF16)
            elif seg == 3:
                qbT_ref[c] = norm_rope(y, ss, c, HEAD_W, HEAD_W ** -0.5 * LOG2E).T.astype(BF16)
            elif seg == 4:
                kb_ref[:, c * HEAD_W:(c + 1) * HEAD_W] = norm_rope(y, ss, c, HEAD_W, 1.0).astype(BF16)
            elif seg == 2:
                vaT_ref[c] = y.T.astype(BF16)
            else:
                vbT_ref[c, 0:HEAD_W, :] = y.T.astype(BF16)
                vbT_ref[c, HEAD_W:V_ROWS, :] = aug_rows()

    for seg in range(N_SEG):
        pl.when(jnp.logical_and(real, j == seg))(functools.partial(segment, seg))

    @pl.when(jnp.logical_and(jnp.logical_not(real), j == 4))
    def _():
        kb_ref[...] = jnp.zeros(kb_ref.shape, BF16)

    @pl.when(jnp.logical_and(jnp.logical_not(real), j == 5))
    def _():
        vbT_ref[...] = jnp.zeros(vbT_ref.shape, BF16)


def _inproj(x, a_norm, w_in_p, gains, gmat, cos_t, sin_t):
    seq = x.shape[0]
    tm = ROW_TILE
    nrb = seq // tm
    pb = DIL_PAD // tm
    seq_p = seq + 2 * DIL_PAD

    def row(ip):
        return jnp.clip(ip - pb, 0, nrb - 1)

    kern = functools.partial(_inproj_kernel, n_row_blocks=nrb, pad_blocks=pb)
    out_shape = (
        jax.ShapeDtypeStruct((N_HEADS, HEAD_W, seq), BF16),
        jax.ShapeDtypeStruct((seq, SEG_W), BF16),
        jax.ShapeDtypeStruct((N_HEADS, HEAD_W, seq), BF16),
        jax.ShapeDtypeStruct((N_HEADS, HEAD_W, seq), BF16),
        jax.ShapeDtypeStruct((seq_p, SEG_W), BF16),
        jax.ShapeDtypeStruct((N_HEADS, V_ROWS, seq_p), BF16),
    )
    in_specs = [
        pl.BlockSpec((tm, D_MODEL), lambda ip, j: (row(ip), 0)),
        pl.BlockSpec((1, D_MODEL), lambda ip, j: (0, 0)),
        pl.BlockSpec((D_MODEL, SEG_W), lambda ip, j: (0, j)),
        pl.BlockSpec((None, 1, SEG_W), lambda ip, j: (j, 0, 0)),
        pl.BlockSpec((None, tm, HEAD_W), lambda ip, j: (j // 3, row(ip), 0)),
        pl.BlockSpec((None, tm, HEAD_W), lambda ip, j: (j // 3, row(ip), 0)),
        pl.BlockSpec((None, 2 * HEAD_W, 2 * HEAD_W), lambda ip, j: (j // 3, 0, 0)),
    ]
    out_specs = (
        pl.BlockSpec((N_HEADS, HEAD_W, tm), lambda ip, j: (0, 0, row(ip))),
        pl.BlockSpec((tm, SEG_W), lambda ip, j: (row(ip), 0)),
        pl.BlockSpec((N_HEADS, HEAD_W, tm), lambda ip, j: (0, 0, row(ip))),
        pl.BlockSpec((N_HEADS, HEAD_W, tm), lambda ip, j: (0, 0, row(ip))),
        pl.BlockSpec((tm, SEG_W), lambda ip, j: (ip, 0)),
        pl.BlockSpec((N_HEADS, V_ROWS, tm), lambda ip, j: (0, 0, ip)),
    )
    return pl.pallas_call(
        kern,
        out_shape=out_shape,
        grid=(nrb + 2 * pb, N_SEG),
        in_specs=in_specs,
        out_specs=out_specs,
        scratch_shapes=[pltpu.VMEM((tm, D_MODEL), BF16)],
        compiler_params=_cparams(("arbitrary", "arbitrary")),
        name="inproj",
    )(x, a_norm.reshape(1, D_MODEL).astype(F32), w_in_p, gains, cos_t, sin_t, gmat)


def _flash_step(s, m_old, acc_ref, vT_t):
    m_new = jnp.maximum(m_old, jnp.max(s, axis=0, keepdims=True))
    alpha = jnp.exp2(m_old - m_new)
    p = jnp.exp2((s - m_new).astype(BF16))
    acc_ref[...] = acc_ref[...] * alpha + jnp.dot(vT_t, p, preferred_element_type=F32)
    return m_new


def _flash_step_sum(s, m_old, l_old, acc_ref, vT_t):
    m_new = jnp.maximum(m_old, jnp.max(s, axis=0, keepdims=True))
    alpha = jnp.exp2(m_old - m_new)
    e = jnp.exp2(s - m_new)
    acc_ref[...] = acc_ref[...] * alpha + jnp.dot(vT_t, e.astype(BF16), preferred_element_type=F32)
    return m_new, l_old * alpha + jnp.sum(e, axis=0, keepdims=True)


def _head_out(aT, gain_row, out_scale):
    a = aT.T
    ms = jnp.mean(a * a, axis=-1, keepdims=True)
    y = a * lax.rsqrt(ms + EPS) * gain_row
    if out_scale != 1.0:
        y = y * out_scale
    return y


def _diff_attn_kernel(*refs, lam_init, tk, n_riders):
    lq1_ref, lk1_ref, lq2_ref, lk2_ref, qT_ref, k_ref, vT_ref, og_ref, gmat_ref = refs[:9]
    rider_in = refs[9:9 + n_riders]
    o_ref = refs[9 + n_riders]
    rider_out = refs[10 + n_riders:10 + 2 * n_riders]
    acc0, acc1, lsum, k0_scr, k1_scr, kmax_scr = refs[10 + 2 * n_riders:]
    tq = qT_ref.shape[1]
    seq = k_ref.shape[0]
    lane0, lane1 = MAP1_LANE, MAP0_LANE

    @pl.when(pl.program_id(1) == 0)
    def _():
        ck = min(1024, seq)

        def kchunk(c, mx):
            rows = pl.ds(pl.multiple_of(c * ck, ck), ck)
            kc = k_ref[rows, :]
            kf = kc.astype(F32)
            n2 = jnp.dot((kf * kf).astype(BF16), gmat_ref[...], preferred_element_type=F32)
            lane = lax.broadcasted_iota(jnp.int32, kc.shape, 1)
            one = jnp.ones_like(kc)
            k0_scr[rows, :] = jnp.where(lane == lane0, one, kc)
            k1_scr[rows, :] = jnp.where(lane == lane1, one, kc)
            return jnp.maximum(mx, jnp.max(n2, axis=0, keepdims=True))

        kn2 = lax.fori_loop(0, seq // ck, kchunk, jnp.zeros((1, HEAD_W), F32))
        kmax_scr[...] = jnp.sqrt(kn2)

    qT = qT_ref[...]
    row = lax.broadcasted_iota(jnp.int32, qT.shape, 0)
    in_map1 = ((row // 32) % 2) == 1
    zero = jnp.zeros_like(qT)
    qf = qT.astype(F32)
    q2 = qf * qf
    nq0 = jnp.sum(jnp.where(in_map1, 0.0, q2), axis=0, keepdims=True)
    nq1 = jnp.sum(jnp.where(in_map1, q2, 0.0), axis=0, keepdims=True)
    kmax = kmax_scr[...]
    b0 = jnp.sqrt(nq0) * kmax[:, MAP0_LANE:MAP0_LANE + 1]
    b1 = jnp.sqrt(nq1) * kmax[:, MAP1_LANE:MAP1_LANE + 1]
    q0 = jnp.where(in_map1, zero, qT)
    q1 = jnp.where(in_map1, qT, zero)
    q0s = jnp.where(row == lane0, jnp.broadcast_to(-b0, qf.shape).astype(BF16), q0)
    q1s = jnp.where(row == lane1, jnp.broadcast_to(-b1, qf.shape).astype(BF16), q1)
    acc0[...] = jnp.zeros(acc0.shape, F32)
    acc1[...] = jnp.zeros(acc1.shape, F32)

    def body(kt, carry):
        l0, l1 = carry
        rows = pl.ds(pl.multiple_of(kt * tk, tk), tk)
        vT_t = vT_ref[:, rows]
        e0 = jnp.exp2(jnp.dot(k0_scr[rows, :], q0s, preferred_element_type=F32))
        acc0[...] += jnp.dot(vT_t, e0.astype(BF16), preferred_element_type=F32)
        e1 = jnp.exp2(jnp.dot(k1_scr[rows, :], q1s, preferred_element_type=F32))
        acc1[...] += jnp.dot(vT_t, e1.astype(BF16), preferred_element_type=F32)
        for w_ref, wb_ref in zip(rider_in, rider_out):
            r = w_ref.shape[0] // (seq // tk)
            part = pl.ds(pl.multiple_of(kt * r, 16), r)
            wb_ref[part, :] = w_ref[part, :].astype(BF16)
        return l0 + jnp.sum(e0, axis=0, keepdims=True), l1 + jnp.sum(e1, axis=0, keepdims=True)

    l_init = jnp.zeros((1, tq), F32)
    l0, l1 = lax.fori_loop(0, seq // tk, body, (l_init, l_init))
    lsum[0:1, :] = l0
    lsum[1:2, :] = l1

    @pl.when(jnp.logical_not(jnp.min(jnp.minimum(l0, l1)) >= ROW_SUM_FLOOR))
    def _():
        acc0[...] = jnp.zeros(acc0.shape, F32)
        acc1[...] = jnp.zeros(acc1.shape, F32)

        def robust(kt, carry):
            m0, l0, m1, l1 = carry
            rows = pl.ds(pl.multiple_of(kt * tk, tk), tk)
            k_t = k_ref[rows, :]
            vT_t = vT_ref[:, rows]
            m0, l0 = _flash_step_sum(jnp.dot(k_t, q0, preferred_element_type=F32), m0, l0, acc0, vT_t)
            m1, l1 = _flash_step_sum(jnp.dot(k_t, q1, preferred_element_type=F32), m1, l1, acc1, vT_t)
            return m0, l0, m1, l1

        m_init = jnp.full((1, tq), NEG, F32)
        _, l0, _, l1 = lax.fori_loop(0, seq // tk, robust, (m_init, l_init, m_init, l_init))
        lsum[0:1, :] = l0
        lsum[1:2, :] = l1

    lam = (jnp.exp(jnp.sum(lq1_ref[...] * lk1_ref[...], axis=-1, keepdims=True))
           - jnp.exp(jnp.sum(lq2_ref[...] * lk2_ref[...], axis=-1, keepdims=True)) + lam_init)
    o0 = acc0[...] * (1.0 / lsum[0:1, :])
    o1 = acc1[...] * (1.0 / lsum[1:2, :])
    aT = o0 - lam * o1
    o_ref[...] = _head_out(aT, og_ref[...], 1.0 - lam_init).astype(o_ref.dtype)


def _diff_attn(qaT, ka, vaT, lq1, lk1, lq2, lk2, og, gmat_diff, lam_init, riders=()):
    seq = ka.shape[0]
    tq = min(ATT_TQ, seq)
    tk = min(ATT_TK, seq)
    nq = seq // tq
    n_steps = N_HEADS * nq
    kern = functools.partial(_diff_attn_kernel, lam_init=lam_init, tk=tk, n_riders=len(riders))
    vec = lambda v: v.reshape(1, -1).astype(F32)
    small = pl.BlockSpec((1, DIFF_DIM), lambda h, qi: (0, 0))
    once = pl.Buffered(1)
    rider_specs = []
    for w in riders:
        n_e, n_r, n_c = w.shape
        per_e = n_steps // n_e
        rider_specs.append(pl.BlockSpec(
            (None, n_r // per_e, n_c),
            lambda h, qi, per_e=per_e: ((h * nq + qi) // per_e, (h * nq + qi) % per_e, 0)))
    outs = pl.pallas_call(
        kern,
        out_shape=(jax.ShapeDtypeStruct((seq, SEG_W), BF16),
                   *[jax.ShapeDtypeStruct(w.shape, BF16) for w in riders]),
        grid=(N_HEADS, nq),
        in_specs=[small, small, small, small,
                  pl.BlockSpec((None, HEAD_W, tq), lambda h, qi: (h, 0, qi)),
                  pl.BlockSpec((seq, HEAD_W), lambda h, qi: (0, h), pipeline_mode=once),
                  pl.BlockSpec((None, HEAD_W, seq), lambda h, qi: (h, 0, 0), pipeline_mode=once),
                  pl.BlockSpec((1, HEAD_W), lambda h, qi: (0, 0)),
                  pl.BlockSpec((HEAD_W, HEAD_W), lambda h, qi: (0, 0)),
                  *rider_specs],
        out_specs=(pl.BlockSpec((tq, HEAD_W), lambda h, qi: (qi, h)), *rider_specs),
        scratch_shapes=[pltpu.VMEM((HEAD_W, tq), F32), pltpu.VMEM((HEAD_W, tq), F32),
                        pltpu.VMEM((8, tq), F32),
                        pltpu.VMEM((seq, HEAD_W), BF16), pltpu.VMEM((seq, HEAD_W), BF16),
                        pltpu.VMEM((1, HEAD_W), F32)],
        compiler_params=_cparams(("arbitrary", "arbitrary")),
        name="diff_attn",
    )(vec(lq1), vec(lk1), vec(lq2), vec(lk2), qaT, ka, vaT, vec(og), gmat_diff, *riders)
    return outs[0], tuple(outs[1:])


def _dil_chunks(tq):
    chunks, off = [], 0
    for _, dil in DIL_BRANCHES:
        pad = -(-(N_SIDE * dil) // 128) * 128
        total = tq + 2 * pad
        c0 = 0
        while c0 < total:
            nk = min(DIL_CHUNK, total - c0)
            chunks.append((dil, c0 - pad, nk, off))
            off += nk
            c0 += nk
    return chunks, off


def _dil_bias(tq):
    chunks, total = _dil_chunks(tq)
    bias = np.full((total, tq), NEG, np.float32)
    col = np.arange(tq)[None, :]
    for dil, rel, nk, off in chunks:
        delta = rel + np.arange(nk)[:, None] - col
        ok = (np.abs(delta) <= N_SIDE * dil) & (delta % dil == 0)
        bias[off:off + nk][ok] = 0.0
    return jnp.asarray(bias)


def _dil_attn_kernel(qT_ref, k_ref, vT_ref, bias_ref, og_ref, gmat_ref, o_ref, acc, kmax_scr, *, seq, chunks):
    tq = acc.shape[1]
    n_sub = qT_ref.shape[1] // tq
    base = pl.program_id(1) * (tq * n_sub)

    @pl.when(pl.program_id(1) == 0)
    def _():
        ck = 1024

        def kchunk(c, mx):
            kf = k_ref[pl.ds(pl.multiple_of(c * ck, ck), ck), :].astype(F32)
            n2 = jnp.dot((kf * kf).astype(BF16), gmat_ref[...], preferred_element_type=F32)
            return jnp.maximum(mx, jnp.max(n2, axis=0, keepdims=True))

        kn2 = lax.fori_loop(0, k_ref.shape[0] // ck, kchunk, jnp.zeros((1, HEAD_W), F32))
        kmax_scr[...] = jnp.sqrt(kn2)

    def windows(i0):
        for dil, rel, nk, off in chunks:
            start = pl.multiple_of(i0 + (DIL_PAD + rel), 128)
            yield rel, nk, off, k_ref[pl.ds(start, nk), :], vT_ref[:, pl.ds(start, nk)]

    def finish(t, num):
        oT = num[0:HEAD_W, :] * (1.0 / num[HEAD_W:HEAD_W + 1, :])
        o_ref[t * tq:(t + 1) * tq, :] = _head_out(oT, og_ref[...], 1.0).astype(o_ref.dtype)

    lmin = None
    for t in range(n_sub):
        qT = qT_ref[:, t * tq:(t + 1) * tq]
        qf = qT.astype(F32)
        shift = jnp.sqrt(jnp.sum(qf * qf, axis=0, keepdims=True)) * kmax_scr[:, 0:1]
        num = jnp.zeros((V_ROWS, tq), F32)
        for rel, nk, off, k_t, vT_t in windows(base + t * tq):
            s = jnp.dot(k_t, qT, preferred_element_type=F32) - shift + bias_ref[off:off + nk, :]
            num = num + jnp.dot(vT_t, jnp.exp2(s).astype(BF16), preferred_element_type=F32)
        finish(t, num)
        l_t = jnp.min(num[HEAD_W:HEAD_W + 1, :])
        lmin = l_t if lmin is None else jnp.minimum(lmin, l_t)

    @pl.when(jnp.logical_not(lmin >= ROW_SUM_FLOOR))
    def _():
        for t in range(n_sub):
            i0 = base + t * tq
            qT = qT_ref[:, t * tq:(t + 1) * tq]
            acc[...] = jnp.zeros(acc.shape, F32)
            m = jnp.full((1, tq), NEG, F32)
            for rel, nk, off, k_t, vT_t in windows(i0):
                s = jnp.dot(k_t, qT, preferred_element_type=F32)
                kpos = lax.broadcasted_iota(jnp.int32, (nk, tq), 0) + (i0 + rel)
                valid = jnp.logical_and(kpos >= 0, kpos < seq)
                s = jnp.where(valid, s + bias_ref[off:off + nk, :], NEG)
                m = _flash_step(s, m, acc, vT_t)
            finish(t, acc[...])


def _dil_attn(qbT, kb, vbT, og, gmat_ones):
    seq = qbT.shape[2]
    seq_p = kb.shape[0]
    tq = min(DIL_TQ, seq)
    n_sub = DIL_SUBTILES
    chunks, _ = _dil_chunks(tq)
    bias = _dil_bias(tq)
    kern = functools.partial(_dil_attn_kernel, seq=seq, chunks=chunks)
    return pl.pallas_call(
        kern,
        out_shape=jax.ShapeDtypeStruct((seq, SEG_W), BF16),
        grid=(N_HEADS, seq // (tq * n_sub)),
        in_specs=[pl.BlockSpec((None, HEAD_W, tq * n_sub), lambda h, qi: (h, 0, qi)),
                  pl.BlockSpec((seq_p, HEAD_W), lambda h, qi: (0, h)),
                  pl.BlockSpec((None, V_ROWS, seq_p), lambda h, qi: (h, 0, 0)),
                  pl.BlockSpec(bias.shape, lambda h, qi: (0, 0)),
                  pl.BlockSpec((1, HEAD_W), lambda h, qi: (0, 0)),
                  pl.BlockSpec((HEAD_W, HEAD_W), lambda h, qi: (0, 0))],
        out_specs=pl.BlockSpec((tq * n_sub, HEAD_W), lambda h, qi: (qi, h)),
        scratch_shapes=[pltpu.VMEM((V_ROWS, tq), F32), pltpu.VMEM((1, HEAD_W), F32)],
        compiler_params=_cparams(("arbitrary", "arbitrary")),
        name="dil_attn",
    )(qbT, kb, vbT, bias, og.reshape(1, HEAD_W).astype(F32), gmat_ones)


def _outproj_kernel(*refs, with_router):
    if with_router:
        a_ref, b_ref, w_ref, x_ref, g_ref, r_hi_ref, r_lo_ref, x1_ref, h_ref, route_ref = refs
    else:
        a_ref, b_ref, w_ref, x_ref, g_ref, x1_ref, h_ref = refs
    acc = (jnp.dot(a_ref[...], w_ref[0:SEG_W, :], preferred_element_type=F32)
           + jnp.dot(b_ref[...], w_ref[SEG_W:2 * SEG_W, :], preferred_element_type=F32))
    x1 = x_ref[...] + acc
    x1_ref[...] = x1
    ms = jnp.mean(x1 * x1, axis=-1, keepdims=True)
    hn = x1 * lax.rsqrt(ms + EPS) * g_ref[...]
    if not with_router:
        h_ref[...] = hn.astype(BF16)
    else:
        tm = hn.shape[0]
        for s in range(ROW_CHUNKS):
            h_ref[pl.ds(s, tm, stride=ROW_CHUNKS), :] = hn[:, s * HEAD_W:(s + 1) * HEAD_W]
        h_hi = hn.astype(BF16)
        h_lo = (hn - h_hi.astype(F32)).astype(BF16)
        logits = (jnp.dot(h_hi, r_hi_ref[...], preferred_element_type=F32)
                  + jnp.dot(h_hi, r_lo_ref[...], preferred_element_type=F32)
                  + jnp.dot(h_lo, r_hi_ref[...], preferred_element_type=F32))
        lane = lax.broadcasted_iota(jnp.int32, logits.shape, 1)
        ninf = jnp.float32(-jnp.inf)
        lg = jnp.where(lane < N_EXPERTS, logits, ninf)
        v1 = jnp.max(lg, axis=-1, keepdims=True)
        i1 = jnp.min(jnp.where(lg == v1, lane, HEAD_W), axis=-1, keepdims=True)
        lg2 = jnp.where(lane == i1, ninf, lg)
        v2 = jnp.max(lg2, axis=-1, keepdims=True)
        i2 = jnp.min(jnp.where(lg2 == v2, lane, HEAD_W), axis=-1, keepdims=True)
        g1 = 1.0 / (1.0 + jnp.exp(v2 - v1))
        g2 = 1.0 - g1
        route_ref[...] = jnp.where(lane == 0, i1.astype(F32),
                                   jnp.where(lane == 1, i2.astype(F32),
                                             jnp.where(lane == 2, g1, jnp.where(lane == 3, g2, 0.0))))


def _outproj(out_a, out_b, w_out, x, f_norm, router=None):
    seq = x.shape[0]
    tm = ROW_TILE
    with_router = router is not None
    in_specs = [pl.BlockSpec((tm, SEG_W), lambda i: (i, 0)),
                pl.BlockSpec((tm, SEG_W), lambda i: (i, 0)),
                pl.BlockSpec((D_MODEL, D_MODEL), lambda i: (0, 0)),
                pl.BlockSpec((tm, D_MODEL), lambda i: (i, 0)),
                pl.BlockSpec((1, D_MODEL), lambda i: (0, 0))]
    args = [out_a, out_b, w_out.astype(BF16), x, f_norm.reshape(1, D_MODEL).astype(F32)]
    out_shape = [jax.ShapeDtypeStruct((seq, D_MODEL), F32)]
    out_specs = [pl.BlockSpec((tm, D_MODEL), lambda i: (i, 0))]
    if not with_router:
        out_shape.append(jax.ShapeDtypeStruct((seq, D_MODEL), BF16))
        out_specs.append(pl.BlockSpec((tm, D_MODEL), lambda i: (i, 0)))
    else:
        out_shape.append(jax.ShapeDtypeStruct((seq * ROW_CHUNKS, HEAD_W), F32))
        out_specs.append(pl.BlockSpec((tm * ROW_CHUNKS, HEAD_W), lambda i: (i, 0)))
        r = jnp.zeros((D_MODEL, HEAD_W), F32).at[:, :N_EXPERTS].set(router.astype(F32))
        r_hi = r.astype(BF16)
        r_lo = (r - r_hi.astype(F32)).astype(BF16)
        in_specs += [pl.BlockSpec((D_MODEL, HEAD_W), lambda i: (0, 0))] * 2
        args += [r_hi, r_lo]
        out_shape.append(jax.ShapeDtypeStruct((seq, HEAD_W), F32))
        out_specs.append(pl.BlockSpec((tm, HEAD_W), lambda i: (i, 0)))
    return pl.pallas_call(
        functools.partial(_outproj_kernel, with_router=with_router),
        out_shape=tuple(out_shape),
        grid=(seq // tm,),
        in_specs=in_specs,
        out_specs=tuple(out_specs),
        compiler_params=_cparams(("arbitrary",)),
        name="outproj",
    )(*args)


def _swiglu_part(h, wg, wu, wd):
    g = jnp.dot(h, wg, preferred_element_type=F32)
    u = jnp.dot(h, wu, preferred_element_type=F32)
    act = (g * (1.0 / (1.0 + jnp.exp(-g))) * u).astype(BF16)
    return jnp.dot(act, wd, preferred_element_type=F32)


def _dense_ffn_kernel(h_ref, x_ref, wg_ref, wu_ref, wd_ref, o_ref):
    @pl.when(pl.program_id(1) == 0)
    def _():
        o_ref[...] = x_ref[...]

    o_ref[...] += _swiglu_part(h_ref[...], wg_ref[...], wu_ref[...], wd_ref[...])


def _dense_ffn(h, x1, wg, wu, wd):
    seq = x1.shape[0]
    tm, tf = ROW_TILE, FF_TILE
    return pl.pallas_call(
        _dense_ffn_kernel,
        out_shape=jax.ShapeDtypeStruct((seq, D_MODEL), F32),
        grid=(seq // tm, D_FF // tf),
        in_specs=[pl.BlockSpec((tm, D_MODEL), lambda i, j: (i, 0)),
                  pl.BlockSpec((tm, D_MODEL), lambda i, j: (i, 0)),
                  pl.BlockSpec((D_MODEL, tf), lambda i, j: (0, j)),
                  pl.BlockSpec((D_MODEL, tf), lambda i, j: (0, j)),
                  pl.BlockSpec((tf, D_MODEL), lambda i, j: (j, 0))],
        out_specs=pl.BlockSpec((tm, D_MODEL), lambda i, j: (i, 0)),
        compiler_params=_cparams(("arbitrary", "arbitrary")),
        name="dense_ffn",
    )(h, x1, wg.astype(BF16), wu.astype(BF16), wd.astype(BF16))


def _moe_ffn_kernel(be_ref, nused_ref, src_ref, src_next_ref, dst_ref, dst_prev_ref,
                    h3_ref, wg_ref, wu_ref, wd_ref, y_ref, x3, xb, acc, o3, gsem, ssem):
    b = pl.program_id(0)
    j = pl.program_id(1)
    nb = pl.num_programs(0)
    nj = pl.num_programs(1)
    tm = xb.shape[0]
    n_used = nused_ref[0]
    used = b < n_used

    def slab(ref, row0):
        return ref.at[pl.ds(pl.multiple_of(row0, 8), ROW_CHUNKS), :]

    def start_gather(idx_ref, slot):
        def start(r, c):
            pltpu.make_async_copy(slab(h3_ref, idx_ref[0, r]), slab(x3.at[slot], r * SLAB_PITCH),
                                  gsem.at[slot]).start()
            return c

        lax.fori_loop(0, tm, start, 0)

    def scatter_copy(idx_ref, r):
        return pltpu.make_async_copy(slab(o3, r * SLAB_PITCH), slab(y_ref, idx_ref[0, r]), ssem)

    def wait_gather(slot):
        n_rows = tm * ROW_CHUNKS
        pltpu.make_async_copy(h3_ref.at[pl.ds(0, n_rows), :], x3.at[slot, pl.ds(0, n_rows), :],
                              gsem.at[slot]).wait()

    def wait_scatter():
        n_rows = tm * ROW_CHUNKS
        pltpu.make_async_copy(o3.at[pl.ds(0, n_rows), :], y_ref.at[pl.ds(0, n_rows), :], ssem).wait()

    def flush_scatter(idx_ref):
        def start(r, c):
            scatter_copy(idx_ref, r).start()
            return c

        lax.fori_loop(0, tm, start, 0)
        wait_scatter()

    @pl.when(jnp.logical_and(j == 0, jnp.logical_and(b == 0, used)))
    def _():
        start_gather(src_ref, 0)
        o3[...] = jnp.zeros(o3.shape, F32)

    @pl.when(jnp.logical_and(j == 0, jnp.logical_and(b >= 1, b == n_used)))
    def _():
        flush_scatter(dst_prev_ref)

    @pl.when(jnp.logical_and(j == 0, jnp.where(b == 0, used, b - 1 < n_used)))
    def _():
        wait_gather(b % 2)

    @pl.when(jnp.logical_and(j == 0, used))
    def _():
        slot = b % 2
        for s in range(ROW_CHUNKS):
            xb[:, s * HEAD_W:(s + 1) * HEAD_W] = x3[slot, pl.ds(s, tm, stride=SLAB_PITCH), :].astype(BF16)
        acc[...] = jnp.zeros(acc.shape, F32)

    @pl.when(used)
    def _():
        per_step = tm // D_EXPERT_STEPS
        nslot = (b + 1) % 2

        def move(r):
            pltpu.make_async_copy(slab(h3_ref, src_next_ref[0, r]), slab(x3.at[nslot], r * SLAB_PITCH),
                                  gsem.at[nslot]).start()
            scatter_copy(dst_prev_ref, r).start(priority=1)

        for i in range(per_step):
            move(j * per_step + i)

        @pl.when(j == 0)
        def _():
            for r in range(per_step * D_EXPERT_STEPS, tm):
                move(r)

        acc[...] += _swiglu_part(xb[...], wg_ref[...], wu_ref[...], wd_ref[...])

    @pl.when(jnp.logical_and(j == nj - 1, used))
    def _():
        wait_scatter()
        for s in range(ROW_CHUNKS):
            o3[pl.ds(s, tm, stride=SLAB_PITCH), :] = acc[:, s * HEAD_W:(s + 1) * HEAD_W]

        @pl.when(b == nb - 1)
        def _():
            flush_scatter(dst_ref)
            wait_gather((b + 1) % 2)


def _moe_ffn(h3, src_idx, dst_idx, dst_prev_idx, blk_expert, n_used, wg, wu, wd):
    seq = h3.shape[0] // ROW_CHUNKS
    n_blocks = src_idx.shape[0]
    tm, tf = MOE_TILE, MOE_FF_TILE
    nj = D_EXPERT // tf

    def jj(b, j, be, nu):
        return jnp.where(b < nu[0], j, nj - 1)

    def idx_spec(shift):
        return pl.BlockSpec((None, 1, tm), lambda b, j, be, nu: (jnp.clip(b + shift, 0, n_blocks - 1), 0, 0),
                            memory_space=pltpu.SMEM)

    return pl.pallas_call(
        _moe_ffn_kernel,
        out_shape=jax.ShapeDtypeStruct(((2 * seq + tm) * ROW_CHUNKS, HEAD_W), F32),
        grid_spec=pltpu.PrefetchScalarGridSpec(
            num_scalar_prefetch=2,
            grid=(n_blocks, nj),
            in_specs=[idx_spec(0), idx_spec(1), idx_spec(0), idx_spec(0),
                      pl.BlockSpec(memory_space=pl.ANY),
                      pl.BlockSpec((None, D_MODEL, tf), lambda b, j, be, nu: (be[b], 0, jj(b, j, be, nu))),
                      pl.BlockSpec((None, D_MODEL, tf), lambda b, j, be, nu: (be[b], 0, jj(b, j, be, nu))),
                      pl.BlockSpec((None, tf, D_MODEL), lambda b, j, be, nu: (be[b], jj(b, j, be, nu), 0))],
            out_specs=pl.BlockSpec(memory_space=pl.ANY),
            scratch_shapes=[pltpu.VMEM((2, tm * SLAB_PITCH, HEAD_W), F32),
                            pltpu.VMEM((tm, D_MODEL), BF16),
                            pltpu.VMEM((tm, D_MODEL), F32),
                            pltpu.VMEM((tm * SLAB_PITCH, HEAD_W), F32),
                            pltpu.SemaphoreType.DMA((2,)),
                            pltpu.SemaphoreType.DMA(())]),
        compiler_params=_cparams(("arbitrary", "arbitrary")),
        name="moe_ffn",
    )(blk_expert, n_used, src_idx, src_idx, dst_idx, dst_prev_idx, h3, wg, wu, wd)


def _combine_kernel(x_ref, ya_ref, yb_ref, route_ref, o_ref):
    r = route_ref[...]
    g1 = jnp.broadcast_to(r[:, 2:3], (r.shape[0], HEAD_W))
    g2 = jnp.broadcast_to(r[:, 3:4], (r.shape[0], HEAD_W))
    tm = r.shape[0]
    for s in range(ROW_CHUNKS):
        cols = slice(s * HEAD_W, (s + 1) * HEAD_W)
        rows = pl.ds(s, tm, stride=ROW_CHUNKS)
        o_ref[:, cols] = x_ref[:, cols] + g1 * ya_ref[rows, :] + g2 * yb_ref[rows, :]


def _moe_combine(x1, y, route):
    seq = x1.shape[0]
    tm = ROW_TILE
    nb = seq // tm
    big = pl.BlockSpec((tm, D_MODEL), lambda i: (i, 0))
    return pl.pallas_call(
        _combine_kernel,
        out_shape=jax.ShapeDtypeStruct((seq, D_MODEL), F32),
        grid=(nb,),
        in_specs=[big,
                  pl.BlockSpec((tm * ROW_CHUNKS, HEAD_W), lambda i: (i, 0)),
                  pl.BlockSpec((tm * ROW_CHUNKS, HEAD_W), lambda i: (i + nb, 0)),
                  pl.BlockSpec((tm, HEAD_W), lambda i: (i, 0))],
        out_specs=big,
        compiler_params=_cparams(("arbitrary",)),
        name="moe_combine",
    )(x1, y, y, route)


def _moe_plan(route, seq):
    tm = MOE_TILE
    n_assign = 2 * seq
    n_blocks = n_assign // tm + N_EXPERTS
    e = route[:, 0:2].astype(jnp.int32).reshape(-1)
    onehot = (e[:, None] == jnp.arange(N_EXPERTS, dtype=jnp.int32)[None, :]).astype(jnp.int32)
    csum = jnp.cumsum(onehot, axis=0)
    counts = csum[-1]
    rank = jnp.sum((csum - onehot) * onehot, axis=1)
    nblk = (counts + tm - 1) // tm
    cum_blk = jnp.cumsum(nblk)
    pad_start = (cum_blk - nblk) * tm
    pos = pad_start[e] + rank
    assign = jnp.arange(n_assign, dtype=jnp.int32)
    slot_assign = jnp.full((n_blocks * tm,), -1, jnp.int32).at[pos].set(
        assign, unique_indices=True, mode='promise_in_bounds')
    real = slot_assign >= 0
    tok = jnp.maximum(slot_assign, 0) // 2
    in_block = jnp.arange(n_blocks * tm, dtype=jnp.int32) % tm
    src_idx = tok * ROW_CHUNKS
    dst_idx = jnp.where(real, (slot_assign % 2) * seq + tok, n_assign + in_block) * ROW_CHUNKS
    src_idx = src_idx.reshape(n_blocks, 1, tm)
    dst_idx = dst_idx.reshape(n_blocks, 1, tm)
    scratch_blk = ((n_assign + jnp.arange(tm, dtype=jnp.int32)) * ROW_CHUNKS).reshape(1, 1, tm)
    dst_prev_idx = jnp.concatenate([scratch_blk, dst_idx[:-1]], axis=0)
    n_used = cum_blk[-1].astype(jnp.int32)
    blk = jnp.minimum(jnp.arange(n_blocks, dtype=jnp.int32), n_used - 1)
    blk_expert = jnp.sum((cum_blk[None, :] <= blk[:, None]).astype(jnp.int32), axis=1)
    blk_expert = jnp.minimum(blk_expert, N_EXPERTS - 1)
    return src_idx, dst_idx, dst_prev_idx, blk_expert, n_used.reshape(1)


def _moe(h3, x1, route, wg, wu, wd):
    seq = x1.shape[0]
    src_idx, dst_idx, dst_prev_idx, blk_expert, n_used = _moe_plan(route, seq)
    y = _moe_ffn(h3, src_idx, dst_idx, dst_prev_idx, blk_expert, n_used, wg, wu, wd)
    return _moe_combine(x1, y, route)


def _layer(x, i, a_norm, w_in, dqn, dkn, lq1, lk1, lq2, lk2, don, bqn, bkn, bon, w_out, f_norm, ffn, riders=()):
    seq = x.shape[0]
    lam_init = 0.8 - 0.6 * math.exp(-0.3 * i)
    gains, gmat, cos_t, sin_t = _segment_tables(seq, dqn, dkn, bqn, bkn)
    qaT, ka, vaT, qbT, kb, vbT = _inproj(x, a_norm, _permute_w_in(w_in), gains, gmat, cos_t, sin_t)
    out_a, casted = _diff_attn(qaT, ka, vaT, lq1, lk1, lq2, lk2, don, gmat[0, :HEAD_W, :HEAD_W], lam_init, riders)
    out_b = _dil_attn(qbT, kb, vbT, bon, gmat[1, :HEAD_W, :HEAD_W])
    if len(ffn) == 3:
        x1, h = _outproj(out_a, out_b, w_out, x, f_norm)
        return _dense_ffn(h, x1, *ffn), casted
    router, wg, wu, wd = ffn
    x1, h3, route = _outproj(out_a, out_b, w_out, x, f_norm, router)
    return _moe(h3, x1, route, wg, wu, wd), casted


def kernel(x, attn_norm_0, w_in_0, diff_q_norm_0, diff_k_norm_0, diff_lam_q1_0, diff_lam_k1_0, diff_lam_q2_0, diff_lam_k2_0, diff_out_norm_0, dil_q_norm_0, dil_k_norm_0, dil_out_norm_0, w_out_0, ffn_norm_0, ffn_w_gate_0, ffn_w_up_0, ffn_w_down_0, attn_norm_1, w_in_1, diff_q_norm_1, diff_k_norm_1, diff_lam_q1_1, diff_lam_k1_1, diff_lam_q2_1, diff_lam_k2_1, diff_out_norm_1, dil_q_norm_1, dil_k_norm_1, dil_out_norm_1, w_out_1, ffn_norm_1, router_1, moe_w_gate_1, moe_w_up_1, moe_w_down_1):
    b, seq, d = x.shape
    assert b == 1 and d == D_MODEL and seq % (DIL_TQ * DIL_SUBTILES) == 0 and seq % ATT_TQ == 0
    assert MOE_TILE == ROW_TILE and DIL_PAD % ROW_TILE == 0
    xs = x.reshape(seq, d)
    xs, moe_w = _layer(xs, 0, attn_norm_0, w_in_0, diff_q_norm_0, diff_k_norm_0, diff_lam_q1_0, diff_lam_k1_0,
                       diff_lam_q2_0, diff_lam_k2_0, diff_out_norm_0, dil_q_norm_0, dil_k_norm_0, dil_out_norm_0,
                       w_out_0, ffn_norm_0, (ffn_w_gate_0, ffn_w_up_0, ffn_w_down_0),
                       riders=(moe_w_gate_1, moe_w_up_1, moe_w_down_1))
    xs, _ = _layer(xs, 1, attn_norm_1, w_in_1, diff_q_norm_1, diff_k_norm_1, diff_lam_q1_1, diff_lam_k1_1,
                   diff_lam_q2_1, diff_lam_k2_1, diff_out_norm_1, dil_q_norm_1, dil_k_norm_1, dil_out_norm_1,
                   w_out_1, ffn_norm_1, (router_1, *moe_w))
    return xs.reshape(b, seq, d)
```
